```python
import jax, jax.numpy as jnp
from jax import lax
import numpy as np

D_MODEL = 1024
BATCH = 8
SEQ = 2048
DEPTH = 2
DEC_BATCH = 32
DEC_SEQ = 1
PAST_LEN = 8192
PAGE_SIZE = 128

H_A = 8
DH_A = 64
W_A = H_A * DH_A
FORGET_BIAS = 2.0
H_B = 8
KV_B = 2
DH_B = 64
G_B = H_B // KV_B
W_B = H_B * DH_B
H_I = 8
D_I = 64
TOPK_MAX = 256
D_C = 512
CONV_W = 31
P_DIM = 256
N_BRANCH = 3
Q_BLOCK = 128
ROPE_THETA = 500000.0
ROT_DIM = DH_B // 4
DN_ALPHA = (2 * DEPTH) ** 0.25
DN_BETA = (8 * DEPTH) ** -0.25
LN_EPS = 1e-5
SPLIT_SIZES = (W_A, W_A, W_A, H_A, W_A,
               W_B, KV_B * DH_B, KV_B * DH_B, H_I * D_I, D_I, H_I, W_B,
               2 * D_C, D_C,
               N_BRANCH * D_MODEL)
D_IN = sum(SPLIT_SIZES)

kernel_name = "fox_dsa_conformer_gated_hybrid_step"


def _layer_norm(x, g, b):
    xf = x.astype(jnp.float32)
    mu = jnp.mean(xf, -1, keepdims=True)
    var = jnp.mean(jnp.square(xf - mu), -1, keepdims=True)
    y = (xf - mu) * lax.rsqrt(var + LN_EPS) * g.astype(jnp.float32) + b.astype(jnp.float32)
    return y.astype(x.dtype)


def _partial_rope(x, pos):
    half = ROT_DIM // 2
    inv = ROPE_THETA ** (-jnp.arange(half, dtype=jnp.float32) / half)
    ang = pos.astype(jnp.float32)[:, None] * inv[None, :]
    cos = jnp.cos(ang)[:, None, :]
    sin = jnp.sin(ang)[:, None, :]
    xf = x.astype(jnp.float32)
    x1 = xf[..., :half]
    x2 = xf[..., half:ROT_DIM]
    y = jnp.concatenate([x1 * cos - x2 * sin, x2 * cos + x1 * sin, xf[..., ROT_DIM:]], -1)
    return y.astype(x.dtype)


def _project(x, w_in):
    z = jnp.einsum('btd,de->bte', x, w_in)
    parts = []
    start = 0
    for n in SPLIT_SIZES:
        parts.append(z[..., start:start + n])
        start += n
    return parts


def _to_blocks(a):
    B, S = a.shape[:2]
    return jnp.moveaxis(a.reshape((B, S // Q_BLOCK, Q_BLOCK) + a.shape[2:]), 1, 0)


def _from_blocks(a):
    nb, B, Q = a.shape[:3]
    return jnp.moveaxis(a, 0, 1).reshape((B, nb * Q) + a.shape[3:])


def _gather_rows(rows, idx):
    return jax.vmap(lambda r, i: r[i])(rows, idx)


def _forget_log(fa, b_f):
    return jax.nn.log_sigmoid(fa.astype(jnp.float32) + b_f.astype(jnp.float32))


def _fox_attend(q, k, v, cq, ck, q_pos, k_pos):
    s = jnp.einsum('bqhd,bkhd->bhqk', q, k).astype(jnp.float32) * (DH_A ** -0.5)
    s = s + jnp.transpose(cq, (0, 2, 1))[..., :, None] - jnp.transpose(ck, (0, 2, 1))[..., None, :]
    s = jnp.where(k_pos[None, :] <= q_pos[:, None], s, -jnp.inf)
    p = jax.nn.softmax(s, axis=-1)
    return jnp.einsum('bhqk,bkhd->bqhd', p.astype(v.dtype), v)


def _index_topk(qi, wi, ki, q_pos, k_pos, n_top):
    r = jax.nn.relu(jnp.einsum('bthd,bld->bthl', qi, ki).astype(jnp.float32) * (D_I ** -0.5))
    sc = jnp.einsum('bthl,bth->btl', r, wi.astype(jnp.float32) * (H_I ** -0.5))
    sc = jnp.where(k_pos[None, None, :] <= q_pos[None, :, None], sc, -jnp.inf)
    _, idx = lax.top_k(sc, n_top)
    valid = idx <= q_pos[None, :, None]
    return idx, valid


def _sparse_attend(q, kv_sel, valid):
    B, T = q.shape[:2]
    qg = q.reshape(B, T, KV_B, G_B, DH_B)
    s = jnp.einsum('btngd,btknd->btngk', qg, kv_sel[:, :, :, 0]).astype(jnp.float32) * (DH_B ** -0.5)
    s = jnp.where(valid[:, :, None, None, :], s, -jnp.inf)
    p = jax.nn.softmax(s, axis=-1)
    o = jnp.einsum('btngk,btknd->btngd', p.astype(kv_sel.dtype), kv_sel[:, :, :, 1])
    return o.reshape(B, T, W_B)


def _glu(glu):
    a = glu[..., :D_C]
    g = glu[..., D_C:]
    return a * jax.nn.sigmoid(g)


def _conv_branch(u_ext, wl):
    y = lax.conv_general_dilated(u_ext, wl['conv_w'][:, None, :], window_strides=(1,), padding='VALID',
                                 dimension_numbers=('NWC', 'WIO', 'NWC'), feature_group_count=D_C)
    y = y + wl['conv_b']
    return jax.nn.silu(_layer_norm(y, wl['conv_ln_g'], wl['conv_ln_b']))


def _merge(x, p, ya, yb, yc, ga, gb, gc, mg, wl):
    ua = jnp.einsum('bte,ed->btd', ya * jax.nn.silu(ga), wl['w_a'])
    ub = jnp.einsum('bte,ed->btd', yb * jax.nn.silu(gb), wl['w_b'])
    uc = jnp.einsum('bte,ed->btd', yc * jax.nn.silu(gc), wl['w_c'])
    gates = jax.nn.sigmoid(mg)
    m = gates[..., :D_MODEL] * ua + gates[..., D_MODEL:2 * D_MODEL] * ub + gates[..., 2 * D_MODEL:] * uc
    mix = jnp.einsum('btd,de->bte', m, wl['w_out'])
    h = _layer_norm(DN_ALPHA * x + mix, wl['ln_g'], wl['ln_b'])
    pe = jnp.einsum('btp,pd->btd', p, wl['w_pe'])
    return h + jax.nn.sigmoid(jnp.einsum('btd,de->bte', h, wl['w_pe_gate'])) * pe


def _prompt_layer(x, p, wl):
    B, S, _ = x.shape
    qa, ka, va, fa, ga, qb, kb, vb, qi, ki, wi, gb, glu, gc, mg = _project(x, wl['w_in'])
    pos = jnp.arange(S, dtype=jnp.int32)
    pos_blocks = pos.reshape(-1, Q_BLOCK)
    qa = qa.reshape(B, S, H_A, DH_A)
    ka = ka.reshape(B, S, H_A, DH_A)
    va = va.reshape(B, S, H_A, DH_A)
    logf = _forget_log(fa, wl['b_f'])
    cum = jnp.cumsum(logf, axis=1)

    def fox_block(blk):
        q_blk, c_blk, p_blk = blk
        return _fox_attend(q_blk, ka, va, c_blk, cum, p_blk, pos)

    ya = _from_blocks(lax.map(fox_block, (_to_blocks(qa), _to_blocks(cum), pos_blocks))).reshape(B, S, W_A)
    qb = _partial_rope(qb.reshape(B, S, H_B, DH_B), pos)
    kv_b = jnp.stack([_partial_rope(kb.reshape(B, S, KV_B, DH_B), pos), vb.reshape(B, S, KV_B, DH_B)], axis=2)
    qi = _partial_rope(qi.reshape(B, S, H_I, D_I), pos)
    ki = _partial_rope(ki.reshape(B, S, 1, D_I), pos)[:, :, 0]
    n_top = min(TOPK_MAX, S // 4)

    def dsa_block(blk):
        q_blk, qi_blk, wi_blk, p_blk = blk
        idx, valid = _index_topk(qi_blk, wi_blk, ki, p_blk, pos, n_top)
        return _sparse_attend(q_blk, _gather_rows(kv_b, idx), valid)

    yb = _from_blocks(lax.map(dsa_block, (_to_blocks(qb), _to_blocks(qi), _to_blocks(wi), pos_blocks)))
    u = _glu(glu)
    u_ext = jnp.concatenate([jnp.zeros((B, CONV_W - 1, D_C), u.dtype), u], axis=1)
    yc = _conv_branch(u_ext, wl)
    y = _merge(x, p, ya, yb, yc, ga, gb, gc, mg, wl)
    return y, jnp.stack([ka, va], axis=2), logf, kv_b, ki, u_ext[:, -(CONV_W - 1):]


def _sample_layer(x, p, wl, page_table, c_fox_kv, c_fox_logf, c_dsa_kv, c_dsa_ik, s_conv):
    Bd, T, _ = x.shape
    P = page_table.shape[1] * PAGE_SIZE
    pos = P + jnp.arange(T, dtype=jnp.int32)
    kpos = jnp.arange(P + T, dtype=jnp.int32)
    qa, ka, va, fa, ga, qb, kb, vb, qi, ki, wi, gb, glu, gc, mg = _project(x, wl['w_in'])
    qa = qa.reshape(Bd, T, H_A, DH_A)
    kv_a = jnp.stack([ka.reshape(Bd, T, H_A, DH_A), va.reshape(Bd, T, H_A, DH_A)], axis=2)
    logf = _forget_log(fa, wl['b_f'])
    past_kv = c_fox_kv[page_table].reshape(Bd, P, 2, H_A, DH_A)
    past_logf = c_fox_logf[page_table].reshape(Bd, P, H_A).astype(jnp.float32)
    kv_all = jnp.concatenate([past_kv, kv_a.astype(past_kv.dtype)], axis=1)
    cum = jnp.cumsum(jnp.concatenate([past_logf, logf], axis=1), axis=1)
    ya = _fox_attend(qa, kv_all[:, :, 0], kv_all[:, :, 1], cum[:, P:], cum, pos, kpos).reshape(Bd, T, W_A)
    qb = _partial_rope(qb.reshape(Bd, T, H_B, DH_B), pos)
    kv_b = jnp.stack([_partial_rope(kb.reshape(Bd, T, KV_B, DH_B), pos), vb.reshape(Bd, T, KV_B, DH_B)], axis=2)
    qi = _partial_rope(qi.reshape(Bd, T, H_I, D_I), pos)
    ki = _partial_rope(ki.reshape(Bd, T, 1, D_I), pos)[:, :, 0]
    ki_all = jnp.concatenate([c_dsa_ik[page_table].reshape(Bd, P, D_I), ki.astype(c_dsa_ik.dtype)], axis=1)
    n_top = min(TOPK_MAX, (P + T) // 4)
    idx, valid = _index_topk(qi, wi, ki_all, pos, kpos, n_top)
    in_past = idx < P
    pidx = jnp.minimum(idx, P - 1)
    phys = _gather_rows(page_table, pidx // PAGE_SIZE) * PAGE_SIZE + pidx % PAGE_SIZE
    kv_sel_past = c_dsa_kv.reshape(-1, 2, KV_B, DH_B)[phys]
    kv_sel_new = _gather_rows(kv_b, jnp.clip(idx - P, 0, T - 1)).astype(kv_sel_past.dtype)
    kv_sel = jnp.where(in_past[..., None, None, None], kv_sel_past, kv_sel_new)
    yb = _sparse_attend(qb, kv_sel, valid)
    u = _glu(glu)
    u_ext = jnp.concatenate([s_conv.astype(u.dtype), u], axis=1)
    yc = _conv_branch(u_ext, wl)
    y = _merge(x, p, ya, yb, yc, ga, gb, gc, mg, wl)
    return y, kv_a, logf, kv_b, ki, u_ext[:, -(CONV_W - 1):]


def setup_inputs(seed: int = 0) -> dict:
    key = jax.random.key(seed)
    ks = jax.random.split(key, 26)
    f32 = jnp.float32
    n_pages = PAST_LEN // PAGE_SIZE
    n_used = DEC_BATCH * n_pages
    n_phys = n_used + max(1, n_used // 4)

    def nrm(k, shape, scale):
        return scale * jax.random.normal(k, shape, f32)

    page_table = jax.random.permutation(ks[0], n_phys)[:n_used].reshape(DEC_BATCH, n_pages).astype(jnp.int32)
    return {
        'x_prompt': nrm(ks[1], (BATCH, SEQ, D_MODEL), 1.0),
        'x_sample': nrm(ks[2], (DEC_BATCH, DEC_SEQ, D_MODEL), 1.0),
        'cache_fox_kv': nrm(ks[3], (DEPTH, n_phys, PAGE_SIZE, 2, H_A, DH_A), 1.0),
        'cache_fox_logf': jax.nn.log_sigmoid(FORGET_BIAS + nrm(ks[4], (DEPTH, n_phys, PAGE_SIZE, H_A), 1.0)),
        'cache_dsa_kv': nrm(ks[5], (DEPTH, n_phys, PAGE_SIZE, 2, KV_B, DH_B), 1.0),
        'cache_dsa_idx_k': nrm(ks[6], (DEPTH, n_phys, PAGE_SIZE, D_I), 1.0),
        'state_conv': nrm(ks[7], (DEPTH, DEC_BATCH, CONV_W - 1, D_C), 0.5),
        'page_table': page_table,
        'p_prompt': nrm(ks[8], (DEPTH, BATCH, SEQ, P_DIM), 1.0),
        'p_sample': nrm(ks[9], (DEPTH, DEC_BATCH, DEC_SEQ, P_DIM), 1.0),
        'w_in': nrm(ks[10], (DEPTH, D_MODEL, D_IN), D_MODEL ** -0.5),
        'b_f': FORGET_BIAS + nrm(ks[11], (DEPTH, H_A), 0.1),
        'w_a': nrm(ks[12], (DEPTH, W_A, D_MODEL), DN_BETA * W_A ** -0.5),
        'w_b': nrm(ks[13], (DEPTH, W_B, D_MODEL), DN_BETA * W_B ** -0.5),
        'w_c': nrm(ks[14], (DEPTH, D_C, D_MODEL), DN_BETA * D_C ** -0.5),
        'w_out': nrm(ks[15], (DEPTH, D_MODEL, D_MODEL), DN_BETA * D_MODEL ** -0.5),
        'ln_g': 1.0 + nrm(ks[16], (DEPTH, D_MODEL), 0.02),
        'ln_b': nrm(ks[17], (DEPTH, D_MODEL), 0.02),
        'conv_w': nrm(ks[18], (DEPTH, CONV_W, D_C), CONV_W ** -0.5),
        'conv_b': nrm(ks[19], (DEPTH, D_C), 0.02),
        'conv_ln_g': 1.0 + nrm(ks[20], (DEPTH, D_C), 0.02),
        'conv_ln_b': nrm(ks[21], (DEPTH, D_C), 0.02),
        'w_pe': nrm(ks[22], (DEPTH, P_DIM, D_MODEL), P_DIM ** -0.5),
        'w_pe_gate': nrm(ks[23], (DEPTH, D_MODEL, D_MODEL), D_MODEL ** -0.5),
    }


def reference(x_prompt, x_sample, cache_fox_kv, cache_fox_logf, cache_dsa_kv, cache_dsa_idx_k, state_conv,
              page_table, p_prompt, p_sample, w_in, b_f, w_a, w_b, w_c, w_out, ln_g, ln_b,
              conv_w, conv_b, conv_ln_g, conv_ln_b, w_pe, w_pe_gate):
    hp = x_prompt
    hs = x_sample
    st_p = []
    st_s = []
    for l in range(DEPTH):
        wl = dict(w_in=w_in[l], b_f=b_f[l], w_a=w_a[l], w_b=w_b[l], w_c=w_c[l], w_out=w_out[l],
                  ln_g=ln_g[l], ln_b=ln_b[l], conv_w=conv_w[l], conv_b=conv_b[l],
                  conv_ln_g=conv_ln_g[l], conv_ln_b=conv_ln_b[l], w_pe=w_pe[l], w_pe_gate=w_pe_gate[l])
        hp, *sp = _prompt_layer(hp, p_prompt[l], wl)
        hs, *ss = _sample_layer(hs, p_sample[l], wl, page_table, cache_fox_kv[l], cache_fox_logf[l],
                                cache_dsa_kv[l], cache_dsa_idx_k[l], state_conv[l])
        st_p.append(sp)
        st_s.append(ss)
    fox_kv_prompt = jnp.stack([s[0] for s in st_p])
    fox_logf_prompt = jnp.stack([s[1] for s in st_p])
    dsa_kv_prompt = jnp.stack([s[2] for s in st_p])
    dsa_idx_k_prompt = jnp.stack([s[3] for s in st_p])
    conv_state_prompt = jnp.stack([s[4] for s in st_p])
    fox_kv_sample = jnp.stack([s[0] for s in st_s])
    fox_logf_sample = jnp.stack([s[1] for s in st_s])
    dsa_kv_sample = jnp.stack([s[2] for s in st_s])
    dsa_idx_k_sample = jnp.stack([s[3] for s in st_s])
    conv_state_sample = jnp.stack([s[4] for s in st_s])
    return (hp, hs, fox_kv_prompt, fox_kv_sample, fox_logf_prompt, fox_logf_sample,
            dsa_kv_prompt, dsa_kv_sample, dsa_idx_k_prompt, dsa_idx_k_sample,
            conv_state_prompt, conv_state_sample)
```

```python
import functools

import jax
import jax.numpy as jnp
import numpy as np
from jax import lax
from jax.experimental import pallas as pl
from jax.experimental.pallas import tpu as pltpu

F32 = jnp.float32
BF16 = jnp.bfloat16
I32 = jnp.int32

D_MODEL = 1024
H_A = 8
DH = 64
W_A = H_A * DH
KV_B = 2
H_B = 8
G_B = H_B // KV_B
W_B = H_B * DH
H_I = 8
D_I = 64
TOPK_MAX = 256
D_C = 512
CONV_W = 31
P_DIM = 256
N_BRANCH = 3
ROPE_THETA = 500000.0
ROT_DIM = DH // 4
LN_EPS = 1e-5
PAGE = 128
LANES = 128

NEG = -1e30
INT_MIN = -(2 ** 31)
VMEM_LIMIT = 56 * 1024 * 1024

_SPLIT = (W_A, W_A, W_A, H_A, W_A, W_B, KV_B * DH, KV_B * DH, H_I * D_I, D_I, H_I, W_B,
          2 * D_C, D_C, N_BRANCH * D_MODEL)
_OFF = np.concatenate([[0], np.cumsum(_SPLIT)]).tolist()
(O_QA, O_KA, O_VA, O_FA, O_GA, O_QB, O_KB, O_VB, O_QI, O_KI, O_WI, O_GB, O_GLU, O_GC, O_MG,
 O_END) = _OFF


def _cparams(sem):
    return pltpu.CompilerParams(dimension_semantics=sem, vmem_limit_bytes=VMEM_LIMIT)


def _dot(a, b):
    return jnp.dot(a, b, preferred_element_type=F32)


def _dot_nt(a, b):
    return lax.dot_general(a, b, (((1,), (1,)), ((), ())), preferred_element_type=F32)


def _dot_tn(a, b):
    return lax.dot_general(a, b, (((0,), (0,)), ((), ())), preferred_element_type=F32)


def _sigmoid(x):
    return 1.0 / (1.0 + jnp.exp(-x))


def _split3(x):
    hi = x.astype(BF16)
    r1 = x - hi.astype(F32)
    mid = r1.astype(BF16)
    lo = (r1 - mid.astype(F32)).astype(BF16)
    return hi, mid, lo


def _rope(x, c, s1, s2):
    n = x.shape[-1]
    half = ROT_DIM // 2
    return x * c + pltpu.roll(x, half, 1) * s1 + pltpu.roll(x, n - half, 1) * s2


def _tile_lanes(t, reps):
    return t if reps == 1 else jnp.concatenate([t] * reps, axis=1)


def _proj_fox_body(x_ref, w_ref, bf_ref, q_ref, k_ref, v_ref, kv_ref, logf_ref):
    z = _dot(x_ref[...], w_ref[...])
    q_ref[...] = (z[:, :W_A] * (DH ** -0.5)).astype(BF16)
    k_ref[...] = z[:, W_A:2 * W_A].astype(BF16)
    v_ref[...] = z[:, 2 * W_A:3 * W_A].astype(BF16)
    kv_ref[...] = z[:, W_A:3 * W_A]
    f = z[:, 3 * W_A:] + bf_ref[...]
    logf = jnp.minimum(f, 0.0) - jnp.log1p(jnp.exp(-jnp.abs(f)))
    logf_ref[...] = logf[:, :H_A]


def _proj_gates_body(x_ref, w_ref, o_ref):
    z = _dot(x_ref[...], w_ref[...])
    o_ref[...] = z * _sigmoid(z)


def _proj_dsaq_body(x_ref, w_ref, c_ref, s1_ref, s2_ref, qb_ref, qi_ref):
    z = _dot(x_ref[...], w_ref[...])
    reps = z.shape[1] // LANES
    y = _rope(z, _tile_lanes(c_ref[...], reps), _tile_lanes(s1_ref[...], reps),
              _tile_lanes(s2_ref[...], reps)) * (DH ** -0.5)
    qb_ref[...] = y[:, :W_B].astype(BF16)
    qi_ref[...] = y[:, W_B:].astype(BF16)


def _proj_dsak_body(x_ref, w_ref, c_ref, s1_ref, s2_ref, kv_ref, kvbf_ref, ki_ref, kibf_ref, wi_ref):
    z = _dot(x_ref[...], w_ref[...])
    c, s1, s2 = c_ref[...], s1_ref[...], s2_ref[...]
    kb = _rope(z[:, :LANES], c, s1, s2)
    kv = jnp.concatenate([kb, z[:, LANES:2 * LANES]], axis=1)
    kv_ref[...] = kv
    kvbf_ref[...] = kv.astype(BF16)
    t = _rope(z[:, 2 * LANES:], c, s1, s2)
    ki_ref[...] = t[:, :D_I]
    kibf_ref[...] = t[:, :D_I].astype(BF16)
    wi_ref[...] = z[:, 2 * LANES + D_I:2 * LANES + D_I + H_I] * (H_I ** -0.5)


def _proj_glu_body(x_ref, w_ref, u_ref):
    z = _dot(x_ref[...], w_ref[...])
    u_ref[...] = z[:, :D_C] * _sigmoid(z[:, D_C:])


def _row_spec(tm, n):
    return pl.BlockSpec((tm, n), lambda i: (i, 0))


def _const_spec(shape):
    return pl.BlockSpec(shape, lambda i: (0,) * len(shape))


def _project(x_bf, wg, tabs, tm, tab_blocks):
    m = x_bf.shape[0]
    grid = (m // tm,)
    xs = _row_spec(tm, D_MODEL)
    tab_spec = pl.BlockSpec((tm, LANES), lambda i: (i % tab_blocks, 0))
    cp = _cparams(("parallel",))

    def sds(n, dt):
        return jax.ShapeDtypeStruct((m, n), dt)

    q_bf, k_bf, v_bf, kva, logf = pl.pallas_call(
        _proj_fox_body, grid=grid,
        in_specs=[xs, _const_spec(wg['fox'].shape), _const_spec((1, LANES))],
        out_specs=[_row_spec(tm, W_A)] * 3 + [_row_spec(tm, 2 * W_A), _row_spec(tm, H_A)],
        out_shape=[sds(W_A, BF16)] * 3 + [sds(2 * W_A, F32), sds(H_A, F32)],
        compiler_params=cp)(x_bf, wg['fox'], wg['b_f'])
    sg = pl.pallas_call(
        _proj_gates_body, grid=grid,
        in_specs=[xs, _const_spec(wg['gates'].shape)],
        out_specs=_row_spec(tm, 3 * W_A), out_shape=sds(3 * W_A, F32),
        compiler_params=cp)(x_bf, wg['gates'])
    qb_bf, qi_bf = pl.pallas_call(
        _proj_dsaq_body, grid=grid,
        in_specs=[xs, _const_spec(wg['dsaq'].shape), tab_spec, tab_spec, tab_spec],
        out_specs=[_row_spec(tm, W_B), _row_spec(tm, H_I * D_I)],
        out_shape=[sds(W_B, BF16), sds(H_I * D_I, BF16)],
        compiler_params=cp)(x_bf, wg['dsaq'], *tabs)
    kvb, kvb_bf, ki, ki_bf, wi = pl.pallas_call(
        _proj_dsak_body, grid=grid,
        in_specs=[xs, _const_spec(wg['dsak'].shape), tab_spec, tab_spec, tab_spec],
        out_specs=[_row_spec(tm, 2 * LANES), _row_spec(tm, 2 * LANES), _row_spec(tm, D_I),
                   _row_spec(tm, D_I), _row_spec(tm, H_I)],
        out_shape=[sds(2 * LANES, F32), sds(2 * LANES, BF16), sds(D_I, F32), sds(D_I, BF16),
                   sds(H_I, F32)],
        compiler_params=cp)(x_bf, wg['dsak'], *tabs)
    u = pl.pallas_call(
        _proj_glu_body, grid=grid,
        in_specs=[xs, _const_spec(wg['glu'].shape)],
        out_specs=_row_spec(tm, D_C), out_shape=sds(D_C, F32),
        compiler_params=cp)(x_bf, wg['glu'])
    return dict(q_bf=q_bf, k_bf=k_bf, v_bf=v_bf, kva=kva, logf=logf, sg=sg, qb_bf=qb_bf,
                qi_bf=qi_bf, kvb=kvb, kvb_bf=kvb_bf, ki=ki, ki_bf=ki_bf, wi=wi, u=u)


def _cumsum_body(x_ref, o_ref, carry_ref):
    @pl.when(pl.program_id(1) == 0)
    def _():
        carry_ref[...] = jnp.zeros_like(carry_ref)

    t = x_ref.shape[1]
    r = lax.broadcasted_iota(I32, (t, t), 0)
    c = lax.broadcasted_iota(I32, (t, t), 1)
    tri = jnp.where(c <= r, 1.0, 0.0).astype(BF16)
    hi, mid, lo = _split3(x_ref[0])
    cum = _dot(tri, hi) + _dot(tri, mid) + _dot(tri, lo) + carry_ref[...]
    o_ref[0] = cum
    carry_ref[...] = cum[t - 1:t, :]


def _cumsum(logf, tb):
    b, s, h = logf.shape
    return pl.pallas_call(
        _cumsum_body, grid=(b, s // tb),
        in_specs=[pl.BlockSpec((1, tb, h), lambda i, j: (i, j, 0))],
        out_specs=pl.BlockSpec((1, tb, h), lambda i, j: (i, j, 0)),
        out_shape=jax.ShapeDtypeStruct((b, s, h), F32),
        scratch_shapes=[pltpu.VMEM((1, h), F32)],
        compiler_params=_cparams(("parallel", "arbitrary")))(logf)


def _fox_body(q_ref, k_ref, v_ref, cq_ref, ck_ref, o_ref, *, tq):
    i = pl.program_id(1)
    row = i * tq + lax.broadcasted_iota(I32, (tq, tq), 0)
    col0 = lax.broadcasted_iota(I32, (tq, tq), 1)
    for h in range(H_A):
        hs = slice(h * DH, (h + 1) * DH)
        q = q_ref[:, hs]
        cq = cq_ref[0, :, h:h + 1]

        def chunk(c, carry, hs=hs, q=q, cq=cq, h=h):
            m, l, acc = carry
            off = pl.multiple_of(c * tq, tq)
            k = k_ref[0, pl.ds(off, tq), hs]
            v = v_ref[0, pl.ds(off, tq), hs]
            ck = ck_ref[0, h:h + 1, pl.ds(off, tq)]
            s = _dot_nt(q, k) + cq - ck
            s = jnp.where(col0 + off <= row, s, NEG)
            m_new = jnp.maximum(m, jnp.max(s, axis=1, keepdims=True))
            alpha = jnp.exp(m - m_new)
            p = jnp.exp(s - m_new)
            l = alpha * l + jnp.sum(p, axis=1, keepdims=True)
            acc = alpha * acc + _dot(p.astype(BF16), v)
            return m_new, l, acc

        init = (jnp.full((tq, 1), NEG, F32), jnp.zeros((tq, 1), F32), jnp.zeros((tq, DH), F32))
        m, l, acc = lax.fori_loop(0, i + 1, chunk, init)
        o_ref[:, hs] = acc / l


def _fox_prompt(q_bf, k_bf, v_bf, cum, cum_t, b, s, tq):
    nq = s // tq
    k3 = k_bf.reshape(b, s, W_A)
    v3 = v_bf.reshape(b, s, W_A)
    return pl.pallas_call(
        functools.partial(_fox_body, tq=tq), grid=(b, nq),
        in_specs=[pl.BlockSpec((tq, W_A), lambda i, j: (i * nq + j, 0)),
                  pl.BlockSpec((1, s, W_A), lambda i, j: (i, 0, 0)),
                  pl.BlockSpec((1, s, W_A), lambda i, j: (i, 0, 0)),
                  pl.BlockSpec((1, tq, H_A), lambda i, j: (i, j, 0)),
                  pl.BlockSpec((1, H_A, s), lambda i, j: (i, 0, 0))],
        out_specs=pl.BlockSpec((tq, W_A), lambda i, j: (i * nq + j, 0)),
        out_shape=jax.ShapeDtypeStruct((b * s, W_A), F32),
        compiler_params=_cparams(("parallel", "arbitrary")))(q_bf, k3, v3, cum, cum_t)


def _sort_key(x):
    bits = lax.bitcast_convert_type(x, I32)
    return bits ^ ((bits >> 31) & 0x7FFFFFFF)


def _kth_largest(count_ge, rows, n_top):
    kf = jnp.float32(n_top)
    zero = jnp.zeros((rows, 1), I32)
    t0 = jnp.where(count_ge(zero) >= kf, zero, jnp.full((rows, 1), INT_MIN, I32))

    def bit_step(it, t):
        cand = t + jnp.left_shift(jnp.int32(1), 30 - it)
        return jnp.where(count_ge(cand) >= kf, cand, t)

    return lax.fori_loop(0, 31, bit_step, t0)


def _tie_cut(count_eq_below, rows, need, n_bits):
    def bit_step(it, j):
        cand = j + jnp.left_shift(jnp.int32(1), n_bits - 1 - it)
        return jnp.where(count_eq_below(cand) < need, cand, j)

    return lax.fori_loop(0, n_bits, bit_step, jnp.zeros((rows, 1), I32))


def _dsa_body(qb_ref, qi_ref, wi_ref, ki_ref, kv_ref, o_ref, key_ref, bias_ref, *, tq, s_len, n_top):
    i = pl.program_id(1)
    nk = i + 1
    row = i * tq + lax.broadcasted_iota(I32, (tq, tq), 0)
    col0 = lax.broadcasted_iota(I32, (tq, tq), 1)
    wi = wi_ref[...]

    def score_chunk(c, _):
        off = pl.multiple_of(c * tq, tq)
        ki = ki_ref[0, pl.ds(off, tq), :]
        sc = jnp.zeros((tq, tq), F32)
        for h in range(H_I):
            r = _dot_nt(qi_ref[:, h * D_I:(h + 1) * D_I], ki)
            sc = sc + wi[:, h:h + 1] * jnp.maximum(r, 0.0)
        key_ref[:, pl.ds(off, tq)] = jnp.where(col0 + off <= row, _sort_key(sc), INT_MIN)
        return 0

    lax.fori_loop(0, nk, score_chunk, 0)

    def count(pred):
        def body(c, cnt):
            off = pl.multiple_of(c * tq, tq)
            return cnt + jnp.where(pred(key_ref[:, pl.ds(off, tq)], off), 1.0, 0.0)
        cnt = lax.fori_loop(0, nk, body, jnp.zeros((tq, tq), F32))
        return jnp.sum(cnt, axis=1, keepdims=True)

    thr = _kth_largest(lambda cand: count(lambda k, off: k >= cand), tq, n_top)
    c_gt = count(lambda k, off: k > thr)
    c_eq = count(lambda k, off: k == thr)
    need = jnp.float32(n_top) - c_gt
    has_tie = jnp.max(jnp.where((c_eq > need) & (thr > INT_MIN), 1.0, 0.0)) > 0.0

    def tie_path():
        n_bits = int(np.log2(s_len))
        return _tie_cut(lambda cand: count(lambda k, off: (k == thr) & (col0 + off < cand)),
                        tq, need, n_bits)

    cut = lax.cond(has_tie, tie_path, lambda: jnp.full((tq, 1), s_len, I32))

    def bias_chunk(c, _):
        off = pl.multiple_of(c * tq, tq)
        k = key_ref[:, pl.ds(off, tq)]
        idx = col0 + off
        sel = ((k > thr) | ((k == thr) & (idx <= cut))) & (idx <= row)
        bias_ref[:, pl.ds(off, tq)] = jnp.where(sel, 0.0, NEG)
        return 0

    lax.fori_loop(0, nk, bias_chunk, 0)

    for h in range(H_B):
        n = h // G_B
        q = qb_ref[:, h * DH:(h + 1) * DH]

        def chunk(c, carry, q=q, n=n):
            m, l, acc = carry
            off = pl.multiple_of(c * tq, tq)
            k = kv_ref[0, pl.ds(off, tq), n * DH:(n + 1) * DH]
            v = kv_ref[0, pl.ds(off, tq), LANES + n * DH:LANES + (n + 1) * DH]
            s = _dot_nt(q, k) + bias_ref[:, pl.ds(off, tq)]
            m_new = jnp.maximum(m, jnp.max(s, axis=1, keepdims=True))
            alpha = jnp.exp(m - m_new)
            p = jnp.exp(s - m_new)
            l = alpha * l + jnp.sum(p, axis=1, keepdims=True)
            acc = alpha * acc + _dot(p.astype(BF16), v)
            return m_new, l, acc

        init = (jnp.full((tq, 1), NEG, F32), jnp.zeros((tq, 1), F32), jnp.zeros((tq, DH), F32))
        m, l, acc = lax.fori_loop(0, nk, chunk, init)
        o_ref[:, h * DH:(h + 1) * DH] = acc / l


def _dsa_prompt(qb_bf, qi_bf, wi, ki_bf, kvb_bf, b, s, tq):
    nq = s // tq
    n_top = min(TOPK_MAX, s // 4)
    ki3 = ki_bf.reshape(b, s, D_I)
    kv3 = kvb_bf.reshape(b, s, 2 * LANES)
    rows = lambda i, j: (i * nq + j, 0)
    return pl.pallas_call(
        functools.partial(_dsa_body, tq=tq, s_len=s, n_top=n_top), grid=(b, nq),
        in_specs=[pl.BlockSpec((tq, W_B), rows), pl.BlockSpec((tq, H_I * D_I), rows),
                  pl.BlockSpec((tq, H_I), rows),
                  pl.BlockSpec((1, s, D_I), lambda i, j: (i, 0, 0)),
                  pl.BlockSpec((1, s, 2 * LANES), lambda i, j: (i, 0, 0))],
        out_specs=pl.BlockSpec((tq, W_B), rows),
        out_shape=jax.ShapeDtypeStruct((b * s, W_B), F32),
        scratch_shapes=[pltpu.VMEM((tq, s), I32), pltpu.VMEM((tq, s), F32)],
        compiler_params=_cparams(("parallel", "arbitrary")))(qb_bf, qi_bf, wi, ki3, kv3)


HALO = 32


def _conv_ln_silu(acc, g, b):
    mu = jnp.mean(acc, axis=1, keepdims=True)
    d = acc - mu
    var = jnp.mean(d * d, axis=1, keepdims=True)
    y = d * lax.rsqrt(var + LN_EPS) * g + b
    return y * _sigmoid(y)


def _conv_body(u_ref, halo_ref, w_ref, cb_ref, g_ref, b_ref, o_ref, ext_ref, *, tq):
    j = pl.program_id(1)
    ext_ref[0:HALO, :] = jnp.where(j > 0, halo_ref[...], 0.0)
    ext_ref[HALO:, :] = u_ref[...]
    base = HALO - (CONV_W - 1)
    acc = jnp.zeros((tq, D_C), F32) + cb_ref[...]
    for t in range(CONV_W):
        acc = acc + ext_ref[base + t:base + t + tq, :] * w_ref[t:t + 1, :]
    o_ref[...] = _conv_ln_silu(acc, g_ref[...], b_ref[...])


def _conv_prompt(u, conv_w, conv_b, g, bb, b, s, tq):
    nq = s // tq
    hb = tq // HALO
    rows = lambda i, j: (i * nq + j, 0)
    return pl.pallas_call(
        functools.partial(_conv_body, tq=tq), grid=(b, nq),
        in_specs=[pl.BlockSpec((tq, D_C), rows),
                  pl.BlockSpec((HALO, D_C), lambda i, j: (jnp.maximum((i * nq + j) * hb - 1, 0), 0)),
                  pl.BlockSpec((CONV_W, D_C), lambda i, j: (0, 0)),
                  pl.BlockSpec((1, D_C), lambda i, j: (0, 0)),
                  pl.BlockSpec((1, D_C), lambda i, j: (0, 0)),
                  pl.BlockSpec((1, D_C), lambda i, j: (0, 0))],
        out_specs=pl.BlockSpec((tq, D_C), rows),
        out_shape=jax.ShapeDtypeStruct((b * s, D_C), F32),
        scratch_shapes=[pltpu.VMEM((HALO + tq, D_C), F32)],
        compiler_params=_cparams(("parallel", "arbitrary")))(u, u, conv_w, conv_b, g, bb)


def _merge_body(x_ref, p_ref, ya_ref, yb_ref, yc_ref, sg_ref, wmg_ref, wa_ref, wb_ref, wc_ref,
                wo_ref, wpe_ref, wpg_ref, g_ref, b_ref, y_ref, ybf_ref, *, alpha):
    x = x_ref[...]
    gates = _sigmoid(_dot(x.astype(BF16), wmg_ref[...]))
    sg = sg_ref[...]
    ua = _dot((ya_ref[...] * sg[:, :W_A]).astype(BF16), wa_ref[...])
    ub = _dot((yb_ref[...] * sg[:, W_A:2 * W_A]).astype(BF16), wb_ref[...])
    uc = _dot((yc_ref[...] * sg[:, 2 * W_A:]).astype(BF16), wc_ref[...])
    m = (gates[:, :D_MODEL] * ua + gates[:, D_MODEL:2 * D_MODEL] * ub
         + gates[:, 2 * D_MODEL:] * uc)
    r = alpha * x + _dot(m.astype(BF16), wo_ref[...])
    mu = jnp.mean(r, axis=1, keepdims=True)
    d = r - mu
    var = jnp.mean(d * d, axis=1, keepdims=True)
    h = d * lax.rsqrt(var + LN_EPS) * g_ref[...] + b_ref[...]
    pe = _dot(p_ref[...].astype(BF16), wpe_ref[...])
    y = h + _sigmoid(_dot(h.astype(BF16), wpg_ref[...])) * pe
    y_ref[...] = y
    ybf_ref[...] = y.astype(BF16)


def _merge(x, p, ya, yb, yc, sg, wl, tm, alpha):
    m = x.shape[0]
    ws = [wl['w_mg'], wl['w_a'], wl['w_b'], wl['w_c'], wl['w_out'], wl['w_pe'], wl['w_pe_gate'],
          wl['ln_g'], wl['ln_b']]
    return pl.pallas_call(
        functools.partial(_merge_body, alpha=alpha), grid=(m // tm,),
        in_specs=[_row_spec(tm, D_MODEL), _row_spec(tm, P_DIM), _row_spec(tm, W_A),
                  _row_spec(tm, W_B), _row_spec(tm, D_C), _row_spec(tm, 3 * W_A)]
                 + [_const_spec(w.shape) for w in ws],
        out_specs=[_row_spec(tm, D_MODEL), _row_spec(tm, D_MODEL)],
        out_shape=[jax.ShapeDtypeStruct((m, D_MODEL), F32), jax.ShapeDtypeStruct((m, D_MODEL), BF16)],
        compiler_params=_cparams(("parallel",)))(x, p, ya, yb, yc, sg, *ws)


def _head_rows(row, width, seg):
    hh = lax.broadcasted_iota(I32, (H_A, width), 0)
    ll = lax.broadcasted_iota(I32, (H_A, width), 1)
    rows = jnp.broadcast_to(row.astype(F32), (H_A, width))
    return jnp.where(ll // seg == hh, rows, 0.0).astype(row.dtype)


def _fox_dec_body(pt_ref, q_ref, kvn_ref, lfn_ref, *refs, g):
    kv_refs = refs[:g]
    lf_refs = refs[g:2 * g]
    o_ref = refs[2 * g]
    m_ref, l_ref, acc_ref, carry_ref = refs[2 * g + 1:]
    t = pl.program_id(1)
    qrows = _head_rows(q_ref[0], W_A, DH)

    @pl.when(t == 0)
    def _():
        kn = kvn_ref[0][:, :W_A].astype(BF16).astype(F32)
        vn = kvn_ref[0][:, W_A:]
        s_new = jnp.sum(qrows.astype(F32) * kn, axis=1, keepdims=True)
        m_ref[...] = s_new
        l_ref[...] = jnp.ones_like(l_ref)
        acc_ref[...] = jnp.broadcast_to(vn.astype(BF16).astype(F32), (H_A, W_A))
        carry_ref[...] = lfn_ref[0]

    r = lax.broadcasted_iota(I32, (PAGE, PAGE), 0)
    c = lax.broadcasted_iota(I32, (PAGE, PAGE), 1)
    upper = jnp.where(r > c, 1.0, 0.0).astype(BF16)
    for gi in range(g):
        page = kv_refs[gi][0]
        k = page[:, :W_A].astype(BF16)
        v = page[:, W_A:].astype(BF16)
        hi, mid, lo = _split3(lf_refs[gi][0])
        suf = _dot_tn(hi, upper) + _dot_tn(mid, upper) + _dot_tn(lo, upper)
        bias = suf + carry_ref[...]
        ones = jnp.ones((PAGE, PAGE), BF16)
        tot = _dot_tn(hi, ones) + _dot_tn(mid, ones) + _dot_tn(lo, ones)
        carry_ref[...] = carry_ref[...] + tot[:, :1]
        s = _dot_nt(qrows, k) + bias
        m_old = m_ref[...]
        m_new = jnp.maximum(m_old, jnp.max(s, axis=1, keepdims=True))
        alpha = jnp.exp(m_old - m_new)
        p = jnp.exp(s - m_new)
        l_ref[...] = alpha * l_ref[...] + jnp.sum(p, axis=1, keepdims=True)
        acc_ref[...] = alpha * acc_ref[...] + _dot(p.astype(BF16), v)
        m_ref[...] = m_new

    @pl.when(t == pl.num_programs(1) - 1)
    def _():
        o_ref[0] = acc_ref[...] / l_ref[...]


def _fox_decode(q_bf, kva, logf, page_table, c_kv, c_lf, g):
    bd, n_pages = page_table.shape
    n_phys = c_kv.shape[0]
    steps = n_pages // g
    kv2 = c_kv.reshape(n_phys, PAGE, 2 * W_A)

    def page_map(gi):
        return lambda b, t, pt: (pt[b, n_pages - 1 - (t * g + gi)], 0, 0)

    per_b = lambda b, t, pt: (b, 0, 0)
    out = pl.pallas_call(
        functools.partial(_fox_dec_body, g=g),
        grid_spec=pltpu.PrefetchScalarGridSpec(
            num_scalar_prefetch=1, grid=(bd, steps),
            in_specs=[pl.BlockSpec((1, 1, W_A), per_b), pl.BlockSpec((1, 1, 2 * W_A), per_b),
                      pl.BlockSpec((1, H_A, 1), per_b)]
                     + [pl.BlockSpec((1, PAGE, 2 * W_A), page_map(gi)) for gi in range(g)]
                     + [pl.BlockSpec((1, PAGE, H_A), page_map(gi)) for gi in range(g)],
            out_specs=pl.BlockSpec((1, H_A, W_A), per_b),
            scratch_shapes=[pltpu.VMEM((H_A, 1), F32), pltpu.VMEM((H_A, 1), F32),
                            pltpu.VMEM((H_A, W_A), F32), pltpu.VMEM((H_A, 1), F32)]),
        out_shape=jax.ShapeDtypeStruct((bd, H_A, W_A), F32),
        compiler_params=_cparams(("parallel", "arbitrary")))(
            page_table, q_bf.reshape(bd, 1, W_A), kva.reshape(bd, 1, 2 * W_A),
            logf.reshape(bd, H_A, 1), *([kv2] * g), *([c_lf] * g))
    o4 = out.reshape(bd, H_A, H_A, DH)
    return jnp.stack([o4[:, h, h] for h in range(H_A)], axis=1).reshape(bd, W_A)


def _idx_dec_body(pt_ref, qi_ref, wi_ref, kin_ref, *refs, g):
    ki_refs = refs[:g]
    o_ref, on_ref = refs[g:]
    qi = qi_ref[0]
    w = wi_ref[0]
    rows = []
    for gi in range(g):
        r = _dot_nt(qi, ki_refs[gi][0].astype(BF16))
        rows.append(jnp.sum(w * jnp.maximum(r, 0.0), axis=0, keepdims=True))
    o_ref[0] = jnp.concatenate(rows, axis=0)
    r_new = jnp.sum(qi.astype(F32) * kin_ref[0].astype(F32), axis=1, keepdims=True)
    sc_new = jnp.sum(w * jnp.maximum(r_new, 0.0), axis=0, keepdims=True)
    on_ref[0] = jnp.broadcast_to(sc_new, (1, LANES))


def _idx_decode(qi_bf, wi, ki_new_bf, page_table, c_ik, g):
    bd, n_pages = page_table.shape
    steps = n_pages // g

    def page_map(gi):
        return lambda b, t, pt: (pt[b, t * g + gi], 0, 0)

    per_b = lambda b, t, pt: (b, 0, 0)
    sc, sc_new = pl.pallas_call(
        functools.partial(_idx_dec_body, g=g),
        grid_spec=pltpu.PrefetchScalarGridSpec(
            num_scalar_prefetch=1, grid=(bd, steps),
            in_specs=[pl.BlockSpec((1, H_I, D_I), per_b), pl.BlockSpec((1, H_I, 1), per_b),
                      pl.BlockSpec((1, 1, D_I), per_b)]
                     + [pl.BlockSpec((1, PAGE, D_I), page_map(gi)) for gi in range(g)],
            out_specs=[pl.BlockSpec((1, g, PAGE), lambda b, t, pt: (b, t, 0)),
                       pl.BlockSpec((1, 1, LANES), per_b)]),
        out_shape=[jax.ShapeDtypeStruct((bd, n_pages, PAGE), F32),
                   jax.ShapeDtypeStruct((bd, 1, LANES), F32)],
        compiler_params=_cparams(("parallel", "arbitrary")))(
            page_table, qi_bf.reshape(bd, H_I, D_I), wi.reshape(bd, H_I, 1),
            ki_new_bf.reshape(bd, 1, D_I), *([c_ik] * g))
    return sc.reshape(bd, n_pages * PAGE), sc_new[:, 0, :1]


def _sel_dec_body(sc_ref, scn_ref, bias_ref, biasn_ref, *, n_top, n_bits):
    bd, p_len = sc_ref.shape
    key = _sort_key(sc_ref[...])
    key_n = _sort_key(scn_ref[...])
    idx = lax.broadcasted_iota(I32, (bd, p_len), 1)

    def count(pred, pred_n):
        return (jnp.sum(jnp.where(pred, 1.0, 0.0), axis=1, keepdims=True)
                + jnp.where(pred_n, 1.0, 0.0))

    thr = _kth_largest(lambda cand: count(key >= cand, key_n >= cand), bd, n_top)
    need = jnp.float32(n_top) - count(key > thr, key_n > thr)
    cut = _tie_cut(lambda cand: jnp.sum(jnp.where((key == thr) & (idx < cand), 1.0, 0.0),
                                        axis=1, keepdims=True), bd, need, n_bits)
    eq_past = jnp.sum(jnp.where(key == thr, 1.0, 0.0), axis=1, keepdims=True)
    sel = (key > thr) | ((key == thr) & (idx <= cut))
    bias_ref[...] = jnp.where(sel, 0.0, NEG)
    sel_n = (key_n > thr) | ((key_n == thr) & (eq_past < need))
    biasn_ref[...] = jnp.where(sel_n, 0.0, NEG)


def _sel_decode(sc, sc_new, n_top):
    bd, p_len = sc.shape
    n_bits = int(np.log2(p_len))
    return pl.pallas_call(
        functools.partial(_sel_dec_body, n_top=n_top, n_bits=n_bits),
        out_shape=[jax.ShapeDtypeStruct((bd, p_len), F32), jax.ShapeDtypeStruct((bd, 1), F32)],
        compiler_params=pltpu.CompilerParams(vmem_limit_bytes=VMEM_LIMIT))(sc, sc_new)


def _dsa_dec_body(pt_ref, q_ref, kvn_ref, bn_ref, bias_ref, *refs, g):
    kv_refs = refs[:g]
    o_ref = refs[g]
    m_ref, l_ref, acc_ref = refs[g + 1:]
    t = pl.program_id(1)
    hh = lax.broadcasted_iota(I32, (H_B, LANES), 0)
    ll = lax.broadcasted_iota(I32, (H_B, LANES), 1)
    qrows = jnp.where(ll // DH == hh // G_B, q_ref[0].astype(F32), 0.0).astype(BF16)

    @pl.when(t == 0)
    def _():
        kn = kvn_ref[0][:, :LANES].astype(BF16).astype(F32)
        vn = kvn_ref[0][:, LANES:].astype(BF16).astype(F32)
        s_new = jnp.sum(qrows.astype(F32) * kn, axis=1, keepdims=True) + bn_ref[0]
        m_ref[...] = s_new
        l_ref[...] = jnp.where(s_new > 0.5 * NEG, 1.0, 0.0)
        acc_ref[...] = jnp.where(s_new > 0.5 * NEG, jnp.broadcast_to(vn, (H_B, LANES)), 0.0)

    for gi in range(g):
        page = kv_refs[gi][0]
        k = page[:, :LANES].astype(BF16)
        v = page[:, LANES:].astype(BF16)
        s = _dot_nt(qrows, k) + bias_ref[0, gi:gi + 1, :]
        m_old = m_ref[...]
        m_new = jnp.maximum(m_old, jnp.max(s, axis=1, keepdims=True))
        alpha = jnp.exp(m_old - m_new)
        p = jnp.where(s > 0.5 * NEG, jnp.exp(s - m_new), 0.0)
        l_ref[...] = alpha * l_ref[...] + jnp.sum(p, axis=1, keepdims=True)
        acc_ref[...] = alpha * acc_ref[...] + _dot(p.astype(BF16), v)
        m_ref[...] = m_new

    @pl.when(t == pl.num_programs(1) - 1)
    def _():
        o_ref[0] = acc_ref[...] / l_ref[...]


def _dsa_decode(qb_bf, kvb, bias, bias_new, page_table, c_kv, g):
    bd, n_pages = page_table.shape
    n_phys = c_kv.shape[0]
    steps = n_pages // g
    kv2 = c_kv.reshape(n_phys, PAGE, 2 * LANES)
    q2 = jnp.tile(qb_bf.reshape(bd, H_B, DH), (1, 1, 2))

    def page_map(gi):
        return lambda b, t, pt: (pt[b, t * g + gi], 0, 0)

    per_b = lambda b, t, pt: (b, 0, 0)
    out = pl.pallas_call(
        functools.partial(_dsa_dec_body, g=g),
        grid_spec=pltpu.PrefetchScalarGridSpec(
            num_scalar_prefetch=1, grid=(bd, steps),
            in_specs=[pl.BlockSpec((1, H_B, LANES), per_b), pl.BlockSpec((1, 1, 2 * LANES), per_b),
                      pl.BlockSpec((1, 1, 1), per_b),
                      pl.BlockSpec((1, g, PAGE), lambda b, t, pt: (b, t, 0))]
                     + [pl.BlockSpec((1, PAGE, 2 * LANES), page_map(gi)) for gi in range(g)],
            out_specs=pl.BlockSpec((1, H_B, LANES), per_b),
            scratch_shapes=[pltpu.VMEM((H_B, 1), F32), pltpu.VMEM((H_B, 1), F32),
                            pltpu.VMEM((H_B, LANES), F32)]),
        out_shape=jax.ShapeDtypeStruct((bd, H_B, LANES), F32),
        compiler_params=_cparams(("parallel", "arbitrary")))(
            page_table, q2, kvb.reshape(bd, 1, 2 * LANES), bias_new.reshape(bd, 1, 1),
            bias.reshape(bd, n_pages, PAGE), *([kv2] * g))
    o4 = out.reshape(bd, KV_B, G_B, KV_B, DH)
    return jnp.stack([o4[:, n, :, n] for n in range(KV_B)], axis=1).reshape(bd, W_B)


def _conv_dec_body(st_ref, u_ref, w_ref, cb_ref, g_ref, b_ref, o_ref):
    acc = u_ref[...] * w_ref[CONV_W - 1:CONV_W, :] + cb_ref[...]
    for t in range(CONV_W - 1):
        acc = acc + st_ref[t] * w_ref[t:t + 1, :]
    o_ref[...] = _conv_ln_silu(acc, g_ref[...], b_ref[...])


def _conv_decode(state, u, conv_w, conv_b, g, bb):
    bd = u.shape[0]
    return pl.pallas_call(
        _conv_dec_body, out_shape=jax.ShapeDtypeStruct((bd, D_C), F32),
        compiler_params=pltpu.CompilerParams(vmem_limit_bytes=VMEM_LIMIT))(
            jnp.swapaxes(state, 0, 1), u, conv_w, conv_b, g, bb)


def _rope_tables(pos):
    half = ROT_DIM // 2
    inv = ROPE_THETA ** (-jnp.arange(half, dtype=F32) / half)
    ang = pos.astype(F32)[:, None] * inv[None, :]
    cos, sin = jnp.cos(ang), jnp.sin(ang)
    n = pos.shape[0]
    pad = jnp.zeros((n, DH - ROT_DIM), F32)
    c = jnp.concatenate([cos, cos, pad + 1.0], axis=1)
    s1 = jnp.concatenate([jnp.zeros_like(sin), sin, pad], axis=1)
    s2 = jnp.concatenate([-sin, jnp.zeros_like(sin), pad], axis=1)
    return tuple(jnp.tile(t, (1, LANES // DH)) for t in (c, s1, s2))


def _regroup_weights(w_in, b_f):
    w = w_in.astype(BF16)
    d = w.shape[0]
    z = lambda n: jnp.zeros((d, n), BF16)
    fox = jnp.concatenate([w[:, O_QA:O_FA], w[:, O_FA:O_GA], z(LANES - H_A)], axis=1)
    gates = jnp.concatenate([w[:, O_GA:O_QB], w[:, O_GB:O_GLU], w[:, O_GC:O_MG]], axis=1)
    dsaq = jnp.concatenate([w[:, O_QB:O_KB], w[:, O_QI:O_KI]], axis=1)
    dsak = jnp.concatenate([w[:, O_KB:O_QI], w[:, O_KI:O_GB], z(LANES - D_I - H_I)], axis=1)
    glu = w[:, O_GLU:O_GC]
    bf = jnp.concatenate([b_f.astype(F32), jnp.zeros((LANES - H_A,), F32)])[None, :]
    return dict(fox=fox, gates=gates, dsaq=dsaq, dsak=dsak, glu=glu, b_f=bf, w_mg=w[:, O_MG:O_END])


def _pick(n, prefs):
    for t in prefs:
        if n % t == 0:
            return t
    return n


def kernel(x_prompt, x_sample, cache_fox_kv, cache_fox_logf, cache_dsa_kv, cache_dsa_idx_k, state_conv,
           page_table, p_prompt, p_sample, w_in, b_f, w_a, w_b, w_c, w_out, ln_g, ln_b,
           conv_w, conv_b, conv_ln_g, conv_ln_b, w_pe, w_pe_gate):
    depth = w_in.shape[0]
    b, s, d = x_prompt.shape
    bd = x_sample.shape[0]
    n_pages = page_table.shape[1]
    p_len = n_pages * PAGE
    alpha = float((2 * depth) ** 0.25)
    tm = _pick(s, (512, 256, 128))
    tq = _pick(s, (256, 128))
    g_pages = _pick(n_pages, (8, 4, 2, 1))
    n_top_dec = min(TOPK_MAX, (p_len + 1) // 4)

    tabs_p = _rope_tables(jnp.arange(s, dtype=I32))
    tabs_s = _rope_tables(jnp.full((bd,), p_len, I32))
    page_table = page_table.astype(I32)

    hp = x_prompt.reshape(b * s, d)
    hs = x_sample.reshape(bd, d)
    hp_bf = hp.astype(BF16)
    hs_bf = hs.astype(BF16)
    outs_p, outs_s = [], []
    for l in range(depth):
        wg = _regroup_weights(w_in[l], b_f[l])
        wl = dict(w_mg=wg['w_mg'], w_a=w_a[l].astype(BF16), w_b=w_b[l].astype(BF16),
                  w_c=w_c[l].astype(BF16), w_out=w_out[l].astype(BF16), w_pe=w_pe[l].astype(BF16),
                  w_pe_gate=w_pe_gate[l].astype(BF16), ln_g=ln_g[l][None, :], ln_b=ln_b[l][None, :])
        cw, cb = conv_w[l], conv_b[l][None, :]
        cg, cbb = conv_ln_g[l][None, :], conv_ln_b[l][None, :]

        pp = _project(hp_bf, wg, tabs_p, tm, s // tm)
        logf3 = pp['logf'].reshape(b, s, H_A)
        cum = _cumsum(logf3, tq)
        cum_t = jnp.swapaxes(cum, 1, 2)
        ya = _fox_prompt(pp['q_bf'], pp['k_bf'], pp['v_bf'], cum, cum_t, b, s, tq)
        yb = _dsa_prompt(pp['qb_bf'], pp['qi_bf'], pp['wi'], pp['ki_bf'], pp['kvb_bf'], b, s, tq)
        yc = _conv_prompt(pp['u'], cw, cb, cg, cbb, b, s, tq)
        hp_new, hp_bf = _merge(hp, p_prompt[l].reshape(b * s, P_DIM), ya, yb, yc, pp['sg'], wl,
                               _pick(b * s, (256, 128)), alpha)
        outs_p.append((pp['kva'].reshape(b, s, 2, H_A, DH), logf3,
                       pp['kvb'].reshape(b, s, 2, KV_B, DH), pp['ki'].reshape(b, s, D_I),
                       pp['u'].reshape(b, s, D_C)[:, s - (CONV_W - 1):]))
        hp = hp_new

        ps = _project(hs_bf, wg, tabs_s, bd, 1)
        ya_s = _fox_decode(ps['q_bf'], ps['kva'], ps['logf'], page_table, cache_fox_kv[l],
                           cache_fox_logf[l], g_pages)
        sc, sc_new = _idx_decode(ps['qi_bf'], ps['wi'], ps['ki_bf'], page_table, cache_dsa_idx_k[l],
                                 g_pages)
        bias, bias_new = _sel_decode(sc, sc_new, n_top_dec)
        yb_s = _dsa_decode(ps['qb_bf'], ps['kvb'], bias, bias_new, page_table, cache_dsa_kv[l], g_pages)
        yc_s = _conv_decode(state_conv[l], ps['u'], cw, cb, cg, cbb)
        hs_new, hs_bf = _merge(hs, p_sample[l].reshape(bd, P_DIM), ya_s, yb_s, yc_s, ps['sg'], wl,
                               bd, alpha)
        outs_s.append((ps['kva'].reshape(bd, 1, 2, H_A, DH), ps['logf'].reshape(bd, 1, H_A),
                       ps['kvb'].reshape(bd, 1, 2, KV_B, DH), ps['ki'].reshape(bd, 1, D_I),
                       jnp.concatenate([state_conv[l][:, 1:], ps['u'][:, None, :]], axis=1)))
        hs = hs_new

    stack = lambda outs, k: jnp.stack([o[k] for o in outs])
    return (hp.reshape(b, s, d), hs.reshape(bd, 1, d),
            stack(outs_p, 0), stack(outs_s, 0), stack(outs_p, 1), stack(outs_s, 1),
            stack(outs_p, 2), stack(outs_s, 2), stack(outs_p, 3), stack(outs_s, 3),
            stack(outs_p, 4), stack(outs_s, 4))
```

```python
import functools

import jax
import jax.numpy as jnp
import numpy as np
from jax import lax
from jax.experimental import pallas as pl
from jax.experimental.pallas import tpu as pltpu

F32 = jnp.float32
BF16 = jnp.bfloat16
I32 = jnp.int32

D_MODEL = 1024
H_A = 8
DH = 64
W_A = H_A * DH
KV_B = 2
H_B = 8
G_B = H_B // KV_B
W_B = H_B * DH
H_I = 8
D_I = 64
TOPK_MAX = 256
D_C = 512
CONV_W = 31
P_DIM = 256
N_BRANCH = 3
ROPE_THETA = 500000.0
ROT_DIM = DH // 4
LN_EPS = 1e-5
PAGE = 128
LANES = 128
PADW = H_A * LANES

NEG = -1e30
INT_MIN = -(2 ** 31)
VMEM_LIMIT = 56 * 1024 * 1024

_SPLIT = (W_A, W_A, W_A, H_A, W_A, W_B, KV_B * DH, KV_B * DH, H_I * D_I, D_I, H_I, W_B,
          2 * D_C, D_C, N_BRANCH * D_MODEL)
_OFF = np.concatenate([[0], np.cumsum(_SPLIT)]).tolist()
(O_QA, O_KA, O_VA, O_FA, O_GA, O_QB, O_KB, O_VB, O_QI, O_KI, O_WI, O_GB, O_GLU, O_GC, O_MG,
 O_END) = _OFF


def _cparams(sem):
    return pltpu.CompilerParams(dimension_semantics=sem, vmem_limit_bytes=VMEM_LIMIT)


def _dot(a, b):
    return jnp.dot(a, b, preferred_element_type=F32)


def _dot_nt(a, b):
    return lax.dot_general(a, b, (((1,), (1,)), ((), ())), preferred_element_type=F32)


def _sigmoid(x):
    return 1.0 / (1.0 + jnp.exp(-x))


def _split3(x):
    hi = x.astype(BF16)
    r1 = x - hi.astype(F32)
    mid = r1.astype(BF16)
    lo = (r1 - mid.astype(F32)).astype(BF16)
    return hi, mid, lo


def _dot3(x, w):
    hi, mid, lo = _split3(x)
    return _dot(hi, w) + _dot(mid, w) + _dot(lo, w)


def _rope(x, c, s1, s2):
    n = x.shape[-1]
    half = ROT_DIM // 2
    return x * c + pltpu.roll(x, half, 1) * s1 + pltpu.roll(x, n - half, 1) * s2


def _tile_lanes(t, reps):
    return t if reps == 1 else jnp.concatenate([t] * reps, axis=1)


def _proj_foxq_body(x_ref, w_ref, bf_ref, q_ref, logf_ref):
    z = _dot(x_ref[...], w_ref[...])
    q_ref[...] = (z[:, :PADW] * (DH ** -0.5)).astype(BF16)
    f = z[:, PADW:] + bf_ref[...]
    logf = jnp.minimum(f, 0.0) - jnp.log1p(jnp.exp(-jnp.abs(f)))
    logf_ref[...] = logf[:, :H_A]


def _proj_foxkv_body(x_ref, w_ref, k_ref, kv_ref, va_ref):
    z = _dot(x_ref[...], w_ref[...])
    k_ref[...] = z[:, :W_A].astype(BF16)
    kv_ref[...] = z[:, :2 * W_A]
    vp = z[:, 2 * W_A:]
    lane = lax.broadcasted_iota(I32, vp.shape, 1)
    va_ref[...] = jnp.where(lane % LANES == DH, 1.0, vp).astype(BF16)


def _proj_gates_body(x_ref, w_ref, o_ref):
    z = _dot(x_ref[...], w_ref[...])
    o_ref[...] = z * _sigmoid(z)


def _proj_dsaq_body(x_ref, w_ref, c_ref, s1_ref, s2_ref, qb_ref, qi_ref):
    z = _dot(x_ref[...], w_ref[...])
    reps = z.shape[1] // LANES
    y = _rope(z, _tile_lanes(c_ref[...], reps), _tile_lanes(s1_ref[...], reps),
              _tile_lanes(s2_ref[...], reps)) * (DH ** -0.5)
    qb_ref[...] = y[:, :PADW].astype(BF16)
    qi_ref[...] = y[:, PADW:].astype(BF16)


def _proj_dsak_body(x_ref, w_ref, c_ref, s1_ref, s2_ref, kv_ref, kvbf_ref, ki_ref, kibf_ref, wi_ref):
    z = _dot(x_ref[...], w_ref[...])
    c, s1, s2 = c_ref[...], s1_ref[...], s2_ref[...]
    kb = _rope(z[:, :LANES], c, s1, s2)
    kv = jnp.concatenate([kb, z[:, LANES:2 * LANES]], axis=1)
    kv_ref[...] = kv
    kvbf_ref[...] = kv.astype(BF16)
    t = _rope(z[:, 2 * LANES:], c, s1, s2)
    ki_ref[...] = t[:, :D_I]
    lane = lax.broadcasted_iota(I32, t.shape, 1)
    kibf_ref[...] = jnp.where(lane < D_I, t, 0.0).astype(BF16)
    wi_ref[...] = z[:, 2 * LANES + D_I:2 * LANES + D_I + H_I] * (H_I ** -0.5)


def _proj_glu_body(x_ref, w_ref, u_ref):
    z = _dot(x_ref[...], w_ref[...])
    u_ref[...] = z[:, :D_C] * _sigmoid(z[:, D_C:])


def _proj_vt_body(x_ref, w_ref, o_ref):
    o_ref[0] = _dot_nt(w_ref[...], x_ref[...]).astype(BF16)


def _row_spec(tm, n):
    return pl.BlockSpec((tm, n), lambda i: (i, 0))


def _const_spec(shape):
    return pl.BlockSpec(shape, lambda i: (0,) * len(shape), pipeline_mode=pl.Buffered(1))


def _project(x_bf, wg, tabs, tm, tab_blocks):
    m = x_bf.shape[0]
    grid = (m // tm,)
    xs = _row_spec(tm, D_MODEL)
    tab_spec = pl.BlockSpec((tm, LANES), lambda i: (i % tab_blocks, 0))
    cp = _cparams(("parallel",))

    def sds(n, dt):
        return jax.ShapeDtypeStruct((m, n), dt)

    q_pad, logf = pl.pallas_call(
        _proj_foxq_body, grid=grid, name="proj_foxq",
        in_specs=[xs, _const_spec(wg['foxq'].shape), _const_spec((1, LANES))],
        out_specs=[_row_spec(tm, PADW), _row_spec(tm, H_A)],
        out_shape=[sds(PADW, BF16), sds(H_A, F32)],
        compiler_params=cp)(x_bf, wg['foxq'], wg['b_f'])
    k_bf, kva, v_aug = pl.pallas_call(
        _proj_foxkv_body, grid=grid, name="proj_foxkv",
        in_specs=[xs, _const_spec(wg['foxkv'].shape)],
        out_specs=[_row_spec(tm, W_A), _row_spec(tm, 2 * W_A), _row_spec(tm, PADW)],
        out_shape=[sds(W_A, BF16), sds(2 * W_A, F32), sds(PADW, BF16)],
        compiler_params=cp)(x_bf, wg['foxkv'])
    sg = pl.pallas_call(
        _proj_gates_body, grid=grid, name="proj_gates",
        in_specs=[xs, _const_spec(wg['gates'].shape)],
        out_specs=_row_spec(tm, 3 * W_A), out_shape=sds(3 * W_A, F32),
        compiler_params=cp)(x_bf, wg['gates'])
    qb_pad, qi_pad = pl.pallas_call(
        _proj_dsaq_body, grid=grid, name="proj_dsaq",
        in_specs=[xs, _const_spec(wg['dsaq'].shape), tab_spec, tab_spec, tab_spec],
        out_specs=[_row_spec(tm, PADW), _row_spec(tm, PADW)],
        out_shape=[sds(PADW, BF16), sds(PADW, BF16)],
        compiler_params=cp)(x_bf, wg['dsaq'], *tabs)
    kvb, kvb_bf, ki, ki_pad, wi = pl.pallas_call(
        _proj_dsak_body, grid=grid, name="proj_dsak",
        in_specs=[xs, _const_spec(wg['dsak'].shape), tab_spec, tab_spec, tab_spec],
        out_specs=[_row_spec(tm, 2 * LANES), _row_spec(tm, 2 * LANES), _row_spec(tm, D_I),
                   _row_spec(tm, LANES), _row_spec(tm, H_I)],
        out_shape=[sds(2 * LANES, F32), sds(2 * LANES, BF16), sds(D_I, F32), sds(LANES, BF16),
                   sds(H_I, F32)],
        compiler_params=cp)(x_bf, wg['dsak'], *tabs)
    u = pl.pallas_call(
        _proj_glu_body, grid=grid, name="proj_glu",
        in_specs=[xs, _const_spec(wg['glu'].shape)],
        out_specs=_row_spec(tm, D_C), out_shape=sds(D_C, F32),
        compiler_params=cp)(x_bf, wg['glu'])
    return dict(q_pad=q_pad, k_bf=k_bf, v_aug=v_aug, kva=kva, logf=logf, sg=sg, qb_pad=qb_pad,
                qi_pad=qi_pad, kvb=kvb, kvb_bf=kvb_bf, ki=ki, ki_pad=ki_pad, wi=wi, u=u)


def _project_vt(x_bf, w_vt, b, s, tm):
    nt = s // tm
    return pl.pallas_call(
        _proj_vt_body, grid=(b, nt), name="proj_vt",
        in_specs=[pl.BlockSpec((tm, D_MODEL), lambda i, j: (i * nt + j, 0)),
                  pl.BlockSpec(w_vt.shape, lambda i, j: (0, 0), pipeline_mode=pl.Buffered(1))],
        out_specs=pl.BlockSpec((1, LANES, tm), lambda i, j: (i, 0, j)),
        out_shape=jax.ShapeDtypeStruct((b, LANES, s), BF16),
        compiler_params=_cparams(("parallel", "parallel")))(x_bf, w_vt)


def _cumsum_body(x_ref, o_ref, carry_ref):
    @pl.when(pl.program_id(1) == 0)
    def _():
        carry_ref[...] = jnp.zeros_like(carry_ref)

    t = x_ref.shape[1]
    r = lax.broadcasted_iota(I32, (t, t), 0)
    c = lax.broadcasted_iota(I32, (t, t), 1)
    tri = jnp.where(c <= r, 1.0, 0.0).astype(BF16)
    hi, mid, lo = _split3(x_ref[0])
    cum = _dot(tri, hi) + _dot(tri, mid) + _dot(tri, lo) + carry_ref[...]
    o_ref[0] = cum
    carry_ref[...] = cum[t - 1:t, :]


def _cumsum(logf, tb):
    b, s, h = logf.shape
    return pl.pallas_call(
        _cumsum_body, grid=(b, s // tb), name="cumsum",
        in_specs=[pl.BlockSpec((1, tb, h), lambda i, j: (i, j, 0))],
        out_specs=pl.BlockSpec((1, tb, h), lambda i, j: (i, j, 0)),
        out_shape=jax.ShapeDtypeStruct((b, s, h), F32),
        scratch_shapes=[pltpu.VMEM((1, h), F32)],
        compiler_params=_cparams(("parallel", "arbitrary")))(logf)


def _fox_body(q_ref, k_ref, v_ref, cq_ref, ck_ref, o_ref, acc_ref, m_ref, cqb_ref, s_ref, p_ref,
              *, tq):
    i = pl.program_id(1)
    acc_ref[...] = jnp.zeros_like(acc_ref)
    m_ref[...] = jnp.full(m_ref.shape, NEG, F32)
    for h in range(H_A):
        cqb_ref[h] = jnp.broadcast_to(cq_ref[0, :, h:h + 1], (tq, LANES))

    def step(off, diag):
        if diag:
            row = lax.broadcasted_iota(I32, (tq, tq), 0)
            col = lax.broadcasted_iota(I32, (tq, tq), 1)
            causal = col <= row
        def qk(h):
            grp = slice((h // 2) * LANES, (h // 2 + 1) * LANES)
            k = k_ref[0, pl.ds(off, tq), grp]
            return _dot_nt(q_ref[:, h * LANES:(h + 1) * LANES], k)

        def pv_update(h, alpha):
            v = v_ref[0, pl.ds(off, tq), h * LANES:(h + 1) * LANES]
            acc_ref[h] = alpha * acc_ref[h] + _dot(p_ref[h % 2], v)

        s_ref[0] = qk(0)
        pending = None
        for h in range(H_A):
            if h + 1 < H_A:
                s_ref[(h + 1) % 2] = qk(h + 1)
            ck = ck_ref[0, h:h + 1, pl.ds(off, tq)]
            s = s_ref[h % 2] + _tile_lanes(cqb_ref[h], tq // LANES) - ck
            if diag:
                s = jnp.where(causal, s, NEG)
            m_old = m_ref[h]
            m_new = jnp.maximum(m_old, jnp.max(s, axis=1, keepdims=True))
            p_ref[h % 2] = jnp.exp(s - m_new).astype(BF16)
            m_ref[h] = m_new
            if pending is not None:
                pv_update(*pending)
            pending = (h, jnp.exp(m_old - m_new))
        pv_update(*pending)

    def body(c, carry):
        step(pl.multiple_of(c * tq, tq), False)
        return carry

    lax.fori_loop(0, i, body, 0)
    step(pl.multiple_of(i * tq, tq), True)

    lane = lax.broadcasted_iota(I32, (tq, LANES), 1)
    for j in range(H_A // 2):
        a0 = acc_ref[2 * j]
        a1 = acc_ref[2 * j + 1]
        o0 = a0 / a0[:, DH:DH + 1]
        o1 = a1 / a1[:, DH:DH + 1]
        o_ref[:, j * LANES:(j + 1) * LANES] = jnp.where(lane < DH, o0, pltpu.roll(o1, DH, 1))


def _fox_prompt(q_pad, k_bf, v_aug, cum, cum_t, b, s, tq):
    nq = s // tq
    k3 = k_bf.reshape(b, s, W_A)
    v3 = v_aug.reshape(b, s, PADW)
    return pl.pallas_call(
        functools.partial(_fox_body, tq=tq), grid=(b, nq), name="fox_prompt",
        in_specs=[pl.BlockSpec((tq, PADW), lambda i, j: (i * nq + j, 0)),
                  pl.BlockSpec((1, s, W_A), lambda i, j: (i, 0, 0)),
                  pl.BlockSpec((1, s, PADW), lambda i, j: (i, 0, 0)),
                  pl.BlockSpec((1, tq, H_A), lambda i, j: (i, j, 0)),
                  pl.BlockSpec((1, H_A, s), lambda i, j: (i, 0, 0))],
        out_specs=pl.BlockSpec((tq, W_A), lambda i, j: (i * nq + j, 0)),
        out_shape=jax.ShapeDtypeStruct((b * s, W_A), F32),
        scratch_shapes=[pltpu.VMEM((H_A, tq, LANES), F32), pltpu.VMEM((H_A, tq, 1), F32),
                        pltpu.VMEM((H_A, tq, LANES), F32), pltpu.VMEM((2, tq, tq), F32),
                        pltpu.VMEM((2, tq, tq), BF16)],
        compiler_params=_cparams(("parallel", "arbitrary")))(q_pad, k3, v3, cum, cum_t)


def _key_to_f32(key):
    bits = key ^ ((key >> 31) & 0x7FFFFFFF)
    return lax.bitcast_convert_type(bits, F32)


def _kth_largest(count_ge, shape, n_top):
    kf = jnp.float32(n_top)
    zero = jnp.zeros(shape, I32)
    t0 = jnp.where(count_ge(_key_to_f32(zero)) >= kf, zero, jnp.full(shape, INT_MIN, I32))

    def bit_step(it, t):
        cand = t + jnp.left_shift(jnp.int32(1), 30 - it)
        return jnp.where(count_ge(_key_to_f32(cand)) >= kf, cand, t)

    key = lax.fori_loop(0, 31, bit_step, t0)
    few = key == INT_MIN
    return jnp.where(few, -jnp.inf, _key_to_f32(key)), few


def _tie_cut(count_eq_below, shape, need, n_bits):
    def bit_step(it, j):
        cand = j + jnp.left_shift(jnp.int32(1), n_bits - 1 - it)
        return jnp.where(count_eq_below(cand) < need, cand, j)

    return lax.fori_loop(0, n_bits, bit_step, jnp.zeros(shape, I32))


def _dsa_body(qb_ref, qi_ref, wi_ref, ki_ref, k_ref, vt_ref, o_ref,
              sc_ref, bias_ref, acc_ref, m_ref, l_ref, ot_ref, s_ref, p_ref, *, tq, s_len, n_top):
    i = pl.program_id(1)
    nk = i + 1
    kidx0 = lax.broadcasted_iota(I32, (tq, tq), 0)
    qidx = i * tq + lax.broadcasted_iota(I32, (tq, tq), 1)
    wi = wi_ref[0]

    def score_chunk(c, _):
        off = pl.multiple_of(c * tq, tq)
        ki = ki_ref[0, pl.ds(off, tq), :]
        sc = jnp.zeros((tq, tq), F32)
        for h in range(H_I):
            r = _dot_nt(ki, qi_ref[:, h * LANES:(h + 1) * LANES])
            sc = sc + wi[h:h + 1, :] * jnp.maximum(r, 0.0)
        sc_ref[pl.ds(off, tq), :] = jnp.where(kidx0 + off <= qidx, sc, -jnp.inf)
        return 0

    lax.fori_loop(0, nk, score_chunk, 0)

    def count(pred):
        def body(c, cnt):
            off = pl.multiple_of(c * tq, tq)
            hit = jnp.where(pred(sc_ref[pl.ds(off, tq), :], off), 1.0, 0.0)
            return cnt + jnp.sum(hit, axis=0, keepdims=True)
        return lax.fori_loop(0, nk, body, jnp.zeros((1, tq), F32))

    thr, few = _kth_largest(lambda cand: count(lambda x, off: x >= cand), (1, tq), n_top)
    need = jnp.float32(n_top) - count(lambda x, off: x > thr)
    c_eq = count(lambda x, off: x == thr)
    has_tie = jnp.max(jnp.where((c_eq > need) & jnp.logical_not(few), 1.0, 0.0)) > 0.0

    def tie_path():
        n_bits = int(np.log2(s_len))
        return _tie_cut(lambda cand: count(lambda x, off: (x == thr) & (kidx0 + off < cand)),
                        (1, tq), need, n_bits)

    cut = lax.cond(has_tie, tie_path, lambda: jnp.full((1, tq), s_len, I32))

    def bias_chunk(c, _):
        off = pl.multiple_of(c * tq, tq)
        x = sc_ref[pl.ds(off, tq), :]
        kidx = kidx0 + off
        sel = ((x > thr) | ((x == thr) & (kidx <= cut))) & (kidx <= qidx)
        bias_ref[pl.ds(off, tq), :] = jnp.where(sel, 0.0, NEG)
        return 0

    lax.fori_loop(0, nk, bias_chunk, 0)

    acc_ref[...] = jnp.zeros_like(acc_ref)
    l_ref[...] = jnp.zeros_like(l_ref)
    m_ref[...] = jnp.full(m_ref.shape, NEG, F32)
    ones = jnp.ones((16, tq), BF16)

    def attend(c, _):
        off = pl.multiple_of(c * tq, tq)
        k = k_ref[0, pl.ds(off, tq), :]
        vt = jnp.concatenate([vt_ref[0, :, pl.ds(off, tq)], ones], axis=0)
        bias = bias_ref[pl.ds(off, tq), :]
        def qk(h):
            return _dot_nt(k, qb_ref[:, h * LANES:(h + 1) * LANES])

        def pv_update(h, alpha):
            n = h // G_B
            pv = _dot(vt, p_ref[h % 2])
            acc_ref[h] = alpha * acc_ref[h] + pv[n * DH:(n + 1) * DH, :]
            l_ref[h] = alpha * l_ref[h] + pv[LANES:LANES + 1, :]

        s_ref[0] = qk(0)
        pending = None
        for h in range(H_B):
            if h + 1 < H_B:
                s_ref[(h + 1) % 2] = qk(h + 1)
            s = s_ref[h % 2] + bias
            m_old = m_ref[h]
            m_new = jnp.maximum(m_old, jnp.max(s, axis=0, keepdims=True))
            alpha = jnp.exp(m_old - m_new)
            p_ref[h % 2] = jnp.exp(s - m_new).astype(BF16)
            m_ref[h] = m_new
            if pending is not None:
                pv_update(*pending)
            pending = (h, alpha)
        pv_update(*pending)
        return 0

    lax.fori_loop(0, nk, attend, 0)
    for h in range(H_B):
        ot_ref[h * DH:(h + 1) * DH, :] = acc_ref[h] / l_ref[h]
    o_ref[...] = ot_ref[...].T


def _dsa_prompt(qb_pad, qi_pad, wi_t, ki_pad, kvb_bf, vt, b, s, tq):
    nq = s // tq
    n_top = min(TOPK_MAX, s // 4)
    ki3 = ki_pad.reshape(b, s, LANES)
    kv3 = kvb_bf.reshape(b, s, 2 * LANES)
    rows = lambda i, j: (i * nq + j, 0)
    return pl.pallas_call(
        functools.partial(_dsa_body, tq=tq, s_len=s, n_top=n_top), grid=(b, nq), name="dsa_prompt",
        in_specs=[pl.BlockSpec((tq, PADW), rows), pl.BlockSpec((tq, PADW), rows),
                  pl.BlockSpec((1, H_I, tq), lambda i, j: (i, 0, j)),
                  pl.BlockSpec((1, s, LANES), lambda i, j: (i, 0, 0)),
                  pl.BlockSpec((1, s, LANES), lambda i, j: (i, 0, 0)),
                  pl.BlockSpec((1, LANES, s), lambda i, j: (i, 0, 0))],
        out_specs=pl.BlockSpec((tq, W_B), rows),
        out_shape=jax.ShapeDtypeStruct((b * s, W_B), F32),
        scratch_shapes=[pltpu.VMEM((s, tq), F32), pltpu.VMEM((s, tq), F32),
                        pltpu.VMEM((H_B, DH, tq), F32), pltpu.VMEM((H_B, 1, tq), F32),
                        pltpu.VMEM((H_B, 1, tq), F32), pltpu.VMEM((W_B, tq), F32),
                        pltpu.VMEM((2, tq, tq), F32), pltpu.VMEM((2, tq, tq), BF16)],
        compiler_params=_cparams(("parallel", "arbitrary")))(qb_pad, qi_pad, wi_t, ki3, kv3, vt)


HALO = 32


def _conv_ln_silu(acc, g, b):
    mu = jnp.mean(acc, axis=1, keepdims=True)
    d = acc - mu
    var = jnp.mean(d * d, axis=1, keepdims=True)
    y = d * lax.rsqrt(var + LN_EPS) * g + b
    return y * _sigmoid(y)


def _conv_body(u_ref, halo_ref, w_ref, cb_ref, g_ref, b_ref, o_ref, ext_ref, *, tq):
    j = pl.program_id(1)
    ext_ref[0:HALO, :] = jnp.where(j > 0, halo_ref[...], 0.0)
    ext_ref[HALO:, :] = u_ref[...]
    base = HALO - (CONV_W - 1)
    acc = jnp.zeros((tq, D_C), F32) + cb_ref[...]
    for t in range(CONV_W):
        acc = acc + ext_ref[base + t:base + t + tq, :] * w_ref[t:t + 1, :]
    o_ref[...] = _conv_ln_silu(acc, g_ref[...], b_ref[...])


def _conv_prompt(u, conv_w, conv_b, g, bb, b, s, tq):
    nq = s // tq
    hb = tq // HALO
    rows = lambda i, j: (i * nq + j, 0)
    return pl.pallas_call(
        functools.partial(_conv_body, tq=tq), grid=(b, nq), name="conv_prompt",
        in_specs=[pl.BlockSpec((tq, D_C), rows),
                  pl.BlockSpec((HALO, D_C), lambda i, j: (jnp.maximum((i * nq + j) * hb - 1, 0), 0)),
                  pl.BlockSpec((CONV_W, D_C), lambda i, j: (0, 0)),
                  pl.BlockSpec((1, D_C), lambda i, j: (0, 0)),
                  pl.BlockSpec((1, D_C), lambda i, j: (0, 0)),
                  pl.BlockSpec((1, D_C), lambda i, j: (0, 0))],
        out_specs=pl.BlockSpec((tq, D_C), rows),
        out_shape=jax.ShapeDtypeStruct((b * s, D_C), F32),
        scratch_shapes=[pltpu.VMEM((HALO + tq, D_C), F32)],
        compiler_params=_cparams(("parallel", "arbitrary")))(u, u, conv_w, conv_b, g, bb)


def _merge_body(x_ref, p_ref, ya_ref, yb_ref, yc_ref, sg_ref, wmg_ref, wa_ref, wb_ref, wc_ref,
                wo_ref, wpe_ref, wpg_ref, g_ref, b_ref, y_ref, ybf_ref, *, alpha):
    x = x_ref[...]
    gates = _sigmoid(_dot(x.astype(BF16), wmg_ref[...]))
    sg = sg_ref[...]
    ua = _dot((ya_ref[...] * sg[:, :W_A]).astype(BF16), wa_ref[...])
    ub = _dot((yb_ref[...] * sg[:, W_A:2 * W_A]).astype(BF16), wb_ref[...])
    uc = _dot((yc_ref[...] * sg[:, 2 * W_A:]).astype(BF16), wc_ref[...])
    m = (gates[:, :D_MODEL] * ua + gates[:, D_MODEL:2 * D_MODEL] * ub
         + gates[:, 2 * D_MODEL:] * uc)
    r = alpha * x + _dot(m.astype(BF16), wo_ref[...])
    mu = jnp.mean(r, axis=1, keepdims=True)
    d = r - mu
    var = jnp.mean(d * d, axis=1, keepdims=True)
    h = d * lax.rsqrt(var + LN_EPS) * g_ref[...] + b_ref[...]
    pe = _dot(p_ref[...].astype(BF16), wpe_ref[...])
    y = h + _sigmoid(_dot(h.astype(BF16), wpg_ref[...])) * pe
    y_ref[...] = y
    ybf_ref[...] = y.astype(BF16)


def _merge(x, p, ya, yb, yc, sg, wl, tm, alpha):
    m = x.shape[0]
    ws = [wl['w_mg'], wl['w_a'], wl['w_b'], wl['w_c'], wl['w_out'], wl['w_pe'], wl['w_pe_gate'],
          wl['ln_g'], wl['ln_b']]
    return pl.pallas_call(
        functools.partial(_merge_body, alpha=alpha), grid=(m // tm,), name="merge",
        in_specs=[_row_spec(tm, D_MODEL), _row_spec(tm, P_DIM), _row_spec(tm, W_A),
                  _row_spec(tm, W_B), _row_spec(tm, D_C), _row_spec(tm, 3 * W_A)]
                 + [_const_spec(w.shape) for w in ws],
        out_specs=[_row_spec(tm, D_MODEL), _row_spec(tm, D_MODEL)],
        out_shape=[jax.ShapeDtypeStruct((m, D_MODEL), F32), jax.ShapeDtypeStruct((m, D_MODEL), BF16)],
        compiler_params=_cparams(("parallel",)))(x, p, ya, yb, yc, sg, *ws)


def _fox_dec_body(pt_ref, q_ref, kvn_ref, lfn_ref, *refs, g, layer):
    kv_refs = refs[:g]
    lf_refs = refs[g:2 * g]
    o_ref = refs[2 * g]
    m_ref, l_ref, acc_ref, carry_ref = refs[2 * g + 1:]
    t = pl.program_id(1)
    qrows = q_ref[0]

    @pl.when(t == 0)
    def _():
        kn = kvn_ref[0][:, :W_A].astype(BF16).astype(F32)
        vn = kvn_ref[0][:, W_A:]
        s_new = jnp.sum(qrows.astype(F32) * kn, axis=1, keepdims=True)
        m_ref[...] = s_new
        l_ref[...] = jnp.ones_like(l_ref)
        acc_ref[...] = jnp.broadcast_to(vn.astype(BF16).astype(F32), (H_A, W_A))
        carry_ref[...] = lfn_ref[0]

    r = lax.broadcasted_iota(I32, (PAGE, PAGE), 0)
    c = lax.broadcasted_iota(I32, (PAGE, PAGE), 1)
    upper = jnp.where(r > c, 1.0, 0.0).astype(BF16)
    lf = jnp.concatenate([lf_refs[gi][0, 0] for gi in range(g)], axis=0)
    suf = _dot3(lf, upper)
    tot = jnp.sum(lf, axis=1, keepdims=True)
    carry = carry_ref[...]
    parts = []
    for gi in range(g):
        kt = kv_refs[gi][0, 0, :W_A, :].astype(BF16)
        bias = suf[gi * H_A:(gi + 1) * H_A, :] + carry
        carry = carry + tot[gi * H_A:(gi + 1) * H_A, :]
        parts.append(_dot(qrows, kt) + bias)
    carry_ref[...] = carry
    s = jnp.concatenate(parts, axis=1)
    m_old = m_ref[...]
    m_new = jnp.maximum(m_old, jnp.max(s, axis=1, keepdims=True))
    alpha = jnp.exp(m_old - m_new)
    p = jnp.exp(s - m_new)
    l_ref[...] = alpha * l_ref[...] + jnp.sum(p, axis=1, keepdims=True)
    pb = p.astype(BF16)
    acc = alpha * acc_ref[...]
    for gi in range(g):
        vt = kv_refs[gi][0, 0, W_A:, :].astype(BF16)
        acc = acc + _dot_nt(pb[:, gi * PAGE:(gi + 1) * PAGE], vt)
    acc_ref[...] = acc
    m_ref[...] = m_new

    @pl.when(t == pl.num_programs(1) - 1)
    def _():
        o_ref[0] = acc_ref[...] / l_ref[...]


def _fox_decode(qrows, kva, logf, page_table, kv_t, lf_t, g, layer):
    bd, n_pages = page_table.shape
    steps = n_pages // g

    def page_map(gi):
        return lambda b, t, pt: (layer, pt[b, n_pages - 1 - (t * g + gi)], 0, 0)

    per_b = lambda b, t, pt: (b, 0, 0)
    out = pl.pallas_call(
        functools.partial(_fox_dec_body, g=g, layer=layer), name="fox_decode",
        grid_spec=pltpu.PrefetchScalarGridSpec(
            num_scalar_prefetch=1, grid=(bd, steps),
            in_specs=[pl.BlockSpec((1, H_A, W_A), per_b), pl.BlockSpec((1, 1, 2 * W_A), per_b),
                      pl.BlockSpec((1, H_A, 1), per_b)]
                     + [pl.BlockSpec((1, 1, 2 * W_A, PAGE), page_map(gi)) for gi in range(g)]
                     + [pl.BlockSpec((1, 1, H_A, PAGE), page_map(gi)) for gi in range(g)],
            out_specs=pl.BlockSpec((1, H_A, W_A), per_b),
            scratch_shapes=[pltpu.VMEM((H_A, 1), F32), pltpu.VMEM((H_A, 1), F32),
                            pltpu.VMEM((H_A, W_A), F32), pltpu.VMEM((H_A, 1), F32)]),
        out_shape=jax.ShapeDtypeStruct((bd, H_A, W_A), F32),
        compiler_params=_cparams(("parallel", "arbitrary")))(
            page_table, qrows, kva.reshape(bd, 1, 2 * W_A), logf.reshape(bd, H_A, 1),
            *([kv_t] * g), *([lf_t] * g))
    o4 = out.reshape(bd, H_A, H_A, DH)
    return jnp.stack([o4[:, h, h] for h in range(H_A)], axis=1).reshape(bd, W_A)


def _idx_dec_body(pt_ref, qi_ref, wi_ref, kin_ref, *refs, g):
    ki_refs = refs[:g]
    o_ref, on_ref = refs[g:]
    qi = qi_ref[0]
    w = wi_ref[0]
    rows = []
    for gi in range(g):
        r = _dot(qi, ki_refs[gi][0, 0].astype(BF16))
        rows.append(jnp.sum(w * jnp.maximum(r, 0.0), axis=0, keepdims=True))
    o_ref[0] = jnp.concatenate(rows, axis=0)
    r_new = jnp.sum(qi.astype(F32) * kin_ref[0].astype(F32), axis=1, keepdims=True)
    sc_new = jnp.sum(w * jnp.maximum(r_new, 0.0), axis=0, keepdims=True)
    on_ref[0] = jnp.broadcast_to(sc_new, (1, LANES))


def _idx_decode(qi_bf, wi, ki_new_bf, page_table, ik_t, g, layer):
    bd, n_pages = page_table.shape
    steps = n_pages // g

    def page_map(gi):
        return lambda b, t, pt: (layer, pt[b, t * g + gi], 0, 0)

    per_b = lambda b, t, pt: (b, 0, 0)
    sc, sc_new = pl.pallas_call(
        functools.partial(_idx_dec_body, g=g), name="idx_decode",
        grid_spec=pltpu.PrefetchScalarGridSpec(
            num_scalar_prefetch=1, grid=(bd, steps),
            in_specs=[pl.BlockSpec((1, H_I, D_I), per_b), pl.BlockSpec((1, H_I, 1), per_b),
                      pl.BlockSpec((1, 1, D_I), per_b)]
                     + [pl.BlockSpec((1, 1, D_I, PAGE), page_map(gi)) for gi in range(g)],
            out_specs=[pl.BlockSpec((1, g, PAGE), lambda b, t, pt: (b, t, 0)),
                       pl.BlockSpec((1, 1, LANES), per_b)]),
        out_shape=[jax.ShapeDtypeStruct((bd, n_pages, PAGE), F32),
                   jax.ShapeDtypeStruct((bd, 1, LANES), F32)],
        compiler_params=_cparams(("parallel", "arbitrary")))(
            page_table, qi_bf, wi.reshape(bd, H_I, 1), ki_new_bf.reshape(bd, 1, D_I),
            *([ik_t] * g))
    return sc.reshape(bd, n_pages * PAGE), sc_new[:, 0, :1]


def _sel_dec_body(sc_ref, scn_ref, bias_ref, biasn_ref, *, n_top, n_bits):
    bd, p_len = sc_ref.shape
    sc = sc_ref[...]
    sc_n = scn_ref[...]
    idx = lax.broadcasted_iota(I32, (bd, p_len), 1)

    def count(pred, pred_n):
        return (jnp.sum(jnp.where(pred, 1.0, 0.0), axis=1, keepdims=True)
                + jnp.where(pred_n, 1.0, 0.0))

    thr, _ = _kth_largest(lambda cand: count(sc >= cand, sc_n >= cand), (bd, 1), n_top)
    need = jnp.float32(n_top) - count(sc > thr, sc_n > thr)
    cut = _tie_cut(lambda cand: jnp.sum(jnp.where((sc == thr) & (idx < cand), 1.0, 0.0),
                                        axis=1, keepdims=True), (bd, 1), need, n_bits)
    eq_past = jnp.sum(jnp.where(sc == thr, 1.0, 0.0), axis=1, keepdims=True)
    sel = (sc > thr) | ((sc == thr) & (idx <= cut))
    bias_ref[...] = jnp.where(sel, 0.0, NEG)
    sel_n = (sc_n > thr) | ((sc_n == thr) & (eq_past < need))
    biasn_ref[...] = jnp.where(sel_n, 0.0, NEG)


def _sel_decode(sc, sc_new, n_top):
    bd, p_len = sc.shape
    n_bits = int(np.log2(p_len))
    return pl.pallas_call(
        functools.partial(_sel_dec_body, n_top=n_top, n_bits=n_bits), name="sel_decode",
        out_shape=[jax.ShapeDtypeStruct((bd, p_len), F32), jax.ShapeDtypeStruct((bd, 1), F32)],
        compiler_params=pltpu.CompilerParams(vmem_limit_bytes=VMEM_LIMIT))(sc, sc_new)


def _dsa_dec_body(pt_ref, q_ref, kvn_ref, bn_ref, bias_ref, *refs, g):
    kv_refs = refs[:g]
    o_ref = refs[g]
    m_ref, l_ref, acc_ref = refs[g + 1:]
    t = pl.program_id(1)
    qrows = q_ref[0]

    @pl.when(t == 0)
    def _():
        kn = kvn_ref[0][:, :LANES].astype(BF16).astype(F32)
        vn = kvn_ref[0][:, LANES:].astype(BF16).astype(F32)
        s_new = jnp.sum(qrows.astype(F32) * kn, axis=1, keepdims=True) + bn_ref[0]
        m_ref[...] = s_new
        l_ref[...] = jnp.where(s_new > 0.5 * NEG, 1.0, 0.0)
        acc_ref[...] = jnp.where(s_new > 0.5 * NEG, jnp.broadcast_to(vn, (H_B, LANES)), 0.0)

    parts = []
    for gi in range(g):
        kt = kv_refs[gi][0, 0, :LANES, :].astype(BF16)
        parts.append(_dot(qrows, kt) + bias_ref[0, gi:gi + 1, :])
    s = jnp.concatenate(parts, axis=1)
    m_old = m_ref[...]
    m_new = jnp.maximum(m_old, jnp.max(s, axis=1, keepdims=True))
    alpha = jnp.exp(m_old - m_new)
    p = jnp.where(s > 0.5 * NEG, jnp.exp(s - m_new), 0.0)
    l_ref[...] = alpha * l_ref[...] + jnp.sum(p, axis=1, keepdims=True)
    pb = p.astype(BF16)
    acc = alpha * acc_ref[...]
    for gi in range(g):
        vt = kv_refs[gi][0, 0, LANES:, :].astype(BF16)
        acc = acc + _dot_nt(pb[:, gi * PAGE:(gi + 1) * PAGE], vt)
    acc_ref[...] = acc
    m_ref[...] = m_new

    @pl.when(t == pl.num_programs(1) - 1)
    def _():
        o_ref[0] = acc_ref[...] / l_ref[...]


def _dsa_decode(q2, kvb, bias, bias_new, page_table, kv_t, g, layer):
    bd, n_pages = page_table.shape
    steps = n_pages // g

    def page_map(gi):
        return lambda b, t, pt: (layer, pt[b, t * g + gi], 0, 0)

    per_b = lambda b, t, pt: (b, 0, 0)
    out = pl.pallas_call(
        functools.partial(_dsa_dec_body, g=g), name="dsa_decode",
        grid_spec=pltpu.PrefetchScalarGridSpec(
            num_scalar_prefetch=1, grid=(bd, steps),
            in_specs=[pl.BlockSpec((1, H_B, LANES), per_b), pl.BlockSpec((1, 1, 2 * LANES), per_b),
                      pl.BlockSpec((1, 1, 1), per_b),
                      pl.BlockSpec((1, g, PAGE), lambda b, t, pt: (b, t, 0))]
                     + [pl.BlockSpec((1, 1, 2 * LANES, PAGE), page_map(gi)) for gi in range(g)],
            out_specs=pl.BlockSpec((1, H_B, LANES), per_b),
            scratch_shapes=[pltpu.VMEM((H_B, 1), F32), pltpu.VMEM((H_B, 1), F32),
                            pltpu.VMEM((H_B, LANES), F32)]),
        out_shape=jax.ShapeDtypeStruct((bd, H_B, LANES), F32),
        compiler_params=_cparams(("parallel", "arbitrary")))(
            page_table, q2, kvb.reshape(bd, 1, 2 * LANES), bias_new.reshape(bd, 1, 1),
            bias.reshape(bd, n_pages, PAGE), *([kv_t] * g))
    o4 = out.reshape(bd, KV_B, G_B, KV_B, DH)
    return jnp.stack([o4[:, n, :, n] for n in range(KV_B)], axis=1).reshape(bd, W_B)


def _conv_dec_body(st_ref, u_ref, w_ref, cb_ref, g_ref, b_ref, o_ref):
    acc = u_ref[...] * w_ref[CONV_W - 1:CONV_W, :] + cb_ref[...]
    for t in range(CONV_W - 1):
        acc = acc + st_ref[0, t] * w_ref[t:t + 1, :]
    o_ref[...] = _conv_ln_silu(acc, g_ref[...], b_ref[...])


def _conv_decode(state_t, u, conv_w, conv_b, g, bb, layer):
    bd = u.shape[0]
    full = lambda a: pl.BlockSpec(a.shape, lambda i: (0,) * a.ndim)
    return pl.pallas_call(
        _conv_dec_body, grid=(1,), name="conv_decode",
        in_specs=[pl.BlockSpec((1,) + state_t.shape[1:], lambda i: (layer, 0, 0, 0)),
                  full(u), full(conv_w), full(conv_b), full(g), full(bb)],
        out_specs=pl.BlockSpec((bd, D_C), lambda i: (0, 0)),
        out_shape=jax.ShapeDtypeStruct((bd, D_C), F32),
        compiler_params=_cparams(("arbitrary",)))(state_t, u, conv_w, conv_b, g, bb)


def _rope_tables(pos):
    half = ROT_DIM // 2
    inv = ROPE_THETA ** (-jnp.arange(half, dtype=F32) / half)
    ang = pos.astype(F32)[:, None] * inv[None, :]
    cos, sin = jnp.cos(ang), jnp.sin(ang)
    n = pos.shape[0]
    pad = jnp.zeros((n, DH - ROT_DIM), F32)
    c = jnp.concatenate([cos, cos, pad + 1.0], axis=1)
    s1 = jnp.concatenate([jnp.zeros_like(sin), sin, pad], axis=1)
    s2 = jnp.concatenate([-sin, jnp.zeros_like(sin), pad], axis=1)
    return tuple(jnp.tile(t, (1, LANES // DH)) for t in (c, s1, s2))


def _pad_heads(w, half_of_head):
    d = w.shape[0]
    z = jnp.zeros((d, DH), w.dtype)
    cols = []
    for h in range(w.shape[1] // DH):
        wh = w[:, h * DH:(h + 1) * DH]
        cols += [wh, z] if half_of_head(h) == 0 else [z, wh]
    return jnp.concatenate(cols, axis=1)


def _regroup_weights(w_in, b_f):
    w = w_in.astype(BF16)
    d = w.shape[0]
    z = lambda n: jnp.zeros((d, n), BF16)
    w_va = w[:, O_VA:O_FA]
    foxq = jnp.concatenate([_pad_heads(w[:, O_QA:O_KA], lambda h: h % 2), w[:, O_FA:O_GA],
                            z(LANES - H_A)], axis=1)
    foxkv = jnp.concatenate([w[:, O_KA:O_VA], w_va, _pad_heads(w_va, lambda h: 0)], axis=1)
    gates = jnp.concatenate([w[:, O_GA:O_QB], w[:, O_GB:O_GLU], w[:, O_GC:O_MG]], axis=1)
    dsaq = jnp.concatenate([_pad_heads(w[:, O_QB:O_KB], lambda h: h // G_B),
                            _pad_heads(w[:, O_QI:O_KI], lambda h: 0)], axis=1)
    dsak = jnp.concatenate([w[:, O_KB:O_QI], w[:, O_KI:O_GB], z(LANES - D_I - H_I)], axis=1)
    glu = w[:, O_GLU:O_GC]
    vt = jnp.transpose(w[:, O_VB:O_QI])
    bf = jnp.concatenate([b_f.astype(F32), jnp.zeros((LANES - H_A,), F32)])[None, :]
    return dict(foxq=foxq, foxkv=foxkv, gates=gates, dsaq=dsaq, dsak=dsak, glu=glu, vt=vt, b_f=bf,
                w_mg=w[:, O_MG:O_END])


def _pick(n, prefs):
    for t in prefs:
        if n % t == 0:
            return t
    return n


def kernel(x_prompt, x_sample, cache_fox_kv, cache_fox_logf, cache_dsa_kv, cache_dsa_idx_k, state_conv,
           page_table, p_prompt, p_sample, w_in, b_f, w_a, w_b, w_c, w_out, ln_g, ln_b,
           conv_w, conv_b, conv_ln_g, conv_ln_b, w_pe, w_pe_gate):
    depth = w_in.shape[0]
    b, s, d = x_prompt.shape
    bd = x_sample.shape[0]
    n_phys = cache_fox_kv.shape[1]
    n_pages = page_table.shape[1]
    p_len = n_pages * PAGE
    alpha = float((2 * depth) ** 0.25)
    tm = _pick(s, (512, 256, 128))
    tq = _pick(s, (256, 128))
    g_fox = _pick(n_pages, (8, 4, 2, 1))
    g_dsa = _pick(n_pages, (16, 8, 4, 2, 1))
    n_top_dec = min(TOPK_MAX, (p_len + 1) // 4)

    tabs_p = _rope_tables(jnp.arange(s, dtype=I32))
    tabs_s = _rope_tables(jnp.full((bd,), p_len, I32))
    page_table = page_table.astype(I32)

    fox_kv_t = jnp.transpose(cache_fox_kv, (0, 1, 3, 4, 5, 2)).reshape(depth, n_phys, 2 * W_A, PAGE)
    fox_lf_t = jnp.transpose(cache_fox_logf, (0, 1, 3, 2))
    dsa_kv_t = jnp.transpose(cache_dsa_kv, (0, 1, 3, 4, 5, 2)).reshape(depth, n_phys, 2 * LANES, PAGE)
    dsa_ik_t = jnp.transpose(cache_dsa_idx_k, (0, 1, 3, 2))
    state_t = jnp.transpose(state_conv, (0, 2, 1, 3))
    head_eye = jnp.repeat(jnp.eye(H_A, dtype=BF16), DH, axis=1)

    hp = x_prompt.reshape(b * s, d)
    hs = x_sample.reshape(bd, d)
    hp_bf = hp.astype(BF16)
    hs_bf = hs.astype(BF16)
    outs_p, outs_s = [], []
    for l in range(depth):
        wg = _regroup_weights(w_in[l], b_f[l])
        wl = dict(w_mg=wg['w_mg'], w_a=w_a[l].astype(BF16), w_b=w_b[l].astype(BF16),
                  w_c=w_c[l].astype(BF16), w_out=w_out[l].astype(BF16), w_pe=w_pe[l].astype(BF16),
                  w_pe_gate=w_pe_gate[l].astype(BF16), ln_g=ln_g[l][None, :], ln_b=ln_b[l][None, :])
        cw, cb = conv_w[l], conv_b[l][None, :]
        cg, cbb = conv_ln_g[l][None, :], conv_ln_b[l][None, :]

        pp = _project(hp_bf, wg, tabs_p, tm, s // tm)
        vt = _project_vt(hp_bf, wg['vt'], b, s, tm)
        logf3 = pp['logf'].reshape(b, s, H_A)
        cum = _cumsum(logf3, tq)
        cum_t = jnp.swapaxes(cum, 1, 2)
        wi_t = jnp.swapaxes(pp['wi'].reshape(b, s, H_I), 1, 2)
        ya = _fox_prompt(pp['q_pad'], pp['k_bf'], pp['v_aug'], cum, cum_t, b, s, tq)
        yb = _dsa_prompt(pp['qb_pad'], pp['qi_pad'], wi_t, pp['ki_pad'], pp['kvb_bf'], vt, b, s, tq)
        yc = _conv_prompt(pp['u'], cw, cb, cg, cbb, b, s, tq)
        hp_new, hp_bf = _merge(hp, p_prompt[l].reshape(b * s, P_DIM), ya, yb, yc, pp['sg'], wl,
                               _pick(b * s, (256, 128)), alpha)
        outs_p.append((pp['kva'].reshape(b, s, 2, H_A, DH), logf3,
                       pp['kvb'].reshape(b, s, 2, KV_B, DH), pp['ki'].reshape(b, s, D_I),
                       pp['u'].reshape(b, s, D_C)[:, s - (CONV_W - 1):]))
        hp = hp_new

        ps = _project(hs_bf, wg, tabs_s, bd, 1)
        q8 = ps['q_pad'].reshape(bd, H_A, 2, DH)
        q_s = jnp.stack([q8[:, h, h % 2] for h in range(H_A)], axis=1).reshape(bd, 1, W_A)
        qrows = q_s * head_eye[None]
        ya_s = _fox_decode(qrows, ps['kva'], ps['logf'], page_table, fox_kv_t, fox_lf_t, g_fox, l)
        qi_s = ps['qi_pad'].reshape(bd, H_I, LANES)[:, :, :D_I]
        sc, sc_new = _idx_decode(qi_s, ps['wi'], ps['ki_pad'][:, :D_I], page_table, dsa_ik_t,
                                 g_dsa, l)
        bias, bias_new = _sel_decode(sc, sc_new, n_top_dec)
        yb_s = _dsa_decode(ps['qb_pad'].reshape(bd, H_B, LANES), ps['kvb'], bias, bias_new,
                           page_table, dsa_kv_t, g_dsa, l)
        yc_s = _conv_decode(state_t, ps['u'], cw, cb, cg, cbb, l)
        hs_new, hs_bf = _merge(hs, p_sample[l].reshape(bd, P_DIM), ya_s, yb_s, yc_s, ps['sg'], wl,
                               bd, alpha)
        outs_s.append((ps['kva'].reshape(bd, 1, 2, H_A, DH), ps['logf'].reshape(bd, 1, H_A),
                       ps['kvb'].reshape(bd, 1, 2, KV_B, DH), ps['ki'].reshape(bd, 1, D_I),
                       jnp.concatenate([state_t[l, 1:], ps['u'][None]], axis=0)))
        hs = hs_new

    stack = lambda outs, k: jnp.stack([o[k] for o in outs])
    return (hp.reshape(b, s, d), hs.reshape(bd, 1, d),
            stack(outs_p, 0), stack(outs_s, 0), stack(outs_p, 1), stack(outs_s, 1),
            stack(outs_p, 2), stack(outs_s, 2), stack(outs_p, 3), stack(outs_s, 3),
            stack(outs_p, 4), jnp.transpose(stack(outs_s, 4), (0, 2, 1, 3)))
```

```python
import functools

import jax
import jax.numpy as jnp
import numpy as np
from jax import lax
from jax.experimental import pallas as pl
from jax.experimental.pallas import tpu as pltpu

F32 = jnp.float32
BF16 = jnp.bfloat16
I32 = jnp.int32

D_MODEL = 1024
H_A = 8
DH = 64
W_A = H_A * DH
KV_B = 2
H_B = 8
G_B = H_B // KV_B
W_B = H_B * DH
H_I = 8
D_I = 64
TOPK_MAX = 256
D_C = 512
CONV_W = 31
P_DIM = 256
N_BRANCH = 3
ROPE_THETA = 500000.0
ROT_DIM = DH // 4
LN_EPS = 1e-5
PAGE = 128
LANES = 128
PADW = H_A * LANES

PIPE_AHEAD = 4
PIPE_LAG = 4
PIPE_SLOTS = 8

NEG = -1e30
INT_MIN = -(2 ** 31)
VMEM_LIMIT = 56 * 1024 * 1024

_SPLIT = (W_A, W_A, W_A, H_A, W_A, W_B, KV_B * DH, KV_B * DH, H_I * D_I, D_I, H_I, W_B,
          2 * D_C, D_C, N_BRANCH * D_MODEL)
_OFF = np.concatenate([[0], np.cumsum(_SPLIT)]).tolist()
(O_QA, O_KA, O_VA, O_FA, O_GA, O_QB, O_KB, O_VB, O_QI, O_KI, O_WI, O_GB, O_GLU, O_GC, O_MG,
 O_END) = _OFF


def _cparams(sem):
    return pltpu.CompilerParams(dimension_semantics=sem, vmem_limit_bytes=VMEM_LIMIT)


def _dot(a, b):
    return jnp.dot(a, b, preferred_element_type=F32)


def _dot_nt(a, b):
    return lax.dot_general(a, b, (((1,), (1,)), ((), ())), preferred_element_type=F32)


def _sigmoid(x):
    return 1.0 / (1.0 + jnp.exp(-x))


def _split3(x):
    hi = x.astype(BF16)
    r1 = x - hi.astype(F32)
    mid = r1.astype(BF16)
    lo = (r1 - mid.astype(F32)).astype(BF16)
    return hi, mid, lo


def _dot3(x, w):
    hi, mid, lo = _split3(x)
    return _dot(hi, w) + _dot(mid, w) + _dot(lo, w)


def _rope(x, c, s1, s2):
    n = x.shape[-1]
    half = ROT_DIM // 2
    return x * c + pltpu.roll(x, half, 1) * s1 + pltpu.roll(x, n - half, 1) * s2


def _tile_lanes(t, reps):
    return t if reps == 1 else jnp.concatenate([t] * reps, axis=1)


def _proj_foxq_body(x_ref, w_ref, bf_ref, q_ref, logf_ref):
    z = _dot(x_ref[...], w_ref[...])
    q_ref[...] = (z[:, :PADW] * (DH ** -0.5)).astype(BF16)
    f = z[:, PADW:] + bf_ref[...]
    logf = jnp.minimum(f, 0.0) - jnp.log1p(jnp.exp(-jnp.abs(f)))
    logf_ref[...] = logf[:, :H_A]


def _proj_foxkv_body(x_ref, w_ref, k_ref, kv_ref):
    z = _dot(x_ref[...], w_ref[...])
    k_ref[...] = z[:, :W_A].astype(BF16)
    kv_ref[...] = z


def _proj_gates_body(x_ref, w_ref, o_ref):
    z = _dot(x_ref[...], w_ref[...])
    o_ref[...] = z * _sigmoid(z)


def _proj_dsaq_body(x_ref, w_ref, c_ref, s1_ref, s2_ref, qb_ref, qi_ref):
    z = _dot(x_ref[...], w_ref[...])
    reps = z.shape[1] // LANES
    y = _rope(z, _tile_lanes(c_ref[...], reps), _tile_lanes(s1_ref[...], reps),
              _tile_lanes(s2_ref[...], reps)) * (DH ** -0.5)
    qb_ref[...] = y[:, :PADW].astype(BF16)
    qi_ref[...] = y[:, PADW:].astype(BF16)


def _proj_dsak_body(x_ref, w_ref, c_ref, s1_ref, s2_ref, kv_ref, kvbf_ref, ki_ref, kibf_ref, wi_ref):
    z = _dot(x_ref[...], w_ref[...])
    c, s1, s2 = c_ref[...], s1_ref[...], s2_ref[...]
    kb = _rope(z[:, :LANES], c, s1, s2)
    kv = jnp.concatenate([kb, z[:, LANES:2 * LANES]], axis=1)
    kv_ref[...] = kv
    kvbf_ref[...] = kv.astype(BF16)
    t = _rope(z[:, 2 * LANES:], c, s1, s2)
    ki_ref[...] = t[:, :D_I]
    lane = lax.broadcasted_iota(I32, t.shape, 1)
    kibf_ref[...] = jnp.where(lane < D_I, t, 0.0).astype(BF16)
    wi_ref[...] = z[:, 2 * LANES + D_I:2 * LANES + D_I + H_I] * (H_I ** -0.5)


def _proj_glu_body(x_ref, w_ref, u_ref):
    z = _dot(x_ref[...], w_ref[...])
    u_ref[...] = z[:, :D_C] * _sigmoid(z[:, D_C:])


def _proj_vt_body(x_ref, w_ref, vb_ref, va_ref):
    zt = _dot_nt(w_ref[...], x_ref[...]).astype(BF16)
    vb_ref[0] = zt[:LANES]
    va_ref[0] = zt[LANES:]


def _row_spec(tm, n):
    return pl.BlockSpec((tm, n), lambda i: (i, 0))


def _const_spec(shape):
    return pl.BlockSpec(shape, lambda i: (0,) * len(shape), pipeline_mode=pl.Buffered(1))


def _project(x_bf, wg, tabs, tm, tab_blocks):
    m = x_bf.shape[0]
    grid = (m // tm,)
    xs = _row_spec(tm, D_MODEL)
    tab_spec = pl.BlockSpec((tm, LANES), lambda i: (i % tab_blocks, 0))
    cp = _cparams(("parallel",))

    def sds(n, dt):
        return jax.ShapeDtypeStruct((m, n), dt)

    q_pad, logf = pl.pallas_call(
        _proj_foxq_body, grid=grid, name="proj_foxq",
        in_specs=[xs, _const_spec(wg['foxq'].shape), _const_spec((1, LANES))],
        out_specs=[_row_spec(tm, PADW), _row_spec(tm, H_A)],
        out_shape=[sds(PADW, BF16), sds(H_A, F32)],
        compiler_params=cp)(x_bf, wg['foxq'], wg['b_f'])
    k_bf, kva = pl.pallas_call(
        _proj_foxkv_body, grid=grid, name="proj_foxkv",
        in_specs=[xs, _const_spec(wg['foxkv'].shape)],
        out_specs=[_row_spec(tm, W_A), _row_spec(tm, 2 * W_A)],
        out_shape=[sds(W_A, BF16), sds(2 * W_A, F32)],
        compiler_params=cp)(x_bf, wg['foxkv'])
    sg = pl.pallas_call(
        _proj_gates_body, grid=grid, name="proj_gates",
        in_specs=[xs, _const_spec(wg['gates'].shape)],
        out_specs=_row_spec(tm, 3 * W_A), out_shape=sds(3 * W_A, F32),
        compiler_params=cp)(x_bf, wg['gates'])
    qb_pad, qi_pad = pl.pallas_call(
        _proj_dsaq_body, grid=grid, name="proj_dsaq",
        in_specs=[xs, _const_spec(wg['dsaq'].shape), tab_spec, tab_spec, tab_spec],
        out_specs=[_row_spec(tm, PADW), _row_spec(tm, PADW)],
        out_shape=[sds(PADW, BF16), sds(PADW, BF16)],
        compiler_params=cp)(x_bf, wg['dsaq'], *tabs)
    kvb, kvb_bf, ki, ki_pad, wi = pl.pallas_call(
        _proj_dsak_body, grid=grid, name="proj_dsak",
        in_specs=[xs, _const_spec(wg['dsak'].shape), tab_spec, tab_spec, tab_spec],
        out_specs=[_row_spec(tm, 2 * LANES), _row_spec(tm, 2 * LANES), _row_spec(tm, D_I),
                   _row_spec(tm, LANES), _row_spec(tm, H_I)],
        out_shape=[sds(2 * LANES, F32), sds(2 * LANES, BF16), sds(D_I, F32), sds(LANES, BF16),
                   sds(H_I, F32)],
        compiler_params=cp)(x_bf, wg['dsak'], *tabs)
    u = pl.pallas_call(
        _proj_glu_body, grid=grid, name="proj_glu",
        in_specs=[xs, _const_spec(wg['glu'].shape)],
        out_specs=_row_spec(tm, D_C), out_shape=sds(D_C, F32),
        compiler_params=cp)(x_bf, wg['glu'])
    return dict(q_pad=q_pad, k_bf=k_bf, kva=kva, logf=logf, sg=sg, qb_pad=qb_pad,
                qi_pad=qi_pad, kvb=kvb, kvb_bf=kvb_bf, ki=ki, ki_pad=ki_pad, wi=wi, u=u)


def _project_vt(x_bf, w_vt, b, s, tm):
    nt = s // tm
    return pl.pallas_call(
        _proj_vt_body, grid=(b, nt), name="proj_vt",
        in_specs=[pl.BlockSpec((tm, D_MODEL), lambda i, j: (i * nt + j, 0)),
                  pl.BlockSpec(w_vt.shape, lambda i, j: (0, 0), pipeline_mode=pl.Buffered(1))],
        out_specs=[pl.BlockSpec((1, LANES, tm), lambda i, j: (i, 0, j)),
                   pl.BlockSpec((1, W_A, tm), lambda i, j: (i, 0, j))],
        out_shape=[jax.ShapeDtypeStruct((b, LANES, s), BF16),
                   jax.ShapeDtypeStruct((b, W_A, s), BF16)],
        compiler_params=_cparams(("parallel", "parallel")))(x_bf, w_vt)


def _cumsum_body(x_ref, o_ref, carry_ref):
    @pl.when(pl.program_id(1) == 0)
    def _():
        carry_ref[...] = jnp.zeros_like(carry_ref)

    t = x_ref.shape[1]
    r = lax.broadcasted_iota(I32, (t, t), 0)
    c = lax.broadcasted_iota(I32, (t, t), 1)
    tri = jnp.where(c <= r, 1.0, 0.0).astype(BF16)
    hi, mid, lo = _split3(x_ref[0])
    cum = _dot(tri, hi) + _dot(tri, mid) + _dot(tri, lo) + carry_ref[...]
    o_ref[0] = cum
    carry_ref[...] = cum[t - 1:t, :]


def _cumsum(logf, tb):
    b, s, h = logf.shape
    return pl.pallas_call(
        _cumsum_body, grid=(b, s // tb), name="cumsum",
        in_specs=[pl.BlockSpec((1, tb, h), lambda i, j: (i, j, 0))],
        out_specs=pl.BlockSpec((1, tb, h), lambda i, j: (i, j, 0)),
        out_shape=jax.ShapeDtypeStruct((b, s, h), F32),
        scratch_shapes=[pltpu.VMEM((1, h), F32)],
        compiler_params=_cparams(("parallel", "arbitrary")))(logf)


def _fox_body(q_ref, k_ref, vt_ref, cq_ref, ck_ref, o_ref, ckb_ref, acc_ref, m_ref, l_ref, ot_ref,
              s_ref, p_ref, *, tq):
    i = pl.program_id(1)

    @pl.when(i == 0)
    def _():
        ck = ck_ref[0]
        for h in range(H_A):
            ckb_ref[h] = jnp.broadcast_to(ck[:, h:h + 1], (ck.shape[0], LANES))

    acc_ref[...] = jnp.zeros_like(acc_ref)
    l_ref[...] = jnp.zeros_like(l_ref)
    m_ref[...] = jnp.full(m_ref.shape, NEG, F32)
    cq = cq_ref[0]
    ones = jnp.ones((16, tq), BF16)

    def step(off, diag):
        if diag:
            krow = lax.broadcasted_iota(I32, (tq, tq), 0)
            qcol = lax.broadcasted_iota(I32, (tq, tq), 1)
            causal = krow <= qcol

        def qk(h):
            grp = slice((h // 2) * LANES, (h // 2 + 1) * LANES)
            k = k_ref[0, pl.ds(off, tq), grp]
            return _dot_nt(k, q_ref[:, h * LANES:(h + 1) * LANES])

        def pv_update(h, alpha):
            vt = jnp.concatenate([vt_ref[0, h * DH:(h + 1) * DH, pl.ds(off, tq)], ones], axis=0)
            pv = _dot(vt, p_ref[h % PIPE_SLOTS])
            acc_ref[h] = alpha * acc_ref[h] + pv[:DH, :]
            l_ref[h] = alpha * l_ref[h] + pv[DH:DH + 1, :]

        ahead, lag, slots = PIPE_AHEAD, PIPE_LAG, PIPE_SLOTS
        for h in range(ahead):
            s_ref[h % slots] = qk(h)
        pending = []
        for h in range(H_A):
            if h + ahead < H_A:
                s_ref[(h + ahead) % slots] = qk(h + ahead)
            ckb = ckb_ref[h, pl.ds(off, tq), :]
            s = s_ref[h % slots] + cq[h:h + 1, :] - _tile_lanes(ckb, tq // LANES)
            if diag:
                s = jnp.where(causal, s, NEG)
            m_old = m_ref[h]
            m_new = jnp.maximum(m_old, jnp.max(s, axis=0, keepdims=True))
            p_ref[h % slots] = jnp.exp(s - m_new).astype(BF16)
            m_ref[h] = m_new
            pending.append((h, jnp.exp(m_old - m_new)))
            if len(pending) > lag:
                pv_update(*pending.pop(0))
        for item in pending:
            pv_update(*item)

    def body(c, carry):
        step(pl.multiple_of(c * tq, tq), False)
        return carry

    lax.fori_loop(0, i, body, 0)
    step(pl.multiple_of(i * tq, tq), True)
    for h in range(H_A):
        ot_ref[h * DH:(h + 1) * DH, :] = acc_ref[h] / l_ref[h]
    o_ref[...] = ot_ref[...].T


def _fox_prompt(q_pad, k_bf, vt, cum, cum_t, b, s, tq):
    nq = s // tq
    k3 = k_bf.reshape(b, s, W_A)
    return pl.pallas_call(
        functools.partial(_fox_body, tq=tq), grid=(b, nq), name="fox_prompt",
        in_specs=[pl.BlockSpec((tq, PADW), lambda i, j: (i * nq + j, 0)),
                  pl.BlockSpec((1, s, W_A), lambda i, j: (i, 0, 0)),
                  pl.BlockSpec((1, W_A, s), lambda i, j: (i, 0, 0)),
                  pl.BlockSpec((1, H_A, tq), lambda i, j: (i, 0, j)),
                  pl.BlockSpec((1, s, H_A), lambda i, j: (i, 0, 0))],
        out_specs=pl.BlockSpec((tq, W_A), lambda i, j: (i * nq + j, 0)),
        out_shape=jax.ShapeDtypeStruct((b * s, W_A), F32),
        scratch_shapes=[pltpu.VMEM((H_A, s, LANES), F32), pltpu.VMEM((H_A, DH, tq), F32),
                        pltpu.VMEM((H_A, 1, tq), F32), pltpu.VMEM((H_A, 1, tq), F32),
                        pltpu.VMEM((W_A, tq), F32), pltpu.VMEM((PIPE_SLOTS, tq, tq), F32),
                        pltpu.VMEM((PIPE_SLOTS, tq, tq), BF16)],
        compiler_params=_cparams(("parallel", "arbitrary")))(q_pad, k3, vt, cum_t, cum)


def _key_to_f32(key):
    bits = key ^ ((key >> 31) & 0x7FFFFFFF)
    return lax.bitcast_convert_type(bits, F32)


def _kth_largest(count_ge, shape, n_top):
    kf = jnp.float32(n_top)
    zero = jnp.zeros(shape, I32)
    t0 = jnp.where(count_ge(_key_to_f32(zero)) >= kf, zero, jnp.full(shape, INT_MIN, I32))

    def bit_step(it, t):
        cand = t + jnp.left_shift(jnp.int32(1), 30 - it)
        return jnp.where(count_ge(_key_to_f32(cand)) >= kf, cand, t)

    key = lax.fori_loop(0, 31, bit_step, t0)
    few = key == INT_MIN
    return jnp.where(few, -jnp.inf, _key_to_f32(key)), few


def _tie_cut(count_eq_below, shape, need, n_bits):
    def bit_step(it, j):
        cand = j + jnp.left_shift(jnp.int32(1), n_bits - 1 - it)
        return jnp.where(count_eq_below(cand) < need, cand, j)

    return lax.fori_loop(0, n_bits, bit_step, jnp.zeros(shape, I32))


def _dsa_body(qb_ref, qi_ref, wi_ref, ki_ref, k_ref, vt_ref, o_ref,
              sc_ref, bias_ref, acc_ref, m_ref, l_ref, ot_ref, s_ref, p_ref, *, tq, s_len, n_top):
    i = pl.program_id(1)
    nk = i + 1
    kidx0 = lax.broadcasted_iota(I32, (tq, tq), 0)
    qidx = i * tq + lax.broadcasted_iota(I32, (tq, tq), 1)
    wi = wi_ref[0]

    def score_chunk(c, _):
        off = pl.multiple_of(c * tq, tq)
        ki = ki_ref[0, pl.ds(off, tq), :]
        sc = jnp.zeros((tq, tq), F32)
        for h in range(H_I):
            r = _dot_nt(ki, qi_ref[:, h * LANES:(h + 1) * LANES])
            sc = sc + wi[h:h + 1, :] * jnp.maximum(r, 0.0)
        sc_ref[pl.ds(off, tq), :] = jnp.where(kidx0 + off <= qidx, sc, -jnp.inf)
        return 0

    lax.fori_loop(0, nk, score_chunk, 0)

    def count(pred):
        def body(c, cnt):
            off = pl.multiple_of(c * tq, tq)
            hit = jnp.where(pred(sc_ref[pl.ds(off, tq), :], off), 1.0, 0.0)
            parts = [hit[r * 8:(r + 1) * 8, :] for r in range(tq // 8)]
            while len(parts) > 1:
                parts = [parts[a] + parts[a + 1] for a in range(0, len(parts), 2)]
            return cnt + parts[0]
        cnt = lax.fori_loop(0, nk, body, jnp.zeros((8, tq), F32))
        return jnp.sum(cnt, axis=0, keepdims=True)

    thr, few = _kth_largest(lambda cand: count(lambda x, off: x >= cand), (1, tq), n_top)
    need = jnp.float32(n_top) - count(lambda x, off: x > thr)
    c_eq = count(lambda x, off: x == thr)
    has_tie = jnp.max(jnp.where((c_eq > need) & jnp.logical_not(few), 1.0, 0.0)) > 0.0

    def tie_path():
        n_bits = int(np.log2(s_len))
        return _tie_cut(lambda cand: count(lambda x, off: (x == thr) & (kidx0 + off < cand)),
                        (1, tq), need, n_bits)

    cut = lax.cond(has_tie, tie_path, lambda: jnp.full((1, tq), s_len, I32))

    def bias_chunk(c, _):
        off = pl.multiple_of(c * tq, tq)
        x = sc_ref[pl.ds(off, tq), :]
        kidx = kidx0 + off
        sel = ((x > thr) | ((x == thr) & (kidx <= cut))) & (kidx <= qidx)
        bias_ref[pl.ds(off, tq), :] = jnp.where(sel, 0.0, NEG)
        return 0

    lax.fori_loop(0, nk, bias_chunk, 0)

    acc_ref[...] = jnp.zeros_like(acc_ref)
    l_ref[...] = jnp.zeros_like(l_ref)
    m_ref[...] = jnp.full(m_ref.shape, NEG, F32)
    ones = jnp.ones((16, tq), BF16)

    def attend(c, _):
        off = pl.multiple_of(c * tq, tq)
        k = k_ref[0, pl.ds(off, tq), :]
        vt = jnp.concatenate([vt_ref[0, :, pl.ds(off, tq)], ones], axis=0)
        bias = bias_ref[pl.ds(off, tq), :]
        def qk(h):
            return _dot_nt(k, qb_ref[:, h * LANES:(h + 1) * LANES])

        def pv_update(h, alpha):
            n = h // G_B
            pv = _dot(vt, p_ref[h % PIPE_SLOTS])
            acc_ref[h] = alpha * acc_ref[h] + pv[n * DH:(n + 1) * DH, :]
            l_ref[h] = alpha * l_ref[h] + pv[LANES:LANES + 1, :]

        ahead, lag, slots = PIPE_AHEAD, PIPE_LAG, PIPE_SLOTS
        for h in range(ahead):
            s_ref[h % slots] = qk(h)
        pending = []
        for h in range(H_B):
            if h + ahead < H_B:
                s_ref[(h + ahead) % slots] = qk(h + ahead)
            s = s_ref[h % slots] + bias
            m_old = m_ref[h]
            m_new = jnp.maximum(m_old, jnp.max(s, axis=0, keepdims=True))
            alpha = jnp.exp(m_old - m_new)
            p_ref[h % slots] = jnp.exp(s - m_new).astype(BF16)
            m_ref[h] = m_new
            pending.append((h, alpha))
            if len(pending) > lag:
                pv_update(*pending.pop(0))
        for item in pending:
            pv_update(*item)
        return 0

    lax.fori_loop(0, nk, attend, 0)
    for h in range(H_B):
        ot_ref[h * DH:(h + 1) * DH, :] = acc_ref[h] / l_ref[h]
    o_ref[...] = ot_ref[...].T


def _dsa_prompt(qb_pad, qi_pad, wi_t, ki_pad, kvb_bf, vt, b, s, tq):
    nq = s // tq
    n_top = min(TOPK_MAX, s // 4)
    ki3 = ki_pad.reshape(b, s, LANES)
    kv3 = kvb_bf.reshape(b, s, 2 * LANES)
    rows = lambda i, j: (i * nq + j, 0)
    return pl.pallas_call(
        functools.partial(_dsa_body, tq=tq, s_len=s, n_top=n_top), grid=(b, nq), name="dsa_prompt",
        in_specs=[pl.BlockSpec((tq, PADW), rows), pl.BlockSpec((tq, PADW), rows),
                  pl.BlockSpec((1, H_I, tq), lambda i, j: (i, 0, j)),
                  pl.BlockSpec((1, s, LANES), lambda i, j: (i, 0, 0)),
                  pl.BlockSpec((1, s, LANES), lambda i, j: (i, 0, 0)),
                  pl.BlockSpec((1, LANES, s), lambda i, j: (i, 0, 0))],
        out_specs=pl.BlockSpec((tq, W_B), rows),
        out_shape=jax.ShapeDtypeStruct((b * s, W_B), F32),
        scratch_shapes=[pltpu.VMEM((s, tq), F32), pltpu.VMEM((s, tq), F32),
                        pltpu.VMEM((H_B, DH, tq), F32), pltpu.VMEM((H_B, 1, tq), F32),
                        pltpu.VMEM((H_B, 1, tq), F32), pltpu.VMEM((W_B, tq), F32),
                        pltpu.VMEM((PIPE_SLOTS, tq, tq), F32),
                        pltpu.VMEM((PIPE_SLOTS, tq, tq), BF16)],
        compiler_params=_cparams(("parallel", "arbitrary")))(qb_pad, qi_pad, wi_t, ki3, kv3, vt)


HALO = 32


def _conv_ln_silu(acc, g, b):
    mu = jnp.mean(acc, axis=1, keepdims=True)
    d = acc - mu
    var = jnp.mean(d * d, axis=1, keepdims=True)
    y = d * lax.rsqrt(var + LN_EPS) * g + b
    return y * _sigmoid(y)


def _conv_body(u_ref, halo_ref, w_ref, cb_ref, g_ref, b_ref, o_ref, ext_ref, sh_ref, *, tq):
    j = pl.program_id(1)
    ext_ref[0:HALO, :] = jnp.where(j > 0, halo_ref[...], 0.0)
    ext_ref[HALO:, :] = u_ref[...]
    base = HALO - (CONV_W - 1)
    for r in range(1, 8):
        sh_ref[r - 1] = ext_ref[r:r + tq + HALO - 8, :]
    accs = [jnp.zeros((tq, D_C), F32) + cb_ref[...], jnp.zeros((tq, D_C), F32)]
    for t in range(CONV_W):
        r, a = (base + t) % 8, (base + t) // 8
        if r == 0:
            x = ext_ref[8 * a:8 * a + tq, :]
        else:
            x = sh_ref[r - 1, 8 * a:8 * a + tq, :]
        accs[t % 2] = accs[t % 2] + x * w_ref[t:t + 1, :]
    o_ref[...] = _conv_ln_silu(accs[0] + accs[1], g_ref[...], b_ref[...])


def _conv_prompt(u, conv_w, conv_b, g, bb, b, s, tq):
    nq = s // tq
    hb = tq // HALO
    rows = lambda i, j: (i * nq + j, 0)
    return pl.pallas_call(
        functools.partial(_conv_body, tq=tq), grid=(b, nq), name="conv_prompt",
        in_specs=[pl.BlockSpec((tq, D_C), rows),
                  pl.BlockSpec((HALO, D_C), lambda i, j: (jnp.maximum((i * nq + j) * hb - 1, 0), 0)),
                  pl.BlockSpec((CONV_W, D_C), lambda i, j: (0, 0)),
                  pl.BlockSpec((1, D_C), lambda i, j: (0, 0)),
                  pl.BlockSpec((1, D_C), lambda i, j: (0, 0)),
                  pl.BlockSpec((1, D_C), lambda i, j: (0, 0))],
        out_specs=pl.BlockSpec((tq, D_C), rows),
        out_shape=jax.ShapeDtypeStruct((b * s, D_C), F32),
        scratch_shapes=[pltpu.VMEM((HALO + tq, D_C), F32),
                        pltpu.VMEM((7, tq + HALO - 8, D_C), F32)],
        compiler_params=_cparams(("parallel", "arbitrary")))(u, u, conv_w, conv_b, g, bb)


def _merge_body(x_ref, p_ref, ya_ref, yb_ref, yc_ref, sg_ref, wmg_ref, wa_ref, wb_ref, wc_ref,
                wo_ref, wpe_ref, wpg_ref, g_ref, b_ref, y_ref, ybf_ref, *, alpha):
    x = x_ref[...]
    gates = _sigmoid(_dot(x.astype(BF16), wmg_ref[...]))
    sg = sg_ref[...]
    ua = _dot((ya_ref[...] * sg[:, :W_A]).astype(BF16), wa_ref[...])
    ub = _dot((yb_ref[...] * sg[:, W_A:2 * W_A]).astype(BF16), wb_ref[...])
    uc = _dot((yc_ref[...] * sg[:, 2 * W_A:]).astype(BF16), wc_ref[...])
    m = (gates[:, :D_MODEL] * ua + gates[:, D_MODEL:2 * D_MODEL] * ub
         + gates[:, 2 * D_MODEL:] * uc)
    r = alpha * x + _dot(m.astype(BF16), wo_ref[...])
    mu = jnp.mean(r, axis=1, keepdims=True)
    d = r - mu
    var = jnp.mean(d * d, axis=1, keepdims=True)
    h = d * lax.rsqrt(var + LN_EPS) * g_ref[...] + b_ref[...]
    pe = _dot(p_ref[...].astype(BF16), wpe_ref[...])
    y = h + _sigmoid(_dot(h.astype(BF16), wpg_ref[...])) * pe
    y_ref[...] = y
    ybf_ref[...] = y.astype(BF16)


def _merge(x, p, ya, yb, yc, sg, wl, tm, alpha):
    m = x.shape[0]
    ws = [wl['w_mg'], wl['w_a'], wl['w_b'], wl['w_c'], wl['w_out'], wl['w_pe'], wl['w_pe_gate'],
          wl['ln_g'], wl['ln_b']]
    return pl.pallas_call(
        functools.partial(_merge_body, alpha=alpha), grid=(m // tm,), name="merge",
        in_specs=[_row_spec(tm, D_MODEL), _row_spec(tm, P_DIM), _row_spec(tm, W_A),
                  _row_spec(tm, W_B), _row_spec(tm, D_C), _row_spec(tm, 3 * W_A)]
                 + [_const_spec(w.shape) for w in ws],
        out_specs=[_row_spec(tm, D_MODEL), _row_spec(tm, D_MODEL)],
        out_shape=[jax.ShapeDtypeStruct((m, D_MODEL), F32), jax.ShapeDtypeStruct((m, D_MODEL), BF16)],
        compiler_params=_cparams(("parallel",)))(x, p, ya, yb, yc, sg, *ws)


def _fox_dec_body(pt_ref, q_ref, kvn_ref, lfn_ref, *refs, g, layer):
    kv_refs = refs[:g]
    lf_refs = refs[g:2 * g]
    o_ref = refs[2 * g]
    m_ref, l_ref, acc_ref, carry_ref = refs[2 * g + 1:]
    t = pl.program_id(1)
    qrows = q_ref[0]

    @pl.when(t == 0)
    def _():
        kn = kvn_ref[0][:, :W_A].astype(BF16).astype(F32)
        vn = kvn_ref[0][:, W_A:]
        s_new = jnp.sum(qrows.astype(F32) * kn, axis=1, keepdims=True)
        m_ref[...] = s_new
        l_ref[...] = jnp.ones_like(l_ref)
        acc_ref[...] = jnp.broadcast_to(vn.astype(BF16).astype(F32), (H_A, W_A))
        carry_ref[...] = lfn_ref[0]

    r = lax.broadcasted_iota(I32, (PAGE, PAGE), 0)
    c = lax.broadcasted_iota(I32, (PAGE, PAGE), 1)
    upper = jnp.where(r > c, 1.0, 0.0).astype(BF16)
    lf = jnp.concatenate([lf_refs[gi][0, 0] for gi in range(g)], axis=0)
    suf = _dot3(lf, upper)
    tot = jnp.sum(lf, axis=1, keepdims=True)
    carry = carry_ref[...]
    parts = []
    for gi in range(g):
        kt = kv_refs[gi][0, 0, :W_A, :].astype(BF16)
        bias = suf[gi * H_A:(gi + 1) * H_A, :] + carry
        carry = carry + tot[gi * H_A:(gi + 1) * H_A, :]
        parts.append(_dot(qrows, kt) + bias)
    carry_ref[...] = carry
    s = jnp.concatenate(parts, axis=1)
    m_old = m_ref[...]
    m_new = jnp.maximum(m_old, jnp.max(s, axis=1, keepdims=True))
    alpha = jnp.exp(m_old - m_new)
    p = jnp.exp(s - m_new)
    l_ref[...] = alpha * l_ref[...] + jnp.sum(p, axis=1, keepdims=True)
    pb = p.astype(BF16)
    acc = alpha * acc_ref[...]
    for gi in range(g):
        vt = kv_refs[gi][0, 0, W_A:, :].astype(BF16)
        acc = acc + _dot_nt(pb[:, gi * PAGE:(gi + 1) * PAGE], vt)
    acc_ref[...] = acc
    m_ref[...] = m_new

    @pl.when(t == pl.num_programs(1) - 1)
    def _():
        o_ref[0] = acc_ref[...] / l_ref[...]


def _fox_decode(qrows, kva, logf, page_table, kv_t, lf_t, g, layer):
    bd, n_pages = page_table.shape
    steps = n_pages // g

    def page_map(gi):
        return lambda b, t, pt: (layer, pt[b, n_pages - 1 - (t * g + gi)], 0, 0)

    per_b = lambda b, t, pt: (b, 0, 0)
    out = pl.pallas_call(
        functools.partial(_fox_dec_body, g=g, layer=layer), name="fox_decode",
        grid_spec=pltpu.PrefetchScalarGridSpec(
            num_scalar_prefetch=1, grid=(bd, steps),
            in_specs=[pl.BlockSpec((1, H_A, W_A), per_b), pl.BlockSpec((1, 1, 2 * W_A), per_b),
                      pl.BlockSpec((1, H_A, 1), per_b)]
                     + [pl.BlockSpec((1, 1, 2 * W_A, PAGE), page_map(gi)) for gi in range(g)]
                     + [pl.BlockSpec((1, 1, H_A, PAGE), page_map(gi)) for gi in range(g)],
            out_specs=pl.BlockSpec((1, H_A, W_A), per_b),
            scratch_shapes=[pltpu.VMEM((H_A, 1), F32), pltpu.VMEM((H_A, 1), F32),
                            pltpu.VMEM((H_A, W_A), F32), pltpu.VMEM((H_A, 1), F32)]),
        out_shape=jax.ShapeDtypeStruct((bd, H_A, W_A), F32),
        compiler_params=_cparams(("parallel", "arbitrary")))(
            page_table, qrows, kva.reshape(bd, 1, 2 * W_A), logf.reshape(bd, H_A, 1),
            *([kv_t] * g), *([lf_t] * g))
    o4 = out.reshape(bd, H_A, H_A, DH)
    return jnp.stack([o4[:, h, h] for h in range(H_A)], axis=1).reshape(bd, W_A)


def _idx_dec_body(pt_ref, qi_ref, wi_ref, kin_ref, *refs, g):
    ki_refs = refs[:g]
    o_ref, on_ref = refs[g:]
    qi = qi_ref[0]
    w = wi_ref[0]
    rows = []
    for gi in range(g):
        r = _dot(qi, ki_refs[gi][0, 0].astype(BF16))
        rows.append(jnp.sum(w * jnp.maximum(r, 0.0), axis=0, keepdims=True))
    o_ref[0] = jnp.concatenate(rows, axis=0)
    r_new = jnp.sum(qi.astype(F32) * kin_ref[0].astype(F32), axis=1, keepdims=True)
    sc_new = jnp.sum(w * jnp.maximum(r_new, 0.0), axis=0, keepdims=True)
    on_ref[0] = jnp.broadcast_to(sc_new, (1, LANES))


def _idx_decode(qi_bf, wi, ki_new_bf, page_table, ik_t, g, layer):
    bd, n_pages = page_table.shape
    steps = n_pages // g

    def page_map(gi):
        return lambda b, t, pt: (layer, pt[b, t * g + gi], 0, 0)

    per_b = lambda b, t, pt: (b, 0, 0)
    sc, sc_new = pl.pallas_call(
        functools.partial(_idx_dec_body, g=g), name="idx_decode",
        grid_spec=pltpu.PrefetchScalarGridSpec(
            num_scalar_prefetch=1, grid=(bd, steps),
            in_specs=[pl.BlockSpec((1, H_I, D_I), per_b), pl.BlockSpec((1, H_I, 1), per_b),
                      pl.BlockSpec((1, 1, D_I), per_b)]
                     + [pl.BlockSpec((1, 1, D_I, PAGE), page_map(gi)) for gi in range(g)],
            out_specs=[pl.BlockSpec((1, g, PAGE), lambda b, t, pt: (b, t, 0)),
                       pl.BlockSpec((1, 1, LANES), per_b)]),
        out_shape=[jax.ShapeDtypeStruct((bd, n_pages, PAGE), F32),
                   jax.ShapeDtypeStruct((bd, 1, LANES), F32)],
        compiler_params=_cparams(("parallel", "arbitrary")))(
            page_table, qi_bf, wi.reshape(bd, H_I, 1), ki_new_bf.reshape(bd, 1, D_I),
            *([ik_t] * g))
    return sc.reshape(bd, n_pages * PAGE), sc_new[:, 0, :1]


def _sel_dec_body(sc_ref, scn_ref, bias_ref, biasn_ref, *, n_top, n_bits):
    bd, p_len = sc_ref.shape
    sc = sc_ref[...]
    sc_n = scn_ref[...]
    idx = lax.broadcasted_iota(I32, (bd, p_len), 1)

    def count(pred, pred_n):
        return (jnp.sum(jnp.where(pred, 1.0, 0.0), axis=1, keepdims=True)
                + jnp.where(pred_n, 1.0, 0.0))

    thr, _ = _kth_largest(lambda cand: count(sc >= cand, sc_n >= cand), (bd, 1), n_top)
    need = jnp.float32(n_top) - count(sc > thr, sc_n > thr)
    cut = _tie_cut(lambda cand: jnp.sum(jnp.where((sc == thr) & (idx < cand), 1.0, 0.0),
                                        axis=1, keepdims=True), (bd, 1), need, n_bits)
    eq_past = jnp.sum(jnp.where(sc == thr, 1.0, 0.0), axis=1, keepdims=True)
    sel = (sc > thr) | ((sc == thr) & (idx <= cut))
    bias_ref[...] = jnp.where(sel, 0.0, NEG)
    sel_n = (sc_n > thr) | ((sc_n == thr) & (eq_past < need))
    biasn_ref[...] = jnp.where(sel_n, 0.0, NEG)


def _sel_decode(sc, sc_new, n_top):
    bd, p_len = sc.shape
    n_bits = int(np.log2(p_len))
    return pl.pallas_call(
        functools.partial(_sel_dec_body, n_top=n_top, n_bits=n_bits), name="sel_decode",
        out_shape=[jax.ShapeDtypeStruct((bd, p_len), F32), jax.ShapeDtypeStruct((bd, 1), F32)],
        compiler_params=pltpu.CompilerParams(vmem_limit_bytes=VMEM_LIMIT))(sc, sc_new)


def _dsa_dec_body(pt_ref, q_ref, kvn_ref, bn_ref, bias_ref, *refs, g):
    kv_refs = refs[:g]
    o_ref = refs[g]
    m_ref, l_ref, acc_ref = refs[g + 1:]
    t = pl.program_id(1)
    qrows = q_ref[0]

    @pl.when(t == 0)
    def _():
        kn = kvn_ref[0][:, :LANES].astype(BF16).astype(F32)
        vn = kvn_ref[0][:, LANES:].astype(BF16).astype(F32)
        s_new = jnp.sum(qrows.astype(F32) * kn, axis=1, keepdims=True) + bn_ref[0]
        m_ref[...] = s_new
        l_ref[...] = jnp.where(s_new > 0.5 * NEG, 1.0, 0.0)
        acc_ref[...] = jnp.where(s_new > 0.5 * NEG, jnp.broadcast_to(vn, (H_B, LANES)), 0.0)

    parts = []
    for gi in range(g):
        kt = kv_refs[gi][0, 0, :LANES, :].astype(BF16)
        parts.append(_dot(qrows, kt) + bias_ref[0, gi:gi + 1, :])
    s = jnp.concatenate(parts, axis=1)
    m_old = m_ref[...]
    m_new = jnp.maximum(m_old, jnp.max(s, axis=1, keepdims=True))
    alpha = jnp.exp(m_old - m_new)
    p = jnp.where(s > 0.5 * NEG, jnp.exp(s - m_new), 0.0)
    l_ref[...] = alpha * l_ref[...] + jnp.sum(p, axis=1, keepdims=True)
    pb = p.astype(BF16)
    acc = alpha * acc_ref[...]
    for gi in range(g):
        vt = kv_refs[gi][0, 0, LANES:, :].astype(BF16)
        acc = acc + _dot_nt(pb[:, gi * PAGE:(gi + 1) * PAGE], vt)
    acc_ref[...] = acc
    m_ref[...] = m_new

    @pl.when(t == pl.num_programs(1) - 1)
    def _():
        o_ref[0] = acc_ref[...] / l_ref[...]


def _dsa_decode(q2, kvb, bias, bias_new, page_table, kv_t, g, layer):
    bd, n_pages = page_table.shape
    steps = n_pages // g

    def page_map(gi):
        return lambda b, t, pt: (layer, pt[b, t * g + gi], 0, 0)

    per_b = lambda b, t, pt: (b, 0, 0)
    out = pl.pallas_call(
        functools.partial(_dsa_dec_body, g=g), name="dsa_decode",
        grid_spec=pltpu.PrefetchScalarGridSpec(
            num_scalar_prefetch=1, grid=(bd, steps),
            in_specs=[pl.BlockSpec((1, H_B, LANES), per_b), pl.BlockSpec((1, 1, 2 * LANES), per_b),
                      pl.BlockSpec((1, 1, 1), per_b),
                      pl.BlockSpec((1, g, PAGE), lambda b, t, pt: (b, t, 0))]
                     + [pl.BlockSpec((1, 1, 2 * LANES, PAGE), page_map(gi)) for gi in range(g)],
            out_specs=pl.BlockSpec((1, H_B, LANES), per_b),
            scratch_shapes=[pltpu.VMEM((H_B, 1), F32), pltpu.VMEM((H_B, 1), F32),
                            pltpu.VMEM((H_B, LANES), F32)]),
        out_shape=jax.ShapeDtypeStruct((bd, H_B, LANES), F32),
        compiler_params=_cparams(("parallel", "arbitrary")))(
            page_table, q2, kvb.reshape(bd, 1, 2 * LANES), bias_new.reshape(bd, 1, 1),
            bias.reshape(bd, n_pages, PAGE), *([kv_t] * g))
    o4 = out.reshape(bd, KV_B, G_B, KV_B, DH)
    return jnp.stack([o4[:, n, :, n] for n in range(KV_B)], axis=1).reshape(bd, W_B)


def _conv_dec_body(st_ref, u_ref, w_ref, cb_ref, g_ref, b_ref, o_ref):
    acc = u_ref[...] * w_ref[CONV_W - 1:CONV_W, :] + cb_ref[...]
    for t in range(CONV_W - 1):
        acc = acc + st_ref[0, t] * w_ref[t:t + 1, :]
    o_ref[...] = _conv_ln_silu(acc, g_ref[...], b_ref[...])


def _conv_decode(state_t, u, conv_w, conv_b, g, bb, layer):
    bd = u.shape[0]
    full = lambda a: pl.BlockSpec(a.shape, lambda i: (0,) * a.ndim)
    return pl.pallas_call(
        _conv_dec_body, grid=(1,), name="conv_decode",
        in_specs=[pl.BlockSpec((1,) + state_t.shape[1:], lambda i: (layer, 0, 0, 0)),
                  full(u), full(conv_w), full(conv_b), full(g), full(bb)],
        out_specs=pl.BlockSpec((bd, D_C), lambda i: (0, 0)),
        out_shape=jax.ShapeDtypeStruct((bd, D_C), F32),
        compiler_params=_cparams(("arbitrary",)))(state_t, u, conv_w, conv_b, g, bb)


def _rope_tables(pos):
    half = ROT_DIM // 2
    inv = ROPE_THETA ** (-jnp.arange(half, dtype=F32) / half)
    ang = pos.astype(F32)[:, None] * inv[None, :]
    cos, sin = jnp.cos(ang), jnp.sin(ang)
    n = pos.shape[0]
    pad = jnp.zeros((n, DH - ROT_DIM), F32)
    c = jnp.concatenate([cos, cos, pad + 1.0], axis=1)
    s1 = jnp.concatenate([jnp.zeros_like(sin), sin, pad], axis=1)
    s2 = jnp.concatenate([-sin, jnp.zeros_like(sin), pad], axis=1)
    return tuple(jnp.tile(t, (1, LANES // DH)) for t in (c, s1, s2))


def _pad_heads(w, half_of_head):
    d = w.shape[0]
    z = jnp.zeros((d, DH), w.dtype)
    cols = []
    for h in range(w.shape[1] // DH):
        wh = w[:, h * DH:(h + 1) * DH]
        cols += [wh, z] if half_of_head(h) == 0 else [z, wh]
    return jnp.concatenate(cols, axis=1)


def _regroup_weights(w_in, b_f):
    w = w_in.astype(BF16)
    d = w.shape[0]
    z = lambda n: jnp.zeros((d, n), BF16)
    foxq = jnp.concatenate([_pad_heads(w[:, O_QA:O_KA], lambda h: h % 2), w[:, O_FA:O_GA],
                            z(LANES - H_A)], axis=1)
    foxkv = w[:, O_KA:O_FA]
    gates = jnp.concatenate([w[:, O_GA:O_QB], w[:, O_GB:O_GLU], w[:, O_GC:O_MG]], axis=1)
    dsaq = jnp.concatenate([_pad_heads(w[:, O_QB:O_KB], lambda h: h // G_B),
                            _pad_heads(w[:, O_QI:O_KI], lambda h: 0)], axis=1)
    dsak = jnp.concatenate([w[:, O_KB:O_QI], w[:, O_KI:O_GB], z(LANES - D_I - H_I)], axis=1)
    glu = w[:, O_GLU:O_GC]
    vt = jnp.transpose(jnp.concatenate([w[:, O_VB:O_QI], w[:, O_VA:O_FA]], axis=1))
    bf = jnp.concatenate([b_f.astype(F32), jnp.zeros((LANES - H_A,), F32)])[None, :]
    return dict(foxq=foxq, foxkv=foxkv, gates=gates, dsaq=dsaq, dsak=dsak, glu=glu, vt=vt, b_f=bf,
                w_mg=w[:, O_MG:O_END])


def _pick(n, prefs):
    for t in prefs:
        if n % t == 0:
            return t
    return n


def kernel(x_prompt, x_sample, cache_fox_kv, cache_fox_logf, cache_dsa_kv, cache_dsa_idx_k, state_conv,
           page_table, p_prompt, p_sample, w_in, b_f, w_a, w_b, w_c, w_out, ln_g, ln_b,
           conv_w, conv_b, conv_ln_g, conv_ln_b, w_pe, w_pe_gate):
    depth = w_in.shape[0]
    b, s, d = x_prompt.shape
    bd = x_sample.shape[0]
    n_phys = cache_fox_kv.shape[1]
    n_pages = page_table.shape[1]
    p_len = n_pages * PAGE
    alpha = float((2 * depth) ** 0.25)
    tm = _pick(s, (512, 256, 128))
    tq = _pick(s, (256, 128))
    g_fox = _pick(n_pages, (16, 8, 4, 2, 1))
    g_dsa = _pick(n_pages, (32, 16, 8, 4, 2, 1))
    g_idx = _pick(n_pages, (64, 32, 16, 8, 4, 2, 1))
    n_top_dec = min(TOPK_MAX, (p_len + 1) // 4)

    tabs_p = _rope_tables(jnp.arange(s, dtype=I32))
    tabs_s = _rope_tables(jnp.full((bd,), p_len, I32))
    page_table = page_table.astype(I32)

    fox_kv_t = jnp.transpose(cache_fox_kv, (0, 1, 3, 4, 5, 2)).reshape(depth, n_phys, 2 * W_A, PAGE)
    fox_lf_t = jnp.transpose(cache_fox_logf, (0, 1, 3, 2))
    dsa_kv_t = jnp.transpose(cache_dsa_kv, (0, 1, 3, 4, 5, 2)).reshape(depth, n_phys, 2 * LANES, PAGE)
    dsa_ik_t = jnp.transpose(cache_dsa_idx_k, (0, 1, 3, 2))
    state_t = jnp.transpose(state_conv, (0, 2, 1, 3))
    head_eye = jnp.repeat(jnp.eye(H_A, dtype=BF16), DH, axis=1)

    hp = x_prompt.reshape(b * s, d)
    hs = x_sample.reshape(bd, d)
    hp_bf = hp.astype(BF16)
    hs_bf = hs.astype(BF16)
    outs_p, outs_s = [], []
    for l in range(depth):
        wg = _regroup_weights(w_in[l], b_f[l])
        wl = dict(w_mg=wg['w_mg'], w_a=w_a[l].astype(BF16), w_b=w_b[l].astype(BF16),
                  w_c=w_c[l].astype(BF16), w_out=w_out[l].astype(BF16), w_pe=w_pe[l].astype(BF16),
                  w_pe_gate=w_pe_gate[l].astype(BF16), ln_g=ln_g[l][None, :], ln_b=ln_b[l][None, :])
        cw, cb = conv_w[l], conv_b[l][None, :]
        cg, cbb = conv_ln_g[l][None, :], conv_ln_b[l][None, :]

        pp = _project(hp_bf, wg, tabs_p, tm, s // tm)
        vbt, vat = _project_vt(hp_bf, wg['vt'], b, s, tm)
        logf3 = pp['logf'].reshape(b, s, H_A)
        cum = _cumsum(logf3, tq)
        cum_t = jnp.swapaxes(cum, 1, 2)
        wi_t = jnp.swapaxes(pp['wi'].reshape(b, s, H_I), 1, 2)
        ya = _fox_prompt(pp['q_pad'], pp['k_bf'], vat, cum, cum_t, b, s, tq)
        yb = _dsa_prompt(pp['qb_pad'], pp['qi_pad'], wi_t, pp['ki_pad'], pp['kvb_bf'], vbt, b, s, tq)
        yc = _conv_prompt(pp['u'], cw, cb, cg, cbb, b, s, tq)
        hp_new, hp_bf = _merge(hp, p_prompt[l].reshape(b * s, P_DIM), ya, yb, yc, pp['sg'], wl,
                               _pick(b * s, (256, 128)), alpha)
        outs_p.append((pp['kva'].reshape(b, s, 2, H_A, DH), logf3,
                       pp['kvb'].reshape(b, s, 2, KV_B, DH), pp['ki'].reshape(b, s, D_I),
                       pp['u'].reshape(b, s, D_C)[:, s - (CONV_W - 1):]))
        hp = hp_new

        ps = _project(hs_bf, wg, tabs_s, bd, 1)
        q8 = ps['q_pad'].reshape(bd, H_A, 2, DH)
        q_s = jnp.stack([q8[:, h, h % 2] for h in range(H_A)], axis=1).reshape(bd, 1, W_A)
        qrows = q_s * head_eye[None]
        ya_s = _fox_decode(qrows, ps['kva'], ps['logf'], page_table, fox_kv_t, fox_lf_t, g_fox, l)
        qi_s = ps['qi_pad'].reshape(bd, H_I, LANES)[:, :, :D_I]
        sc, sc_new = _idx_decode(qi_s, ps['wi'], ps['ki_pad'][:, :D_I], page_table, dsa_ik_t,
                                 g_idx, l)
        bias, bias_new = _sel_decode(sc, sc_new, n_top_dec)
        yb_s = _dsa_decode(ps['qb_pad'].reshape(bd, H_B, LANES), ps['kvb'], bias, bias_new,
                           page_table, dsa_kv_t, g_dsa, l)
        yc_s = _conv_decode(state_t, ps['u'], cw, cb, cg, cbb, l)
        hs_new, hs_bf = _merge(hs, p_sample[l].reshape(bd, P_DIM), ya_s, yb_s, yc_s, ps['sg'], wl,
                               bd, alpha)
        outs_s.append((ps['kva'].reshape(bd, 1, 2, H_A, DH), ps['logf'].reshape(bd, 1, H_A),
                       ps['kvb'].reshape(bd, 1, 2, KV_B, DH), ps['ki'].reshape(bd, 1, D_I),
                       jnp.concatenate([state_t[l, 1:], ps['u'][None]], axis=0)))
        hs = hs_new

    stack = lambda outs, k: jnp.stack([o[k] for o in outs])
    return (hp.reshape(b, s, d), hs.reshape(bd, 1, d),
            stack(outs_p, 0), stack(outs_s, 0), stack(outs_p, 1), stack(outs_s, 1),
            stack(outs_p, 2), stack(outs_s, 2), stack(outs_p, 3), stack(outs_s, 3),
            stack(outs_p, 4), jnp.transpose(stack(outs_s, 4), (0, 2, 1, 3)))
```

```python
import functools

import jax
import jax.numpy as jnp
import numpy as np
from jax import lax
from jax.experimental import pallas as pl
from jax.experimental.pallas import tpu as pltpu

F32 = jnp.float32
BF16 = jnp.bfloat16
I32 = jnp.int32

D_MODEL = 1024
H_A = 8
DH = 64
W_A = H_A * DH
KV_B = 2
H_B = 8
G_B = H_B // KV_B
W_B = H_B * DH
H_I = 8
D_I = 64
TOPK_MAX = 256
D_C = 512
CONV_W = 31
P_DIM = 256
N_BRANCH = 3
ROPE_THETA = 500000.0
ROT_DIM = DH // 4
LN_EPS = 1e-5
PAGE = 128
LANES = 128
PADW = H_A * LANES

PIPE_AHEAD = 4
PIPE_LAG = 4
PIPE_SLOTS = 8

NEG = -1e30
INT_MIN = -(2 ** 31)
VMEM_LIMIT = 56 * 1024 * 1024

_SPLIT = (W_A, W_A, W_A, H_A, W_A, W_B, KV_B * DH, KV_B * DH, H_I * D_I, D_I, H_I, W_B,
          2 * D_C, D_C, N_BRANCH * D_MODEL)
_OFF = np.concatenate([[0], np.cumsum(_SPLIT)]).tolist()
(O_QA, O_KA, O_VA, O_FA, O_GA, O_QB, O_KB, O_VB, O_QI, O_KI, O_WI, O_GB, O_GLU, O_GC, O_MG,
 O_END) = _OFF


def _cparams(sem):
    return pltpu.CompilerParams(dimension_semantics=sem, vmem_limit_bytes=VMEM_LIMIT)


def _dot(a, b):
    return jnp.dot(a, b, preferred_element_type=F32)


def _dot_nt(a, b):
    return lax.dot_general(a, b, (((1,), (1,)), ((), ())), preferred_element_type=F32)


def _sigmoid(x):
    return 1.0 / (1.0 + jnp.exp(-x))


def _split3(x):
    hi = x.astype(BF16)
    r1 = x - hi.astype(F32)
    mid = r1.astype(BF16)
    lo = (r1 - mid.astype(F32)).astype(BF16)
    return hi, mid, lo


def _dot3(x, w):
    hi, mid, lo = _split3(x)
    return _dot(hi, w) + _dot(mid, w) + _dot(lo, w)


def _rope(x, c, s1, s2):
    n = x.shape[-1]
    half = ROT_DIM // 2
    return x * c + pltpu.roll(x, half, 1) * s1 + pltpu.roll(x, n - half, 1) * s2


def _tile_lanes(t, reps):
    return t if reps == 1 else jnp.concatenate([t] * reps, axis=1)


def _proj_foxq_body(x_ref, w_ref, bf_ref, q_ref, logf_ref):
    z = _dot(x_ref[...], w_ref[...])
    q_ref[...] = (z[:, :PADW] * (DH ** -0.5)).astype(BF16)
    f = z[:, PADW:] + bf_ref[...]
    logf = jnp.minimum(f, 0.0) - jnp.log1p(jnp.exp(-jnp.abs(f)))
    logf_ref[...] = logf[:, :H_A]


def _proj_foxkv_body(x_ref, w_ref, k_ref, kv_ref):
    z = _dot(x_ref[...], w_ref[...])
    k_ref[...] = z[:, :W_A].astype(BF16)
    kv_ref[...] = z


def _proj_gates_body(x_ref, w_ref, o_ref):
    z = _dot(x_ref[...], w_ref[...])
    o_ref[...] = z * _sigmoid(z)


def _proj_dsaq_body(x_ref, w_ref, c_ref, s1_ref, s2_ref, qb_ref, qi_ref):
    z = _dot(x_ref[...], w_ref[...])
    reps = z.shape[1] // LANES
    y = _rope(z, _tile_lanes(c_ref[...], reps), _tile_lanes(s1_ref[...], reps),
              _tile_lanes(s2_ref[...], reps)) * (DH ** -0.5)
    qb_ref[...] = y[:, :PADW].astype(BF16)
    qi_ref[...] = y[:, PADW:].astype(BF16)


def _proj_dsak_body(x_ref, w_ref, c_ref, s1_ref, s2_ref, kv_ref, kvbf_ref, ki_ref, kibf_ref, wi_ref):
    z = _dot(x_ref[...], w_ref[...])
    c, s1, s2 = c_ref[...], s1_ref[...], s2_ref[...]
    kb = _rope(z[:, :LANES], c, s1, s2)
    kv = jnp.concatenate([kb, z[:, LANES:2 * LANES]], axis=1)
    kv_ref[...] = kv
    kvbf_ref[...] = kv.astype(BF16)
    t = _rope(z[:, 2 * LANES:], c, s1, s2)
    ki_ref[...] = t[:, :D_I]
    lane = lax.broadcasted_iota(I32, t.shape, 1)
    kibf_ref[...] = jnp.where(lane < D_I, t, 0.0).astype(BF16)
    wi_ref[...] = z[:, 2 * LANES + D_I:2 * LANES + D_I + H_I] * (H_I ** -0.5)


def _proj_glu_body(x_ref, w_ref, u_ref):
    z = _dot(x_ref[...], w_ref[...])
    u_ref[...] = z[:, :D_C] * _sigmoid(z[:, D_C:])


def _proj_vt_body(x_ref, w_ref, vb_ref, va_ref):
    zt = _dot_nt(w_ref[...], x_ref[...]).astype(BF16)
    vb_ref[0] = zt[:LANES]
    va_ref[0] = zt[LANES:]


def _row_spec(tm, n):
    return pl.BlockSpec((tm, n), lambda i: (i, 0))


def _const_spec(shape):
    return pl.BlockSpec(shape, lambda i: (0,) * len(shape), pipeline_mode=pl.Buffered(1))


_PROJ_OUTS = ('q_pad', 'logf', 'k_bf', 'kva', 'sg', 'qb_pad', 'qi_pad', 'kvb', 'kvb_bf', 'ki',
              'ki_pad', 'wi', 'u')


def _proj_body(x_ref, wq_ref, wkv_ref, wg_ref, wdq_ref, wdk_ref, wglu_ref, bf_ref,
               c_ref, s1_ref, s2_ref, *refs, with_vt):
    if with_vt:
        wvt_ref, refs = refs[0], refs[1:]
    (q_ref, logf_ref, k_ref, kv_ref, sg_ref, qb_ref, qi_ref, kvb_ref, kvbbf_ref, ki_ref,
     kibf_ref, wi_ref, u_ref) = refs[:len(_PROJ_OUTS)]
    _proj_foxq_body(x_ref, wq_ref, bf_ref, q_ref, logf_ref)
    _proj_foxkv_body(x_ref, wkv_ref, k_ref, kv_ref)
    _proj_gates_body(x_ref, wg_ref, sg_ref)
    _proj_dsaq_body(x_ref, wdq_ref, c_ref, s1_ref, s2_ref, qb_ref, qi_ref)
    _proj_dsak_body(x_ref, wdk_ref, c_ref, s1_ref, s2_ref, kvb_ref, kvbbf_ref, ki_ref, kibf_ref,
                    wi_ref)
    _proj_glu_body(x_ref, wglu_ref, u_ref)
    if with_vt:
        _proj_vt_body(x_ref, wvt_ref, *refs[len(_PROJ_OUTS):])


def _project(x_bf, wg, tabs, tm, tab_blocks, seq=None):
    m = x_bf.shape[0]
    tab_spec = pl.BlockSpec((tm, LANES), lambda i: (i % tab_blocks, 0))
    ws = [wg['foxq'], wg['foxkv'], wg['gates'], wg['dsaq'], wg['dsak'], wg['glu'], wg['b_f']]
    widths = dict(q_pad=(PADW, BF16), logf=(H_A, F32), k_bf=(W_A, BF16), kva=(2 * W_A, F32),
                  sg=(3 * W_A, F32), qb_pad=(PADW, BF16), qi_pad=(PADW, BF16),
                  kvb=(2 * LANES, F32), kvb_bf=(2 * LANES, BF16), ki=(D_I, F32),
                  ki_pad=(LANES, BF16), wi=(H_I, F32), u=(D_C, F32))
    in_specs = ([_row_spec(tm, D_MODEL)] + [_const_spec(w.shape) for w in ws] + [tab_spec] * 3)
    args = [x_bf] + ws + list(tabs)
    out_specs = [_row_spec(tm, widths[k][0]) for k in _PROJ_OUTS]
    out_shape = [jax.ShapeDtypeStruct((m, widths[k][0]), widths[k][1]) for k in _PROJ_OUTS]
    names = list(_PROJ_OUTS)
    if seq is not None:
        b, s = seq
        nt = s // tm
        in_specs.append(_const_spec(wg['vt'].shape))
        args.append(wg['vt'])
        for name, rows in (('vbt', LANES), ('vat', W_A)):
            out_specs.append(pl.BlockSpec((1, rows, tm), lambda i: (i // nt, 0, i % nt)))
            out_shape.append(jax.ShapeDtypeStruct((b, rows, s), BF16))
            names.append(name)
    outs = pl.pallas_call(
        functools.partial(_proj_body, with_vt=seq is not None), grid=(m // tm,), name="project",
        in_specs=in_specs, out_specs=out_specs, out_shape=out_shape,
        compiler_params=_cparams(("parallel",)))(*args)
    return dict(zip(names, outs))


def _cumsum_body(x_ref, o_ref, carry_ref):
    @pl.when(pl.program_id(1) == 0)
    def _():
        carry_ref[...] = jnp.zeros_like(carry_ref)

    t = x_ref.shape[1]
    r = lax.broadcasted_iota(I32, (t, t), 0)
    c = lax.broadcasted_iota(I32, (t, t), 1)
    tri = jnp.where(c <= r, 1.0, 0.0).astype(BF16)
    hi, mid, lo = _split3(x_ref[0])
    cum = _dot(tri, hi) + _dot(tri, mid) + _dot(tri, lo) + carry_ref[...]
    o_ref[0] = cum
    carry_ref[...] = cum[t - 1:t, :]


def _cumsum(logf, tb):
    b, s, h = logf.shape
    return pl.pallas_call(
        _cumsum_body, grid=(b, s // tb), name="cumsum",
        in_specs=[pl.BlockSpec((1, tb, h), lambda i, j: (i, j, 0))],
        out_specs=pl.BlockSpec((1, tb, h), lambda i, j: (i, j, 0)),
        out_shape=jax.ShapeDtypeStruct((b, s, h), F32),
        scratch_shapes=[pltpu.VMEM((1, h), F32)],
        compiler_params=_cparams(("parallel", "arbitrary")))(logf)


def _fox_body(q_ref, k_ref, vt_ref, cq_ref, ck_ref, o_ref, ckb_ref, acc_ref, m_ref, l_ref, ot_ref,
              s_ref, p_ref, *, tq):
    i = pl.program_id(1)

    @pl.when(i == 0)
    def _():
        ck = ck_ref[0]
        for h in range(H_A):
            ckb_ref[h] = jnp.broadcast_to(ck[:, h:h + 1], (ck.shape[0], LANES))

    acc_ref[...] = jnp.zeros_like(acc_ref)
    l_ref[...] = jnp.zeros_like(l_ref)
    m_ref[...] = jnp.full(m_ref.shape, NEG, F32)
    cq = cq_ref[0]
    ones = jnp.ones((16, tq), BF16)

    def step(off, diag):
        if diag:
            krow = lax.broadcasted_iota(I32, (tq, tq), 0)
            qcol = lax.broadcasted_iota(I32, (tq, tq), 1)
            causal = krow <= qcol

        def qk(h):
            grp = slice((h // 2) * LANES, (h // 2 + 1) * LANES)
            k = k_ref[0, pl.ds(off, tq), grp]
            return _dot_nt(k, q_ref[:, h * LANES:(h + 1) * LANES])

        def pv_update(h, alpha):
            vt = jnp.concatenate([vt_ref[0, h * DH:(h + 1) * DH, pl.ds(off, tq)], ones], axis=0)
            pv = _dot(vt, p_ref[h % PIPE_SLOTS])
            acc_ref[h] = alpha * acc_ref[h] + pv[:DH, :]
            l_ref[h] = alpha * l_ref[h] + pv[DH:DH + 1, :]

        ahead, lag, slots = PIPE_AHEAD, PIPE_LAG, PIPE_SLOTS
        for h in range(ahead):
            s_ref[h % slots] = qk(h)
        pending = []
        for h in range(H_A):
            if h + ahead < H_A:
                s_ref[(h + ahead) % slots] = qk(h + ahead)
            ckb = ckb_ref[h, pl.ds(off, tq), :]
            s = s_ref[h % slots] + cq[h:h + 1, :] - _tile_lanes(ckb, tq // LANES)
            if diag:
                s = jnp.where(causal, s, NEG)
            m_old = m_ref[h]
            m_new = jnp.maximum(m_old, jnp.max(s, axis=0, keepdims=True))
            p_ref[h % slots] = jnp.exp(s - m_new).astype(BF16)
            m_ref[h] = m_new
            pending.append((h, jnp.exp(m_old - m_new)))
            if len(pending) > lag:
                pv_update(*pending.pop(0))
        for item in pending:
            pv_update(*item)

    def body(c, carry):
        step(pl.multiple_of(c * tq, tq), False)
        return carry

    lax.fori_loop(0, i, body, 0)
    step(pl.multiple_of(i * tq, tq), True)
    for h in range(H_A):
        ot_ref[h * DH:(h + 1) * DH, :] = acc_ref[h] / l_ref[h]
    o_ref[...] = ot_ref[...].T


def _fox_prompt(q_pad, k_bf, vt, cum, cum_t, b, s, tq):
    nq = s // tq
    k3 = k_bf.reshape(b, s, W_A)
    return pl.pallas_call(
        functools.partial(_fox_body, tq=tq), grid=(b, nq), name="fox_prompt",
        in_specs=[pl.BlockSpec((tq, PADW), lambda i, j: (i * nq + j, 0)),
                  pl.BlockSpec((1, s, W_A), lambda i, j: (i, 0, 0)),
                  pl.BlockSpec((1, W_A, s), lambda i, j: (i, 0, 0)),
                  pl.BlockSpec((1, H_A, tq), lambda i, j: (i, 0, j)),
                  pl.BlockSpec((1, s, H_A), lambda i, j: (i, 0, 0))],
        out_specs=pl.BlockSpec((tq, W_A), lambda i, j: (i * nq + j, 0)),
        out_shape=jax.ShapeDtypeStruct((b * s, W_A), F32),
        scratch_shapes=[pltpu.VMEM((H_A, s, LANES), F32), pltpu.VMEM((H_A, DH, tq), F32),
                        pltpu.VMEM((H_A, 1, tq), F32), pltpu.VMEM((H_A, 1, tq), F32),
                        pltpu.VMEM((W_A, tq), F32), pltpu.VMEM((PIPE_SLOTS, tq, tq), F32),
                        pltpu.VMEM((PIPE_SLOTS, tq, tq), BF16)],
        compiler_params=_cparams(("parallel", "arbitrary")))(q_pad, k3, vt, cum_t, cum)


def _key_to_f32(key):
    bits = key ^ ((key >> 31) & 0x7FFFFFFF)
    return lax.bitcast_convert_type(bits, F32)


def _kth_largest(count_ge, shape, n_top):
    kf = jnp.float32(n_top)
    zero = jnp.zeros(shape, I32)
    t0 = jnp.where(count_ge(_key_to_f32(zero)) >= kf, zero, jnp.full(shape, INT_MIN, I32))

    def bit_step(it, t):
        cand = t + jnp.left_shift(jnp.int32(1), 30 - it)
        return jnp.where(count_ge(_key_to_f32(cand)) >= kf, cand, t)

    key = lax.fori_loop(0, 31, bit_step, t0)
    few = key == INT_MIN
    return jnp.where(few, -jnp.inf, _key_to_f32(key)), few


def _tie_cut(count_eq_below, shape, need, n_bits):
    def bit_step(it, j):
        cand = j + jnp.left_shift(jnp.int32(1), n_bits - 1 - it)
        return jnp.where(count_eq_below(cand) < need, cand, j)

    return lax.fori_loop(0, n_bits, bit_step, jnp.zeros(shape, I32))


def _dsa_body(qb_ref, qi_ref, wi_ref, ki_ref, k_ref, vt_ref, o_ref,
              sc_ref, bias_ref, acc_ref, m_ref, l_ref, ot_ref, s_ref, p_ref, *, tq, s_len, n_top):
    i = pl.program_id(1)
    nk = i + 1
    kidx0 = lax.broadcasted_iota(I32, (tq, tq), 0)
    qidx = i * tq + lax.broadcasted_iota(I32, (tq, tq), 1)
    wi = wi_ref[0]

    def score_chunk(c, _):
        off = pl.multiple_of(c * tq, tq)
        ki = ki_ref[0, pl.ds(off, tq), :]
        sc = jnp.zeros((tq, tq), F32)
        for h in range(H_I):
            r = _dot_nt(ki, qi_ref[:, h * LANES:(h + 1) * LANES])
            sc = sc + wi[h:h + 1, :] * jnp.maximum(r, 0.0)
        sc_ref[pl.ds(off, tq), :] = jnp.where(kidx0 + off <= qidx, sc, -jnp.inf)
        return 0

    lax.fori_loop(0, nk, score_chunk, 0)

    def count(pred):
        def body(c, cnt):
            off = pl.multiple_of(c * tq, tq)
            hit = jnp.where(pred(sc_ref[pl.ds(off, tq), :], off), 1.0, 0.0)
            parts = [hit[r * 8:(r + 1) * 8, :] for r in range(tq // 8)]
            while len(parts) > 1:
                parts = [parts[a] + parts[a + 1] for a in range(0, len(parts), 2)]
            return cnt + parts[0]
        cnt = lax.fori_loop(0, nk, body, jnp.zeros((8, tq), F32))
        return jnp.sum(cnt, axis=0, keepdims=True)

    thr, few = _kth_largest(lambda cand: count(lambda x, off: x >= cand), (1, tq), n_top)
    need = jnp.where(few, 0.0, jnp.float32(n_top) - count(lambda x, off: x > thr))
    below = jnp.where(lax.broadcasted_iota(I32, (tq, tq), 1) < kidx0, 1.0, 0.0).astype(BF16)

    def bias_chunk(c, seen):
        off = pl.multiple_of(c * tq, tq)
        x = sc_ref[pl.ds(off, tq), :]
        tied = jnp.where(x == thr, 1.0, 0.0)
        rank = _dot(below, tied.astype(BF16)) + seen
        keep_tied = jnp.where(rank < need, 0.0, NEG)
        bias_ref[pl.ds(off, tq), :] = jnp.where(x > thr, 0.0, jnp.where(x == thr, keep_tied, NEG))
        return seen + jnp.sum(tied, axis=0, keepdims=True)

    lax.fori_loop(0, nk, bias_chunk, jnp.zeros((1, tq), F32))

    acc_ref[...] = jnp.zeros_like(acc_ref)
    l_ref[...] = jnp.zeros_like(l_ref)
    m_ref[...] = jnp.full(m_ref.shape, NEG, F32)
    ones = jnp.ones((16, tq), BF16)

    def attend(c, _):
        off = pl.multiple_of(c * tq, tq)
        k = k_ref[0, pl.ds(off, tq), :]
        vt = jnp.concatenate([vt_ref[0, :, pl.ds(off, tq)], ones], axis=0)
        bias = bias_ref[pl.ds(off, tq), :]
        def qk(h):
            return _dot_nt(k, qb_ref[:, h * LANES:(h + 1) * LANES])

        def pv_update(h, alpha):
            n = h // G_B
            pv = _dot(vt, p_ref[h % PIPE_SLOTS])
            acc_ref[h] = alpha * acc_ref[h] + pv[n * DH:(n + 1) * DH, :]
            l_ref[h] = alpha * l_ref[h] + pv[LANES:LANES + 1, :]

        ahead, lag, slots = PIPE_AHEAD, PIPE_LAG, PIPE_SLOTS
        for h in range(ahead):
            s_ref[h % slots] = qk(h) + bias
        pending = []
        for h in range(H_B):
            if h + ahead < H_B:
                s_ref[(h + ahead) % slots] = qk(h + ahead) + bias
            m_old = m_ref[h]
            m_new = jnp.maximum(m_old, jnp.max(s_ref[h % slots], axis=0, keepdims=True))
            alpha = jnp.exp(m_old - m_new)
            p_ref[h % slots] = jnp.exp(s_ref[h % slots] - m_new).astype(BF16)
            m_ref[h] = m_new
            pending.append((h, alpha))
            if len(pending) > lag:
                pv_update(*pending.pop(0))
        for item in pending:
            pv_update(*item)
        return 0

    lax.fori_loop(0, nk, attend, 0)
    for h in range(H_B):
        ot_ref[h * DH:(h + 1) * DH, :] = acc_ref[h] / l_ref[h]
    o_ref[...] = ot_ref[...].T


def _dsa_prompt(qb_pad, qi_pad, wi_t, ki_pad, kvb_bf, vt, b, s, tq):
    nq = s // tq
    n_top = min(TOPK_MAX, s // 4)
    ki3 = ki_pad.reshape(b, s, LANES)
    kv3 = kvb_bf.reshape(b, s, 2 * LANES)
    rows = lambda i, j: (i * nq + j, 0)
    return pl.pallas_call(
        functools.partial(_dsa_body, tq=tq, s_len=s, n_top=n_top), grid=(b, nq), name="dsa_prompt",
        in_specs=[pl.BlockSpec((tq, PADW), rows), pl.BlockSpec((tq, PADW), rows),
                  pl.BlockSpec((1, H_I, tq), lambda i, j: (i, 0, j)),
                  pl.BlockSpec((1, s, LANES), lambda i, j: (i, 0, 0)),
                  pl.BlockSpec((1, s, LANES), lambda i, j: (i, 0, 0)),
                  pl.BlockSpec((1, LANES, s), lambda i, j: (i, 0, 0))],
        out_specs=pl.BlockSpec((tq, W_B), rows),
        out_shape=jax.ShapeDtypeStruct((b * s, W_B), F32),
        scratch_shapes=[pltpu.VMEM((s, tq), F32), pltpu.VMEM((s, tq), F32),
                        pltpu.VMEM((H_B, DH, tq), F32), pltpu.VMEM((H_B, 1, tq), F32),
                        pltpu.VMEM((H_B, 1, tq), F32), pltpu.VMEM((W_B, tq), F32),
                        pltpu.VMEM((PIPE_SLOTS, tq, tq), F32),
                        pltpu.VMEM((PIPE_SLOTS, tq, tq), BF16)],
        compiler_params=_cparams(("parallel", "arbitrary")))(qb_pad, qi_pad, wi_t, ki3, kv3, vt)


HALO = 32


def _conv_ln_silu(acc, g, b):
    mu = jnp.mean(acc, axis=1, keepdims=True)
    d = acc - mu
    var = jnp.mean(d * d, axis=1, keepdims=True)
    y = d * lax.rsqrt(var + LN_EPS) * g + b
    return y * _sigmoid(y)


def _conv_body(u_ref, halo_ref, w_ref, cb_ref, g_ref, b_ref, o_ref, ext_ref, sh_ref, *, tq):
    j = pl.program_id(1)
    ext_ref[0:HALO, :] = jnp.where(j > 0, halo_ref[...], 0.0)
    ext_ref[HALO:, :] = u_ref[...]
    base = HALO - (CONV_W - 1)
    for r in range(1, 8):
        sh_ref[r - 1] = ext_ref[r:r + tq + HALO - 8, :]
    accs = [jnp.zeros((tq, D_C), F32) + cb_ref[...], jnp.zeros((tq, D_C), F32)]
    for t in range(CONV_W):
        r, a = (base + t) % 8, (base + t) // 8
        if r == 0:
            x = ext_ref[8 * a:8 * a + tq, :]
        else:
            x = sh_ref[r - 1, 8 * a:8 * a + tq, :]
        accs[t % 2] = accs[t % 2] + x * w_ref[t:t + 1, :]
    o_ref[...] = _conv_ln_silu(accs[0] + accs[1], g_ref[...], b_ref[...])


def _conv_prompt(u, conv_w, conv_b, g, bb, b, s, tq):
    nq = s // tq
    hb = tq // HALO
    rows = lambda i, j: (i * nq + j, 0)
    return pl.pallas_call(
        functools.partial(_conv_body, tq=tq), grid=(b, nq), name="conv_prompt",
        in_specs=[pl.BlockSpec((tq, D_C), rows),
                  pl.BlockSpec((HALO, D_C), lambda i, j: (jnp.maximum((i * nq + j) * hb - 1, 0), 0)),
                  pl.BlockSpec((CONV_W, D_C), lambda i, j: (0, 0)),
                  pl.BlockSpec((1, D_C), lambda i, j: (0, 0)),
                  pl.BlockSpec((1, D_C), lambda i, j: (0, 0)),
                  pl.BlockSpec((1, D_C), lambda i, j: (0, 0))],
        out_specs=pl.BlockSpec((tq, D_C), rows),
        out_shape=jax.ShapeDtypeStruct((b * s, D_C), F32),
        scratch_shapes=[pltpu.VMEM((HALO + tq, D_C), F32),
                        pltpu.VMEM((7, tq + HALO - 8, D_C), F32)],
        compiler_params=_cparams(("parallel", "arbitrary")))(u, u, conv_w, conv_b, g, bb)


def _merge_body(x_ref, p_ref, ya_ref, yb_ref, yc_ref, sg_ref, wmg_ref, wa_ref, wb_ref, wc_ref,
                wo_ref, wpe_ref, wpg_ref, g_ref, b_ref, y_ref, ybf_ref, *, alpha):
    x = x_ref[...]
    gates = _sigmoid(_dot(x.astype(BF16), wmg_ref[...]))
    sg = sg_ref[...]
    ua = _dot((ya_ref[...] * sg[:, :W_A]).astype(BF16), wa_ref[...])
    ub = _dot((yb_ref[...] * sg[:, W_A:2 * W_A]).astype(BF16), wb_ref[...])
    uc = _dot((yc_ref[...] * sg[:, 2 * W_A:]).astype(BF16), wc_ref[...])
    m = (gates[:, :D_MODEL] * ua + gates[:, D_MODEL:2 * D_MODEL] * ub
         + gates[:, 2 * D_MODEL:] * uc)
    r = alpha * x + _dot(m.astype(BF16), wo_ref[...])
    mu = jnp.mean(r, axis=1, keepdims=True)
    d = r - mu
    var = jnp.mean(d * d, axis=1, keepdims=True)
    h = d * lax.rsqrt(var + LN_EPS) * g_ref[...] + b_ref[...]
    pe = _dot(p_ref[...].astype(BF16), wpe_ref[...])
    y = h + _sigmoid(_dot(h.astype(BF16), wpg_ref[...])) * pe
    y_ref[...] = y
    ybf_ref[...] = y.astype(BF16)


def _merge(x, p, ya, yb, yc, sg, wl, tm, alpha):
    m = x.shape[0]
    ws = [wl['w_mg'], wl['w_a'], wl['w_b'], wl['w_c'], wl['w_out'], wl['w_pe'], wl['w_pe_gate'],
          wl['ln_g'], wl['ln_b']]
    return pl.pallas_call(
        functools.partial(_merge_body, alpha=alpha), grid=(m // tm,), name="merge",
        in_specs=[_row_spec(tm, D_MODEL), _row_spec(tm, P_DIM), _row_spec(tm, W_A),
                  _row_spec(tm, W_B), _row_spec(tm, D_C), _row_spec(tm, 3 * W_A)]
                 + [_const_spec(w.shape) for w in ws],
        out_specs=[_row_spec(tm, D_MODEL), _row_spec(tm, D_MODEL)],
        out_shape=[jax.ShapeDtypeStruct((m, D_MODEL), F32), jax.ShapeDtypeStruct((m, D_MODEL), BF16)],
        compiler_params=_cparams(("parallel",)))(x, p, ya, yb, yc, sg, *ws)


def _fox_dec_body(pt_ref, q_ref, kvn_ref, lfn_ref, *refs, g, layer):
    kv_refs = refs[:g]
    lf_refs = refs[g:2 * g]
    o_ref = refs[2 * g]
    m_ref, l_ref, acc_ref, carry_ref = refs[2 * g + 1:]
    t = pl.program_id(1)
    qrows = q_ref[0]

    @pl.when(t == 0)
    def _():
        kn = kvn_ref[0][:, :W_A].astype(BF16).astype(F32)
        vn = kvn_ref[0][:, W_A:]
        s_new = jnp.sum(qrows.astype(F32) * kn, axis=1, keepdims=True)
        m_ref[...] = s_new
        l_ref[...] = jnp.ones_like(l_ref)
        acc_ref[...] = jnp.broadcast_to(vn.astype(BF16).astype(F32), (H_A, W_A))
        carry_ref[...] = lfn_ref[0]

    r = lax.broadcasted_iota(I32, (PAGE, PAGE), 0)
    c = lax.broadcasted_iota(I32, (PAGE, PAGE), 1)
    upper = jnp.where(r > c, 1.0, 0.0).astype(BF16)
    lf = jnp.concatenate([lf_refs[gi][0, 0] for gi in range(g)], axis=0)
    suf = _dot3(lf, upper)
    tot = jnp.sum(lf, axis=1, keepdims=True)
    carry = carry_ref[...]
    parts = []
    for gi in range(g):
        kt = kv_refs[gi][0, 0, :W_A, :].astype(BF16)
        bias = suf[gi * H_A:(gi + 1) * H_A, :] + carry
        carry = carry + tot[gi * H_A:(gi + 1) * H_A, :]
        parts.append(_dot(qrows, kt) + bias)
    carry_ref[...] = carry
    s = jnp.concatenate(parts, axis=1)
    m_old = m_ref[...]
    m_new = jnp.maximum(m_old, jnp.max(s, axis=1, keepdims=True))
    alpha = jnp.exp(m_old - m_new)
    p = jnp.exp(s - m_new)
    l_ref[...] = alpha * l_ref[...] + jnp.sum(p, axis=1, keepdims=True)
    pb = p.astype(BF16)
    acc = alpha * acc_ref[...]
    for gi in range(g):
        vt = kv_refs[gi][0, 0, W_A:, :].astype(BF16)
        acc = acc + _dot_nt(pb[:, gi * PAGE:(gi + 1) * PAGE], vt)
    acc_ref[...] = acc
    m_ref[...] = m_new

    @pl.when(t == pl.num_programs(1) - 1)
    def _():
        o_ref[0] = acc_ref[...] / l_ref[...]


def _fox_decode(qrows, kva, logf, page_table, kv_t, lf_t, g, layer):
    bd, n_pages = page_table.shape
    steps = n_pages // g

    def page_map(gi):
        return lambda b, t, pt: (layer, pt[b, n_pages - 1 - (t * g + gi)], 0, 0)

    per_b = lambda b, t, pt: (b, 0, 0)
    out = pl.pallas_call(
        functools.partial(_fox_dec_body, g=g, layer=layer), name="fox_decode",
        grid_spec=pltpu.PrefetchScalarGridSpec(
            num_scalar_prefetch=1, grid=(bd, steps),
            in_specs=[pl.BlockSpec((1, H_A, W_A), per_b), pl.BlockSpec((1, 1, 2 * W_A), per_b),
                      pl.BlockSpec((1, H_A, 1), per_b)]
                     + [pl.BlockSpec((1, 1, 2 * W_A, PAGE), page_map(gi)) for gi in range(g)]
                     + [pl.BlockSpec((1, 1, H_A, PAGE), page_map(gi)) for gi in range(g)],
            out_specs=pl.BlockSpec((1, H_A, W_A), per_b),
            scratch_shapes=[pltpu.VMEM((H_A, 1), F32), pltpu.VMEM((H_A, 1), F32),
                            pltpu.VMEM((H_A, W_A), F32), pltpu.VMEM((H_A, 1), F32)]),
        out_shape=jax.ShapeDtypeStruct((bd, H_A, W_A), F32),
        compiler_params=_cparams(("parallel", "arbitrary")))(
            page_table, qrows, kva.reshape(bd, 1, 2 * W_A), logf.reshape(bd, H_A, 1),
            *([kv_t] * g), *([lf_t] * g))
    o4 = out.reshape(bd, H_A, H_A, DH)
    return jnp.stack([o4[:, h, h] for h in range(H_A)], axis=1).reshape(bd, W_A)


def _idx_dec_body(pt_ref, qi_ref, wi_ref, kin_ref, *refs, g):
    ki_refs = refs[:g]
    o_ref, on_ref = refs[g:]
    qi = qi_ref[0]
    w = wi_ref[0]
    rows = []
    for gi in range(g):
        r = _dot(qi, ki_refs[gi][0, 0].astype(BF16))
        rows.append(jnp.sum(w * jnp.maximum(r, 0.0), axis=0, keepdims=True))
    o_ref[0] = jnp.concatenate(rows, axis=0)
    r_new = jnp.sum(qi.astype(F32) * kin_ref[0].astype(F32), axis=1, keepdims=True)
    sc_new = jnp.sum(w * jnp.maximum(r_new, 0.0), axis=0, keepdims=True)
    on_ref[0] = jnp.broadcast_to(sc_new, (1, LANES))


def _idx_decode(qi_bf, wi, ki_new_bf, page_table, ik_t, g, layer):
    bd, n_pages = page_table.shape
    steps = n_pages // g

    def page_map(gi):
        return lambda b, t, pt: (layer, pt[b, t * g + gi], 0, 0)

    per_b = lambda b, t, pt: (b, 0, 0)
    sc, sc_new = pl.pallas_call(
        functools.partial(_idx_dec_body, g=g), name="idx_decode",
        grid_spec=pltpu.PrefetchScalarGridSpec(
            num_scalar_prefetch=1, grid=(bd, steps),
            in_specs=[pl.BlockSpec((1, H_I, D_I), per_b), pl.BlockSpec((1, H_I, 1), per_b),
                      pl.BlockSpec((1, 1, D_I), per_b)]
                     + [pl.BlockSpec((1, 1, D_I, PAGE), page_map(gi)) for gi in range(g)],
            out_specs=[pl.BlockSpec((1, g, PAGE), lambda b, t, pt: (b, t, 0)),
                       pl.BlockSpec((1, 1, LANES), per_b)]),
        out_shape=[jax.ShapeDtypeStruct((bd, n_pages, PAGE), F32),
                   jax.ShapeDtypeStruct((bd, 1, LANES), F32)],
        compiler_params=_cparams(("parallel", "arbitrary")))(
            page_table, qi_bf, wi.reshape(bd, H_I, 1), ki_new_bf.reshape(bd, 1, D_I),
            *([ik_t] * g))
    return sc.reshape(bd, n_pages * PAGE), sc_new[:, 0, :1]


def _sel_dec_body(sc_ref, scn_ref, bias_ref, biasn_ref, *, n_top, n_bits):
    bd, p_len = sc_ref.shape
    sc = sc_ref[...]
    sc_n = scn_ref[...]
    idx = lax.broadcasted_iota(I32, (bd, p_len), 1)

    def count(pred, pred_n):
        return (jnp.sum(jnp.where(pred, 1.0, 0.0), axis=1, keepdims=True)
                + jnp.where(pred_n, 1.0, 0.0))

    thr, _ = _kth_largest(lambda cand: count(sc >= cand, sc_n >= cand), (bd, 1), n_top)
    need = jnp.float32(n_top) - count(sc > thr, sc_n > thr)
    cut = _tie_cut(lambda cand: jnp.sum(jnp.where((sc == thr) & (idx < cand), 1.0, 0.0),
                                        axis=1, keepdims=True), (bd, 1), need, n_bits)
    eq_past = jnp.sum(jnp.where(sc == thr, 1.0, 0.0), axis=1, keepdims=True)
    sel = (sc > thr) | ((sc == thr) & (idx <= cut))
    bias_ref[...] = jnp.where(sel, 0.0, NEG)
    sel_n = (sc_n > thr) | ((sc_n == thr) & (eq_past < need))
    biasn_ref[...] = jnp.where(sel_n, 0.0, NEG)


def _sel_decode(sc, sc_new, n_top):
    bd, p_len = sc.shape
    n_bits = int(np.log2(p_len))
    return pl.pallas_call(
        functools.partial(_sel_dec_body, n_top=n_top, n_bits=n_bits), name="sel_decode",
        out_shape=[jax.ShapeDtypeStruct((bd, p_len), F32), jax.ShapeDtypeStruct((bd, 1), F32)],
        compiler_params=pltpu.CompilerParams(vmem_limit_bytes=VMEM_LIMIT))(sc, sc_new)


def _dsa_dec_body(pt_ref, q_ref, kvn_ref, bn_ref, bias_ref, *refs, g):
    kv_refs = refs[:g]
    o_ref = refs[g]
    m_ref, l_ref, acc_ref = refs[g + 1:]
    t = pl.program_id(1)
    qrows = q_ref[0]

    @pl.when(t == 0)
    def _():
        kn = kvn_ref[0][:, :LANES].astype(BF16).astype(F32)
        vn = kvn_ref[0][:, LANES:].astype(BF16).astype(F32)
        s_new = jnp.sum(qrows.astype(F32) * kn, axis=1, keepdims=True) + bn_ref[0]
        m_ref[...] = s_new
        l_ref[...] = jnp.where(s_new > 0.5 * NEG, 1.0, 0.0)
        acc_ref[...] = jnp.where(s_new > 0.5 * NEG, jnp.broadcast_to(vn, (H_B, LANES)), 0.0)

    parts = []
    for gi in range(g):
        kt = kv_refs[gi][0, 0, :LANES, :].astype(BF16)
        parts.append(_dot(qrows, kt) + bias_ref[0, gi:gi + 1, :])
    s = jnp.concatenate(parts, axis=1)
    m_old = m_ref[...]
    m_new = jnp.maximum(m_old, jnp.max(s, axis=1, keepdims=True))
    alpha = jnp.exp(m_old - m_new)
    p = jnp.where(s > 0.5 * NEG, jnp.exp(s - m_new), 0.0)
    l_ref[...] = alpha * l_ref[...] + jnp.sum(p, axis=1, keepdims=True)
    pb = p.astype(BF16)
    acc = alpha * acc_ref[...]
    for gi in range(g):
        vt = kv_refs[gi][0, 0, LANES:, :].astype(BF16)
        acc = acc + _dot_nt(pb[:, gi * PAGE:(gi + 1) * PAGE], vt)
    acc_ref[...] = acc
    m_ref[...] = m_new

    @pl.when(t == pl.num_programs(1) - 1)
    def _():
        o_ref[0] = acc_ref[...] / l_ref[...]


def _dsa_decode(q2, kvb, bias, bias_new, page_table, kv_t, g, layer):
    bd, n_pages = page_table.shape
    steps = n_pages // g

    def page_map(gi):
        return lambda b, t, pt: (layer, pt[b, t * g + gi], 0, 0)

    per_b = lambda b, t, pt: (b, 0, 0)
    out = pl.pallas_call(
        functools.partial(_dsa_dec_body, g=g), name="dsa_decode",
        grid_spec=pltpu.PrefetchScalarGridSpec(
            num_scalar_prefetch=1, grid=(bd, steps),
            in_specs=[pl.BlockSpec((1, H_B, LANES), per_b), pl.BlockSpec((1, 1, 2 * LANES), per_b),
                      pl.BlockSpec((1, 1, 1), per_b),
                      pl.BlockSpec((1, g, PAGE), lambda b, t, pt: (b, t, 0))]
                     + [pl.BlockSpec((1, 1, 2 * LANES, PAGE), page_map(gi)) for gi in range(g)],
            out_specs=pl.BlockSpec((1, H_B, LANES), per_b),
            scratch_shapes=[pltpu.VMEM((H_B, 1), F32), pltpu.VMEM((H_B, 1), F32),
                            pltpu.VMEM((H_B, LANES), F32)]),
        out_shape=jax.ShapeDtypeStruct((bd, H_B, LANES), F32),
        compiler_params=_cparams(("parallel", "arbitrary")))(
            page_table, q2, kvb.reshape(bd, 1, 2 * LANES), bias_new.reshape(bd, 1, 1),
            bias.reshape(bd, n_pages, PAGE), *([kv_t] * g))
    o4 = out.reshape(bd, KV_B, G_B, KV_B, DH)
    return jnp.stack([o4[:, n, :, n] for n in range(KV_B)], axis=1).reshape(bd, W_B)


def _conv_dec_body(st_ref, u_ref, w_ref, cb_ref, g_ref, b_ref, o_ref):
    acc = u_ref[...] * w_ref[CONV_W - 1:CONV_W, :] + cb_ref[...]
    for t in range(CONV_W - 1):
        acc = acc + st_ref[0, t] * w_ref[t:t + 1, :]
    o_ref[...] = _conv_ln_silu(acc, g_ref[...], b_ref[...])


def _conv_decode(state_t, u, conv_w, conv_b, g, bb, layer):
    bd = u.shape[0]
    full = lambda a: pl.BlockSpec(a.shape, lambda i: (0,) * a.ndim)
    return pl.pallas_call(
        _conv_dec_body, grid=(1,), name="conv_decode",
        in_specs=[pl.BlockSpec((1,) + state_t.shape[1:], lambda i: (layer, 0, 0, 0)),
                  full(u), full(conv_w), full(conv_b), full(g), full(bb)],
        out_specs=pl.BlockSpec((bd, D_C), lambda i: (0, 0)),
        out_shape=jax.ShapeDtypeStruct((bd, D_C), F32),
        compiler_params=_cparams(("arbitrary",)))(state_t, u, conv_w, conv_b, g, bb)


def _rope_tables(pos):
    half = ROT_DIM // 2
    inv = ROPE_THETA ** (-jnp.arange(half, dtype=F32) / half)
    ang = pos.astype(F32)[:, None] * inv[None, :]
    cos, sin = jnp.cos(ang), jnp.sin(ang)
    n = pos.shape[0]
    pad = jnp.zeros((n, DH - ROT_DIM), F32)
    c = jnp.concatenate([cos, cos, pad + 1.0], axis=1)
    s1 = jnp.concatenate([jnp.zeros_like(sin), sin, pad], axis=1)
    s2 = jnp.concatenate([-sin, jnp.zeros_like(sin), pad], axis=1)
    return tuple(jnp.tile(t, (1, LANES // DH)) for t in (c, s1, s2))


def _pad_heads(w, half_of_head):
    d = w.shape[0]
    z = jnp.zeros((d, DH), w.dtype)
    cols = []
    for h in range(w.shape[1] // DH):
        wh = w[:, h * DH:(h + 1) * DH]
        cols += [wh, z] if half_of_head(h) == 0 else [z, wh]
    return jnp.concatenate(cols, axis=1)


def _regroup_weights(w_in, b_f):
    w = w_in.astype(BF16)
    d = w.shape[0]
    z = lambda n: jnp.zeros((d, n), BF16)
    foxq = jnp.concatenate([_pad_heads(w[:, O_QA:O_KA], lambda h: h % 2), w[:, O_FA:O_GA],
                            z(LANES - H_A)], axis=1)
    foxkv = w[:, O_KA:O_FA]
    gates = jnp.concatenate([w[:, O_GA:O_QB], w[:, O_GB:O_GLU], w[:, O_GC:O_MG]], axis=1)
    dsaq = jnp.concatenate([_pad_heads(w[:, O_QB:O_KB], lambda h: h // G_B),
                            _pad_heads(w[:, O_QI:O_KI], lambda h: 0)], axis=1)
    dsak = jnp.concatenate([w[:, O_KB:O_QI], w[:, O_KI:O_GB], z(LANES - D_I - H_I)], axis=1)
    glu = w[:, O_GLU:O_GC]
    vt = jnp.transpose(jnp.concatenate([w[:, O_VB:O_QI], w[:, O_VA:O_FA]], axis=1))
    bf = jnp.concatenate([b_f.astype(F32), jnp.zeros((LANES - H_A,), F32)])[None, :]
    return dict(foxq=foxq, foxkv=foxkv, gates=gates, dsaq=dsaq, dsak=dsak, glu=glu, vt=vt, b_f=bf,
                w_mg=w[:, O_MG:O_END])


def _pick(n, prefs):
    for t in prefs:
        if n % t == 0:
            return t
    return n


def kernel(x_prompt, x_sample, cache_fox_kv, cache_fox_logf, cache_dsa_kv, cache_dsa_idx_k, state_conv,
           page_table, p_prompt, p_sample, w_in, b_f, w_a, w_b, w_c, w_out, ln_g, ln_b,
           conv_w, conv_b, conv_ln_g, conv_ln_b, w_pe, w_pe_gate):
    depth = w_in.shape[0]
    b, s, d = x_prompt.shape
    bd = x_sample.shape[0]
    n_phys = cache_fox_kv.shape[1]
    n_pages = page_table.shape[1]
    p_len = n_pages * PAGE
    alpha = float((2 * depth) ** 0.25)
    tm = _pick(s, (256, 128))
    tq = _pick(s, (256, 128))
    g_fox = _pick(n_pages, (16, 8, 4, 2, 1))
    g_dsa = _pick(n_pages, (32, 16, 8, 4, 2, 1))
    g_idx = _pick(n_pages, (64, 32, 16, 8, 4, 2, 1))
    n_top_dec = min(TOPK_MAX, (p_len + 1) // 4)

    tabs_p = _rope_tables(jnp.arange(s, dtype=I32))
    tabs_s = _rope_tables(jnp.full((bd,), p_len, I32))
    page_table = page_table.astype(I32)

    fox_kv_t = jnp.transpose(cache_fox_kv, (0, 1, 3, 4, 5, 2)).reshape(depth, n_phys, 2 * W_A, PAGE)
    fox_lf_t = jnp.transpose(cache_fox_logf, (0, 1, 3, 2))
    dsa_kv_t = jnp.transpose(cache_dsa_kv, (0, 1, 3, 4, 5, 2)).reshape(depth, n_phys, 2 * LANES, PAGE)
    dsa_ik_t = jnp.transpose(cache_dsa_idx_k, (0, 1, 3, 2))
    state_t = jnp.transpose(state_conv, (0, 2, 1, 3))
    head_eye = jnp.repeat(jnp.eye(H_A, dtype=BF16), DH, axis=1)

    hp = x_prompt.reshape(b * s, d)
    hs = x_sample.reshape(bd, d)
    hp_bf = hp.astype(BF16)
    hs_bf = hs.astype(BF16)
    outs_p, outs_s = [], []
    for l in range(depth):
        wg = _regroup_weights(w_in[l], b_f[l])
        wl = dict(w_mg=wg['w_mg'], w_a=w_a[l].astype(BF16), w_b=w_b[l].astype(BF16),
                  w_c=w_c[l].astype(BF16), w_out=w_out[l].astype(BF16), w_pe=w_pe[l].astype(BF16),
                  w_pe_gate=w_pe_gate[l].astype(BF16), ln_g=ln_g[l][None, :], ln_b=ln_b[l][None, :])
        cw, cb = conv_w[l], conv_b[l][None, :]
        cg, cbb = conv_ln_g[l][None, :], conv_ln_b[l][None, :]

        pp = _project(hp_bf, wg, tabs_p, tm, s // tm, seq=(b, s))
        vbt, vat = pp['vbt'], pp['vat']
        logf3 = pp['logf'].reshape(b, s, H_A)
        cum = _cumsum(logf3, tq)
        cum_t = jnp.swapaxes(cum, 1, 2)
        wi_t = jnp.swapaxes(pp['wi'].reshape(b, s, H_I), 1, 2)
        ya = _fox_prompt(pp['q_pad'], pp['k_bf'], vat, cum, cum_t, b, s, tq)
        yb = _dsa_prompt(pp['qb_pad'], pp['qi_pad'], wi_t, pp['ki_pad'], pp['kvb_bf'], vbt, b, s, tq)
        yc = _conv_prompt(pp['u'], cw, cb, cg, cbb, b, s, tq)
        hp_new, hp_bf = _merge(hp, p_prompt[l].reshape(b * s, P_DIM), ya, yb, yc, pp['sg'], wl,
                               _pick(b * s, (256, 128)), alpha)
        outs_p.append((pp['kva'].reshape(b, s, 2, H_A, DH), logf3,
                       pp['kvb'].reshape(b, s, 2, KV_B, DH), pp['ki'].reshape(b, s, D_I),
                       pp['u'].reshape(b, s, D_C)[:, s - (CONV_W - 1):]))
        hp = hp_new

        ps = _project(hs_bf, wg, tabs_s, bd, 1)
        q8 = ps['q_pad'].reshape(bd, H_A, 2, DH)
        q_s = jnp.stack([q8[:, h, h % 2] for h in range(H_A)], axis=1).reshape(bd, 1, W_A)
        qrows = q_s * head_eye[None]
        ya_s = _fox_decode(qrows, ps['kva'], ps['logf'], page_table, fox_kv_t, fox_lf_t, g_fox, l)
        qi_s = ps['qi_pad'].reshape(bd, H_I, LANES)[:, :, :D_I]
        sc, sc_new = _idx_decode(qi_s, ps['wi'], ps['ki_pad'][:, :D_I], page_table, dsa_ik_t,
                                 g_idx, l)
        bias, bias_new = _sel_decode(sc, sc_new, n_top_dec)
        yb_s = _dsa_decode(ps['qb_pad'].reshape(bd, H_B, LANES), ps['kvb'], bias, bias_new,
                           page_table, dsa_kv_t, g_dsa, l)
        yc_s = _conv_decode(state_t, ps['u'], cw, cb, cg, cbb, l)
        hs_new, hs_bf = _merge(hs, p_sample[l].reshape(bd, P_DIM), ya_s, yb_s, yc_s, ps['sg'], wl,
                               bd, alpha)
        outs_s.append((ps['kva'].reshape(bd, 1, 2, H_A, DH), ps['logf'].reshape(bd, 1, H_A),
                       ps['kvb'].reshape(bd, 1, 2, KV_B, DH), ps['ki'].reshape(bd, 1, D_I),
                       jnp.concatenate([state_t[l, 1:], ps['u'][None]], axis=0)))
        hs = hs_new

    stack = lambda outs, k: jnp.stack([o[k] for o in outs])
    return (hp.reshape(b, s, d), hs.reshape(bd, 1, d),
            stack(outs_p, 0), stack(outs_s, 0), stack(outs_p, 1), stack(outs_s, 1),
            stack(outs_p, 2), stack(outs_s, 2), stack(outs_p, 3), stack(outs_s, 3),
            stack(outs_p, 4), jnp.transpose(stack(outs_s, 4), (0, 2, 1, 3)))
```

```python
import functools

import jax
import jax.numpy as jnp
import numpy as np
from jax import lax
from jax.experimental import pallas as pl
from jax.experimental.pallas import tpu as pltpu

F32 = jnp.float32
BF16 = jnp.bfloat16
I32 = jnp.int32

D_MODEL = 1024
H_A = 8
DH = 64
W_A = H_A * DH
KV_B = 2
H_B = 8
G_B = H_B // KV_B
W_B = H_B * DH
H_I = 8
D_I = 64
TOPK_MAX = 256
D_C = 512
CONV_W = 31
P_DIM = 256
N_BRANCH = 3
ROPE_THETA = 500000.0
ROT_DIM = DH // 4
LN_EPS = 1e-5
PAGE = 128
LANES = 128
PADW = H_A * LANES

PIPE_AHEAD = 4
PIPE_LAG = 4
PIPE_SLOTS = 8

NEG = -1e30
INT_MIN = -(2 ** 31)
VMEM_LIMIT = 56 * 1024 * 1024

_SPLIT = (W_A, W_A, W_A, H_A, W_A, W_B, KV_B * DH, KV_B * DH, H_I * D_I, D_I, H_I, W_B,
          2 * D_C, D_C, N_BRANCH * D_MODEL)
_OFF = np.concatenate([[0], np.cumsum(_SPLIT)]).tolist()
(O_QA, O_KA, O_VA, O_FA, O_GA, O_QB, O_KB, O_VB, O_QI, O_KI, O_WI, O_GB, O_GLU, O_GC, O_MG,
 O_END) = _OFF


def _cparams(sem):
    return pltpu.CompilerParams(dimension_semantics=sem, vmem_limit_bytes=VMEM_LIMIT)


def _dot(a, b):
    return jnp.dot(a, b, preferred_element_type=F32)


def _dot_nt(a, b):
    return lax.dot_general(a, b, (((1,), (1,)), ((), ())), preferred_element_type=F32)


def _sigmoid(x):
    return 1.0 / (1.0 + jnp.exp(-x))


def _split3(x):
    hi = x.astype(BF16)
    r1 = x - hi.astype(F32)
    mid = r1.astype(BF16)
    lo = (r1 - mid.astype(F32)).astype(BF16)
    return hi, mid, lo


def _dot3(x, w):
    hi, mid, lo = _split3(x)
    return _dot(hi, w) + _dot(mid, w) + _dot(lo, w)


def _rope(x, c, s1, s2):
    n = x.shape[-1]
    half = ROT_DIM // 2
    return x * c + pltpu.roll(x, half, 1) * s1 + pltpu.roll(x, n - half, 1) * s2


def _tile_lanes(t, reps):
    return t if reps == 1 else jnp.concatenate([t] * reps, axis=1)


def _spread_heads(y, heads_of_group):
    lane = lax.broadcasted_iota(I32, (y.shape[0], LANES), 1)
    out = [None] * (2 * len(heads_of_group))
    for g, (lo, hi) in enumerate(heads_of_group):
        yg = y[:, g * LANES:(g + 1) * LANES]
        out[lo] = jnp.where(lane < DH, yg, 0.0)
        out[hi] = jnp.where(lane >= DH, yg, 0.0)
    return jnp.concatenate(out, axis=1)


_PAIRS_ADJACENT = tuple((2 * g, 2 * g + 1) for g in range(H_A // 2))
_PAIRS_BY_KV = tuple((g, g + G_B) for g in range(G_B))


def _proj_foxq_body(x_ref, w_ref, bf_ref, q_ref, logf_ref):
    z = _dot(x_ref[...], w_ref[...])
    q_ref[...] = _spread_heads(z[:, :W_A] * (DH ** -0.5), _PAIRS_ADJACENT).astype(BF16)
    f = z[:, W_A:] + bf_ref[...]
    logf = jnp.minimum(f, 0.0) - jnp.log1p(jnp.exp(-jnp.abs(f)))
    logf_ref[...] = logf[:, :H_A]


def _proj_foxkv_body(x_ref, w_ref, k_ref, kv_ref):
    z = _dot(x_ref[...], w_ref[...])
    k_ref[...] = z[:, :W_A].astype(BF16)
    kv_ref[...] = z


def _proj_gates_body(x_ref, w_ref, o_ref):
    z = _dot(x_ref[...], w_ref[...])
    o_ref[...] = z * _sigmoid(z)


def _proj_dsaq_body(x_ref, w_ref, c_ref, s1_ref, s2_ref, qb_ref, qi_ref):
    z = _dot(x_ref[...], w_ref[...])
    reps = z.shape[1] // LANES
    y = _rope(z, _tile_lanes(c_ref[...], reps), _tile_lanes(s1_ref[...], reps),
              _tile_lanes(s2_ref[...], reps)) * (DH ** -0.5)
    qb_ref[...] = _spread_heads(y[:, :W_B], _PAIRS_BY_KV).astype(BF16)
    qi_ref[...] = _spread_heads(y[:, W_B:], _PAIRS_ADJACENT).astype(BF16)


def _proj_dsak_body(x_ref, w_ref, c_ref, s1_ref, s2_ref, kv_ref, kvbf_ref, ki_ref, kibf_ref, wi_ref):
    z = _dot(x_ref[...], w_ref[...])
    c, s1, s2 = c_ref[...], s1_ref[...], s2_ref[...]
    kb = _rope(z[:, :LANES], c, s1, s2)
    kv = jnp.concatenate([kb, z[:, LANES:2 * LANES]], axis=1)
    kv_ref[...] = kv
    kvbf_ref[...] = kv.astype(BF16)
    t = _rope(z[:, 2 * LANES:], c, s1, s2)
    ki_ref[...] = t[:, :D_I]
    lane = lax.broadcasted_iota(I32, t.shape, 1)
    kibf_ref[...] = jnp.where(lane < D_I, t, pltpu.roll(t, D_I, 1)).astype(BF16)
    wi_ref[...] = z[:, 2 * LANES + D_I:2 * LANES + D_I + H_I] * (H_I ** -0.5)


def _proj_glu_body(x_ref, w_ref, u_ref):
    z = _dot(x_ref[...], w_ref[...])
    u_ref[...] = z[:, :D_C] * _sigmoid(z[:, D_C:])


def _proj_t_body(x_ref, w_ref, vb_ref, kvt_ref, va_ref):
    zt = _dot_nt(w_ref[...], x_ref[...])
    vb_ref[0] = zt[:LANES].astype(BF16)
    kvt_ref[0, 0] = zt[LANES:]
    va_ref[0] = zt[LANES + W_A:].astype(BF16)


def _proj_foxk_body(x_ref, w_ref, k_ref):
    k_ref[...] = _dot(x_ref[...], w_ref[...]).astype(BF16)


def _row_spec(tm, n):
    return pl.BlockSpec((tm, n), lambda i: (i, 0))


def _const_spec(shape):
    return pl.BlockSpec(shape, lambda i: (0,) * len(shape), pipeline_mode=pl.Buffered(1))


_PROJ_ROW_OUTS = dict(q_pad=(PADW, BF16), logf=(H_A, F32), k_bf=(W_A, BF16), kva=(2 * W_A, F32),
                      sg=(3 * W_A, F32), qb_pad=(PADW, BF16), qi_pad=(PADW, BF16),
                      kvb=(2 * LANES, F32), kvb_bf=(2 * LANES, BF16), ki=(D_I, F32),
                      ki_pad=(LANES, BF16), wi=(H_I, F32), u=(D_C, F32))


def _proj_body(*refs, names, transposed):
    n_in = 11 + (2 if transposed else 0)
    (x_ref, wq_ref, wkv_ref, wg_ref, wdq_ref, wdk_ref, wglu_ref, bf_ref, c_ref, s1_ref,
     s2_ref) = refs[:11]
    o = dict(zip(names, refs[n_in:]))
    _proj_foxq_body(x_ref, wq_ref, bf_ref, o['q_pad'], o['logf'])
    if transposed:
        _proj_foxk_body(x_ref, wkv_ref, o['k_bf'])
        _proj_t_body(x_ref, refs[11], o['vbt'], o['kvat'], o['vat'])
    else:
        _proj_foxkv_body(x_ref, wkv_ref, o['k_bf'], o['kva'])
    _proj_gates_body(x_ref, wg_ref, o['sg'])
    _proj_dsaq_body(x_ref, wdq_ref, c_ref, s1_ref, s2_ref, o['qb_pad'], o['qi_pad'])
    _proj_dsak_body(x_ref, wdk_ref, c_ref, s1_ref, s2_ref, o['kvb'], o['kvb_bf'], o['ki'],
                    o['ki_pad'], o['wi'])
    _proj_glu_body(x_ref, wglu_ref, o['u'])


def _project(x_bf, wg, tabs, tm, tab_blocks, stacked=None):
    m = x_bf.shape[0]
    transposed = stacked is not None
    tab_spec = pl.BlockSpec((tm, LANES), lambda i: (i % tab_blocks, 0))
    names = [k for k in _PROJ_ROW_OUTS if not (transposed and k == 'kva')]
    ws = [wg['foxq'], wg['foxk'] if transposed else wg['foxkv'], wg['gates'], wg['dsaq'],
          wg['dsak'], wg['glu'], wg['b_f']]
    in_specs = ([_row_spec(tm, D_MODEL)] + [_const_spec(w.shape) for w in ws] + [tab_spec] * 3)
    args = [x_bf] + ws + list(tabs)
    out_specs = [_row_spec(tm, _PROJ_ROW_OUTS[k][0]) for k in names]
    out_shape = [jax.ShapeDtypeStruct((m,) + _PROJ_ROW_OUTS[k][:1], _PROJ_ROW_OUTS[k][1])
                 for k in names]
    aliases = {}
    if transposed:
        kvat, layer, b, s = stacked
        nt = s // tm
        in_specs += [_const_spec(wg['t'].shape), pl.BlockSpec(memory_space=pl.ANY)]
        args += [wg['t'], kvat]
        seq_block = lambda rows: pl.BlockSpec((1, rows, tm), lambda i: (i // nt, 0, i % nt))
        out_specs += [seq_block(LANES),
                      pl.BlockSpec((1, 1, 2 * W_A, tm), lambda i: (layer, i // nt, 0, i % nt)),
                      seq_block(W_A)]
        out_shape += [jax.ShapeDtypeStruct((b, LANES, s), BF16),
                      jax.ShapeDtypeStruct(kvat.shape, F32),
                      jax.ShapeDtypeStruct((b, W_A, s), BF16)]
        aliases = {len(args) - 1: len(names) + 1}
        names = names + ['vbt', 'kvat', 'vat']
    outs = pl.pallas_call(
        functools.partial(_proj_body, names=tuple(names), transposed=transposed),
        grid=(m // tm,), name="project", in_specs=in_specs, out_specs=out_specs,
        out_shape=out_shape, input_output_aliases=aliases,
        compiler_params=_cparams(("parallel",)))(*args)
    return dict(zip(names, outs))


def _cumsum_body(x_ref, o_ref, carry_ref):
    @pl.when(pl.program_id(1) == 0)
    def _():
        carry_ref[...] = jnp.zeros_like(carry_ref)

    t = x_ref.shape[1]
    r = lax.broadcasted_iota(I32, (t, t), 0)
    c = lax.broadcasted_iota(I32, (t, t), 1)
    tri = jnp.where(c <= r, 1.0, 0.0).astype(BF16)
    hi, mid, lo = _split3(x_ref[0])
    cum = _dot(tri, hi) + _dot(tri, mid) + _dot(tri, lo) + carry_ref[...]
    o_ref[0] = cum
    carry_ref[...] = cum[t - 1:t, :]


def _cumsum(logf, tb):
    b, s, h = logf.shape
    return pl.pallas_call(
        _cumsum_body, grid=(b, s // tb), name="cumsum",
        in_specs=[pl.BlockSpec((1, tb, h), lambda i, j: (i, j, 0))],
        out_specs=pl.BlockSpec((1, tb, h), lambda i, j: (i, j, 0)),
        out_shape=jax.ShapeDtypeStruct((b, s, h), F32),
        scratch_shapes=[pltpu.VMEM((1, h), F32)],
        compiler_params=_cparams(("parallel", "arbitrary")))(logf)


def _fox_body(q_ref, k_ref, vt_ref, cq_ref, ck_ref, o_ref, ckb_ref, acc_ref, m_ref, l_ref, ot_ref,
              s_ref, p_ref, *, tq):
    i = pl.program_id(1)

    @pl.when(i == 0)
    def _():
        ck = ck_ref[0]
        for h in range(H_A):
            ckb_ref[h] = jnp.broadcast_to(ck[:, h:h + 1], (ck.shape[0], LANES))

    acc_ref[...] = jnp.zeros_like(acc_ref)
    l_ref[...] = jnp.zeros_like(l_ref)
    m_ref[...] = jnp.full(m_ref.shape, NEG, F32)
    cq = cq_ref[0]
    ones = jnp.ones((16, tq), BF16)

    def step(off, diag):
        if diag:
            krow = lax.broadcasted_iota(I32, (tq, tq), 0)
            qcol = lax.broadcasted_iota(I32, (tq, tq), 1)
            causal = krow <= qcol

        def qk(h):
            grp = slice((h // 2) * LANES, (h // 2 + 1) * LANES)
            k = k_ref[0, pl.ds(off, tq), grp]
            return _dot_nt(k, q_ref[:, h * LANES:(h + 1) * LANES])

        def pv_update(h, alpha):
            vt = jnp.concatenate([vt_ref[0, h * DH:(h + 1) * DH, pl.ds(off, tq)], ones], axis=0)
            pv = _dot(vt, p_ref[h % PIPE_SLOTS])
            acc_ref[h] = alpha * acc_ref[h] + pv[:DH, :]
            l_ref[h] = alpha * l_ref[h] + pv[DH:DH + 1, :]

        ahead, lag, slots = PIPE_AHEAD, PIPE_LAG, PIPE_SLOTS
        for h in range(ahead):
            s_ref[h % slots] = qk(h)
        pending = []
        for h in range(H_A):
            if h + ahead < H_A:
                s_ref[(h + ahead) % slots] = qk(h + ahead)
            ckb = ckb_ref[h, pl.ds(off, tq), :]
            s = s_ref[h % slots] + cq[h:h + 1, :] - _tile_lanes(ckb, tq // LANES)
            if diag:
                s = jnp.where(causal, s, NEG)
            m_old = m_ref[h]
            m_new = jnp.maximum(m_old, jnp.max(s, axis=0, keepdims=True))
            p_ref[h % slots] = jnp.exp(s - m_new).astype(BF16)
            m_ref[h] = m_new
            pending.append((h, jnp.exp(m_old - m_new)))
            if len(pending) > lag:
                pv_update(*pending.pop(0))
        for item in pending:
            pv_update(*item)

    def body(c, carry):
        step(pl.multiple_of(c * tq, tq), False)
        return carry

    lax.fori_loop(0, i, body, 0)
    step(pl.multiple_of(i * tq, tq), True)
    for h in range(H_A):
        ot_ref[h * DH:(h + 1) * DH, :] = acc_ref[h] / l_ref[h]
    o_ref[...] = ot_ref[...].T


def _fox_prompt(q_pad, k_bf, vt, cum, cum_t, b, s, tq):
    nq = s // tq
    k3 = k_bf.reshape(b, s, W_A)
    return pl.pallas_call(
        functools.partial(_fox_body, tq=tq), grid=(b, nq), name="fox_prompt",
        in_specs=[pl.BlockSpec((tq, PADW), lambda i, j: (i * nq + j, 0)),
                  pl.BlockSpec((1, s, W_A), lambda i, j: (i, 0, 0)),
                  pl.BlockSpec((1, W_A, s), lambda i, j: (i, 0, 0)),
                  pl.BlockSpec((1, H_A, tq), lambda i, j: (i, 0, j)),
                  pl.BlockSpec((1, s, H_A), lambda i, j: (i, 0, 0))],
        out_specs=pl.BlockSpec((tq, W_A), lambda i, j: (i * nq + j, 0)),
        out_shape=jax.ShapeDtypeStruct((b * s, W_A), F32),
        scratch_shapes=[pltpu.VMEM((H_A, s, LANES), F32), pltpu.VMEM((H_A, DH, tq), F32),
                        pltpu.VMEM((H_A, 1, tq), F32), pltpu.VMEM((H_A, 1, tq), F32),
                        pltpu.VMEM((W_A, tq), F32), pltpu.VMEM((PIPE_SLOTS, tq, tq), F32),
                        pltpu.VMEM((PIPE_SLOTS, tq, tq), BF16)],
        compiler_params=_cparams(("parallel", "arbitrary")))(q_pad, k3, vt, cum_t, cum)


def _key_to_f32(key):
    bits = key ^ ((key >> 31) & 0x7FFFFFFF)
    return lax.bitcast_convert_type(bits, F32)


def _kth_largest(count_ge, shape, n_top):
    kf = jnp.float32(n_top)
    zero = jnp.zeros(shape, I32)
    t0 = jnp.where(count_ge(_key_to_f32(zero)) >= kf, zero, jnp.full(shape, INT_MIN, I32))

    def bit_step(it, t):
        cand = t + jnp.left_shift(jnp.int32(1), 30 - it)
        return jnp.where(count_ge(_key_to_f32(cand)) >= kf, cand, t)

    key = lax.fori_loop(0, 31, bit_step, t0)
    few = key == INT_MIN
    return jnp.where(few, -jnp.inf, _key_to_f32(key)), few


def _tie_cut(count_eq_below, shape, need, n_bits):
    def bit_step(it, j):
        cand = j + jnp.left_shift(jnp.int32(1), n_bits - 1 - it)
        return jnp.where(count_eq_below(cand) < need, cand, j)

    return lax.fori_loop(0, n_bits, bit_step, jnp.zeros(shape, I32))


def _dsa_body(qb_ref, qi_ref, wi_ref, ki_ref, k_ref, vt_ref, o_ref,
              sc_ref, bias_ref, acc_ref, m_ref, l_ref, ot_ref, s_ref, p_ref, *, tq, s_len, n_top):
    i = pl.program_id(1)
    nk = i + 1
    kidx0 = lax.broadcasted_iota(I32, (tq, tq), 0)
    qidx = i * tq + lax.broadcasted_iota(I32, (tq, tq), 1)
    wi = wi_ref[0]

    def score_chunk(c, _):
        off = pl.multiple_of(c * tq, tq)
        ki = ki_ref[0, pl.ds(off, tq), :]
        sc = jnp.zeros((tq, tq), F32)
        for h in range(H_I):
            r = _dot_nt(ki, qi_ref[:, h * LANES:(h + 1) * LANES])
            sc = sc + wi[h:h + 1, :] * jnp.maximum(r, 0.0)
        sc_ref[pl.ds(off, tq), :] = jnp.where(kidx0 + off <= qidx, sc, -jnp.inf)
        return 0

    lax.fori_loop(0, nk, score_chunk, 0)

    def count(pred):
        def body(c, cnt):
            off = pl.multiple_of(c * tq, tq)
            hit = jnp.where(pred(sc_ref[pl.ds(off, tq), :], off), 1.0, 0.0)
            parts = [hit[r * 8:(r + 1) * 8, :] for r in range(tq // 8)]
            while len(parts) > 1:
                parts = [parts[a] + parts[a + 1] for a in range(0, len(parts), 2)]
            return cnt + parts[0]
        cnt = lax.fori_loop(0, nk, body, jnp.zeros((8, tq), F32))
        return jnp.sum(cnt, axis=0, keepdims=True)

    thr, few = _kth_largest(lambda cand: count(lambda x, off: x >= cand), (1, tq), n_top)
    need = jnp.where(few, 0.0, jnp.float32(n_top) - count(lambda x, off: x > thr))
    below = jnp.where(lax.broadcasted_iota(I32, (tq, tq), 1) < kidx0, 1.0, 0.0).astype(BF16)

    def bias_chunk(c, seen):
        off = pl.multiple_of(c * tq, tq)
        x = sc_ref[pl.ds(off, tq), :]
        tied = jnp.where(x == thr, 1.0, 0.0)
        rank = _dot(below, tied.astype(BF16)) + seen
        keep_tied = jnp.where(rank < need, 0.0, NEG)
        bias_ref[pl.ds(off, tq), :] = jnp.where(x > thr, 0.0, jnp.where(x == thr, keep_tied, NEG))
        return seen + jnp.sum(tied, axis=0, keepdims=True)

    lax.fori_loop(0, nk, bias_chunk, jnp.zeros((1, tq), F32))

    acc_ref[...] = jnp.zeros_like(acc_ref)
    l_ref[...] = jnp.zeros_like(l_ref)
    m_ref[...] = jnp.full(m_ref.shape, NEG, F32)
    ones = jnp.ones((16, tq), BF16)

    def attend(c, _):
        off = pl.multiple_of(c * tq, tq)
        k = k_ref[0, pl.ds(off, tq), :]
        vt = jnp.concatenate([vt_ref[0, :, pl.ds(off, tq)], ones], axis=0)
        bias = bias_ref[pl.ds(off, tq), :]
        def qk(h):
            return _dot_nt(k, qb_ref[:, h * LANES:(h + 1) * LANES])

        def pv_update(h, alpha):
            n = h // G_B
            pv = _dot(vt, p_ref[h % PIPE_SLOTS])
            acc_ref[h] = alpha * acc_ref[h] + pv[n * DH:(n + 1) * DH, :]
            l_ref[h] = alpha * l_ref[h] + pv[LANES:LANES + 1, :]

        ahead, lag, slots = PIPE_AHEAD, PIPE_LAG, PIPE_SLOTS
        for h in range(ahead):
            s_ref[h % slots] = qk(h) + bias
        pending = []
        for h in range(H_B):
            if h + ahead < H_B:
                s_ref[(h + ahead) % slots] = qk(h + ahead) + bias
            m_old = m_ref[h]
            m_new = jnp.maximum(m_old, jnp.max(s_ref[h % slots], axis=0, keepdims=True))
            alpha = jnp.exp(m_old - m_new)
            p_ref[h % slots] = jnp.exp(s_ref[h % slots] - m_new).astype(BF16)
            m_ref[h] = m_new
            pending.append((h, alpha))
            if len(pending) > lag:
                pv_update(*pending.pop(0))
        for item in pending:
            pv_update(*item)
        return 0

    lax.fori_loop(0, nk, attend, 0)
    for h in range(H_B):
        ot_ref[h * DH:(h + 1) * DH, :] = acc_ref[h] / l_ref[h]
    o_ref[...] = ot_ref[...].T


def _dsa_prompt(qb_pad, qi_pad, wi_t, ki_pad, kvb_bf, vt, b, s, tq):
    nq = s // tq
    n_top = min(TOPK_MAX, s // 4)
    ki3 = ki_pad.reshape(b, s, LANES)
    kv3 = kvb_bf.reshape(b, s, 2 * LANES)
    rows = lambda i, j: (i * nq + j, 0)
    return pl.pallas_call(
        functools.partial(_dsa_body, tq=tq, s_len=s, n_top=n_top), grid=(b, nq), name="dsa_prompt",
        in_specs=[pl.BlockSpec((tq, PADW), rows), pl.BlockSpec((tq, PADW), rows),
                  pl.BlockSpec((1, H_I, tq), lambda i, j: (i, 0, j)),
                  pl.BlockSpec((1, s, LANES), lambda i, j: (i, 0, 0)),
                  pl.BlockSpec((1, s, LANES), lambda i, j: (i, 0, 0)),
                  pl.BlockSpec((1, LANES, s), lambda i, j: (i, 0, 0))],
        out_specs=pl.BlockSpec((tq, W_B), rows),
        out_shape=jax.ShapeDtypeStruct((b * s, W_B), F32),
        scratch_shapes=[pltpu.VMEM((s, tq), F32), pltpu.VMEM((s, tq), F32),
                        pltpu.VMEM((H_B, DH, tq), F32), pltpu.VMEM((H_B, 1, tq), F32),
                        pltpu.VMEM((H_B, 1, tq), F32), pltpu.VMEM((W_B, tq), F32),
                        pltpu.VMEM((PIPE_SLOTS, tq, tq), F32),
                        pltpu.VMEM((PIPE_SLOTS, tq, tq), BF16)],
        compiler_params=_cparams(("parallel", "arbitrary")))(qb_pad, qi_pad, wi_t, ki3, kv3, vt)


HALO = 32


def _conv_ln_silu(acc, g, b):
    mu = jnp.mean(acc, axis=1, keepdims=True)
    d = acc - mu
    var = jnp.mean(d * d, axis=1, keepdims=True)
    y = d * lax.rsqrt(var + LN_EPS) * g + b
    return y * _sigmoid(y)


def _conv_body(u_ref, halo_ref, w_ref, cb_ref, g_ref, b_ref, o_ref, ext_ref, sh_ref, *, tq):
    j = pl.program_id(1)
    ext_ref[0:HALO, :] = jnp.where(j > 0, halo_ref[...], 0.0)
    ext_ref[HALO:, :] = u_ref[...]
    base = HALO - (CONV_W - 1)
    for r in range(1, 8):
        sh_ref[r - 1] = ext_ref[r:r + tq + HALO - 8, :]
    accs = [jnp.zeros((tq, D_C), F32) + cb_ref[...], jnp.zeros((tq, D_C), F32)]
    for t in range(CONV_W):
        r, a = (base + t) % 8, (base + t) // 8
        if r == 0:
            x = ext_ref[8 * a:8 * a + tq, :]
        else:
            x = sh_ref[r - 1, 8 * a:8 * a + tq, :]
        accs[t % 2] = accs[t % 2] + x * w_ref[t:t + 1, :]
    o_ref[...] = _conv_ln_silu(accs[0] + accs[1], g_ref[...], b_ref[...])


def _conv_prompt(u, conv_w, conv_b, g, bb, b, s, tq):
    nq = s // tq
    hb = tq // HALO
    rows = lambda i, j: (i * nq + j, 0)
    return pl.pallas_call(
        functools.partial(_conv_body, tq=tq), grid=(b, nq), name="conv_prompt",
        in_specs=[pl.BlockSpec((tq, D_C), rows),
                  pl.BlockSpec((HALO, D_C), lambda i, j: (jnp.maximum((i * nq + j) * hb - 1, 0), 0)),
                  pl.BlockSpec((CONV_W, D_C), lambda i, j: (0, 0)),
                  pl.BlockSpec((1, D_C), lambda i, j: (0, 0)),
                  pl.BlockSpec((1, D_C), lambda i, j: (0, 0)),
                  pl.BlockSpec((1, D_C), lambda i, j: (0, 0))],
        out_specs=pl.BlockSpec((tq, D_C), rows),
        out_shape=jax.ShapeDtypeStruct((b * s, D_C), F32),
        scratch_shapes=[pltpu.VMEM((HALO + tq, D_C), F32),
                        pltpu.VMEM((7, tq + HALO - 8, D_C), F32)],
        compiler_params=_cparams(("parallel", "arbitrary")))(u, u, conv_w, conv_b, g, bb)


def _merge_body(x_ref, p_ref, ya_ref, yb_ref, yc_ref, sg_ref, wmg_ref, wa_ref, wb_ref, wc_ref,
                wo_ref, wpe_ref, wpg_ref, g_ref, b_ref, y_ref, ybf_ref, *, alpha):
    x = x_ref[...]
    gates = _sigmoid(_dot(x.astype(BF16), wmg_ref[...]))
    sg = sg_ref[...]
    ua = _dot((ya_ref[...] * sg[:, :W_A]).astype(BF16), wa_ref[...])
    ub = _dot((yb_ref[...] * sg[:, W_A:2 * W_A]).astype(BF16), wb_ref[...])
    uc = _dot((yc_ref[...] * sg[:, 2 * W_A:]).astype(BF16), wc_ref[...])
    m = (gates[:, :D_MODEL] * ua + gates[:, D_MODEL:2 * D_MODEL] * ub
         + gates[:, 2 * D_MODEL:] * uc)
    r = alpha * x + _dot(m.astype(BF16), wo_ref[...])
    mu = jnp.mean(r, axis=1, keepdims=True)
    d = r - mu
    var = jnp.mean(d * d, axis=1, keepdims=True)
    h = d * lax.rsqrt(var + LN_EPS) * g_ref[...] + b_ref[...]
    pe = _dot(p_ref[...].astype(BF16), wpe_ref[...])
    y = h + _sigmoid(_dot(h.astype(BF16), wpg_ref[...])) * pe
    y_ref[...] = y
    ybf_ref[...] = y.astype(BF16)


def _merge(x, p, ya, yb, yc, sg, wl, tm, alpha):
    m = x.shape[0]
    ws = [wl['w_mg'], wl['w_a'], wl['w_b'], wl['w_c'], wl['w_out'], wl['w_pe'], wl['w_pe_gate'],
          wl['ln_g'], wl['ln_b']]
    return pl.pallas_call(
        functools.partial(_merge_body, alpha=alpha), grid=(m // tm,), name="merge",
        in_specs=[_row_spec(tm, D_MODEL), _row_spec(tm, P_DIM), _row_spec(tm, W_A),
                  _row_spec(tm, W_B), _row_spec(tm, D_C), _row_spec(tm, 3 * W_A)]
                 + [_const_spec(w.shape) for w in ws],
        out_specs=[_row_spec(tm, D_MODEL), _row_spec(tm, D_MODEL)],
        out_shape=[jax.ShapeDtypeStruct((m, D_MODEL), F32), jax.ShapeDtypeStruct((m, D_MODEL), BF16)],
        compiler_params=_cparams(("parallel",)))(x, p, ya, yb, yc, sg, *ws)


def _fox_dec_body(pt_ref, q_ref, kvn_ref, lfn_ref, *refs, g, layer):
    kv_refs = refs[:g]
    lf_refs = refs[g:2 * g]
    o_ref = refs[2 * g]
    m_ref, l_ref, acc_ref, carry_ref = refs[2 * g + 1:]
    t = pl.program_id(1)
    qrows = q_ref[0]

    @pl.when(t == 0)
    def _():
        kn = kvn_ref[0][:, :W_A].astype(BF16).astype(F32)
        vn = kvn_ref[0][:, W_A:]
        s_new = jnp.sum(qrows.astype(F32) * kn, axis=1, keepdims=True)
        m_ref[...] = s_new
        l_ref[...] = jnp.ones_like(l_ref)
        acc_ref[...] = jnp.broadcast_to(vn.astype(BF16).astype(F32), (H_A, W_A))
        carry_ref[...] = lfn_ref[0]

    r = lax.broadcasted_iota(I32, (PAGE, PAGE), 0)
    c = lax.broadcasted_iota(I32, (PAGE, PAGE), 1)
    upper = jnp.where(r > c, 1.0, 0.0).astype(BF16)
    lf = jnp.concatenate([lf_refs[gi][0, 0] for gi in range(g)], axis=0)
    suf = _dot3(lf, upper)
    tot = jnp.sum(lf, axis=1, keepdims=True)
    carry = carry_ref[...]
    parts = []
    for gi in range(g):
        kt = kv_refs[gi][0, 0, :W_A, :].astype(BF16)
        bias = suf[gi * H_A:(gi + 1) * H_A, :] + carry
        carry = carry + tot[gi * H_A:(gi + 1) * H_A, :]
        parts.append(_dot(qrows, kt) + bias)
    carry_ref[...] = carry
    s = jnp.concatenate(parts, axis=1)
    m_old = m_ref[...]
    m_new = jnp.maximum(m_old, jnp.max(s, axis=1, keepdims=True))
    alpha = jnp.exp(m_old - m_new)
    p = jnp.exp(s - m_new)
    l_ref[...] = alpha * l_ref[...] + jnp.sum(p, axis=1, keepdims=True)
    pb = p.astype(BF16)
    acc = alpha * acc_ref[...]
    for gi in range(g):
        vt = kv_refs[gi][0, 0, W_A:, :].astype(BF16)
        acc = acc + _dot_nt(pb[:, gi * PAGE:(gi + 1) * PAGE], vt)
    acc_ref[...] = acc
    m_ref[...] = m_new

    @pl.when(t == pl.num_programs(1) - 1)
    def _():
        o_ref[0] = acc_ref[...] / l_ref[...]


def _fox_decode(qrows, kva, logf, page_table, kv_t, lf_t, g, layer):
    bd, n_pages = page_table.shape
    steps = n_pages // g

    def page_map(gi):
        return lambda b, t, pt: (layer, pt[b, n_pages - 1 - (t * g + gi)], 0, 0)

    per_b = lambda b, t, pt: (b, 0, 0)
    out = pl.pallas_call(
        functools.partial(_fox_dec_body, g=g, layer=layer), name="fox_decode",
        grid_spec=pltpu.PrefetchScalarGridSpec(
            num_scalar_prefetch=1, grid=(bd, steps),
            in_specs=[pl.BlockSpec((1, H_A, W_A), per_b), pl.BlockSpec((1, 1, 2 * W_A), per_b),
                      pl.BlockSpec((1, H_A, 1), per_b)]
                     + [pl.BlockSpec((1, 1, 2 * W_A, PAGE), page_map(gi)) for gi in range(g)]
                     + [pl.BlockSpec((1, 1, H_A, PAGE), page_map(gi)) for gi in range(g)],
            out_specs=pl.BlockSpec((1, H_A, W_A), per_b),
            scratch_shapes=[pltpu.VMEM((H_A, 1), F32), pltpu.VMEM((H_A, 1), F32),
                            pltpu.VMEM((H_A, W_A), F32), pltpu.VMEM((H_A, 1), F32)]),
        out_shape=jax.ShapeDtypeStruct((bd, H_A, W_A), F32),
        compiler_params=_cparams(("parallel", "arbitrary")))(
            page_table, qrows, kva.reshape(bd, 1, 2 * W_A), logf.reshape(bd, H_A, 1),
            *([kv_t] * g), *([lf_t] * g))
    o4 = out.reshape(bd, H_A, H_A, DH)
    return jnp.stack([o4[:, h, h] for h in range(H_A)], axis=1).reshape(bd, W_A)


def _idx_dec_body(pt_ref, qi_ref, wi_ref, kin_ref, *refs, g):
    ki_refs = refs[:g]
    o_ref, on_ref = refs[g:]
    qi = qi_ref[0]
    w = wi_ref[0]
    rows = []
    for gi in range(g):
        r = _dot(qi, ki_refs[gi][0, 0].astype(BF16))
        rows.append(jnp.sum(w * jnp.maximum(r, 0.0), axis=0, keepdims=True))
    o_ref[0] = jnp.concatenate(rows, axis=0)
    r_new = jnp.sum(qi.astype(F32) * kin_ref[0].astype(F32), axis=1, keepdims=True)
    sc_new = jnp.sum(w * jnp.maximum(r_new, 0.0), axis=0, keepdims=True)
    on_ref[0] = jnp.broadcast_to(sc_new, (1, LANES))


def _idx_decode(qi_bf, wi, ki_new_bf, page_table, ik_t, g, layer):
    bd, n_pages = page_table.shape
    steps = n_pages // g

    def page_map(gi):
        return lambda b, t, pt: (layer, pt[b, t * g + gi], 0, 0)

    per_b = lambda b, t, pt: (b, 0, 0)
    sc, sc_new = pl.pallas_call(
        functools.partial(_idx_dec_body, g=g), name="idx_decode",
        grid_spec=pltpu.PrefetchScalarGridSpec(
            num_scalar_prefetch=1, grid=(bd, steps),
            in_specs=[pl.BlockSpec((1, H_I, D_I), per_b), pl.BlockSpec((1, H_I, 1), per_b),
                      pl.BlockSpec((1, 1, D_I), per_b)]
                     + [pl.BlockSpec((1, 1, D_I, PAGE), page_map(gi)) for gi in range(g)],
            out_specs=[pl.BlockSpec((1, g, PAGE), lambda b, t, pt: (b, t, 0)),
                       pl.BlockSpec((1, 1, LANES), per_b)]),
        out_shape=[jax.ShapeDtypeStruct((bd, n_pages, PAGE), F32),
                   jax.ShapeDtypeStruct((bd, 1, LANES), F32)],
        compiler_params=_cparams(("parallel", "arbitrary")))(
            page_table, qi_bf, wi.reshape(bd, H_I, 1), ki_new_bf.reshape(bd, 1, D_I),
            *([ik_t] * g))
    return sc.reshape(bd, n_pages * PAGE), sc_new[:, 0, :1]


def _sel_dec_body(sc_ref, scn_ref, bias_ref, biasn_ref, *, n_top, n_bits):
    bd, p_len = sc_ref.shape
    sc = sc_ref[...]
    sc_n = scn_ref[...]
    idx = lax.broadcasted_iota(I32, (bd, p_len), 1)

    def count(pred, pred_n):
        return (jnp.sum(jnp.where(pred, 1.0, 0.0), axis=1, keepdims=True)
                + jnp.where(pred_n, 1.0, 0.0))

    thr, _ = _kth_largest(lambda cand: count(sc >= cand, sc_n >= cand), (bd, 1), n_top)
    need = jnp.float32(n_top) - count(sc > thr, sc_n > thr)
    cut = _tie_cut(lambda cand: jnp.sum(jnp.where((sc == thr) & (idx < cand), 1.0, 0.0),
                                        axis=1, keepdims=True), (bd, 1), need, n_bits)
    eq_past = jnp.sum(jnp.where(sc == thr, 1.0, 0.0), axis=1, keepdims=True)
    sel = (sc > thr) | ((sc == thr) & (idx <= cut))
    bias_ref[...] = jnp.where(sel, 0.0, NEG)
    sel_n = (sc_n > thr) | ((sc_n == thr) & (eq_past < need))
    biasn_ref[...] = jnp.where(sel_n, 0.0, NEG)


def _sel_decode(sc, sc_new, n_top):
    bd, p_len = sc.shape
    n_bits = int(np.log2(p_len))
    return pl.pallas_call(
        functools.partial(_sel_dec_body, n_top=n_top, n_bits=n_bits), name="sel_decode",
        out_shape=[jax.ShapeDtypeStruct((bd, p_len), F32), jax.ShapeDtypeStruct((bd, 1), F32)],
        compiler_params=pltpu.CompilerParams(vmem_limit_bytes=VMEM_LIMIT))(sc, sc_new)


def _dsa_dec_body(pt_ref, q_ref, kvn_ref, bn_ref, bias_ref, *refs, g):
    kv_refs = refs[:g]
    o_ref = refs[g]
    m_ref, l_ref, acc_ref = refs[g + 1:]
    t = pl.program_id(1)
    qrows = q_ref[0]

    @pl.when(t == 0)
    def _():
        kn = kvn_ref[0][:, :LANES].astype(BF16).astype(F32)
        vn = kvn_ref[0][:, LANES:].astype(BF16).astype(F32)
        s_new = jnp.sum(qrows.astype(F32) * kn, axis=1, keepdims=True) + bn_ref[0]
        m_ref[...] = s_new
        l_ref[...] = jnp.where(s_new > 0.5 * NEG, 1.0, 0.0)
        acc_ref[...] = jnp.where(s_new > 0.5 * NEG, jnp.broadcast_to(vn, (H_B, LANES)), 0.0)

    parts = []
    for gi in range(g):
        kt = kv_refs[gi][0, 0, :LANES, :].astype(BF16)
        parts.append(_dot(qrows, kt) + bias_ref[0, gi:gi + 1, :])
    s = jnp.concatenate(parts, axis=1)
    m_old = m_ref[...]
    m_new = jnp.maximum(m_old, jnp.max(s, axis=1, keepdims=True))
    alpha = jnp.exp(m_old - m_new)
    p = jnp.where(s > 0.5 * NEG, jnp.exp(s - m_new), 0.0)
    l_ref[...] = alpha * l_ref[...] + jnp.sum(p, axis=1, keepdims=True)
    pb = p.astype(BF16)
    acc = alpha * acc_ref[...]
    for gi in range(g):
        vt = kv_refs[gi][0, 0, LANES:, :].astype(BF16)
        acc = acc + _dot_nt(pb[:, gi * PAGE:(gi + 1) * PAGE], vt)
    acc_ref[...] = acc
    m_ref[...] = m_new

    @pl.when(t == pl.num_programs(1) - 1)
    def _():
        o_ref[0] = acc_ref[...] / l_ref[...]


def _dsa_decode(q2, kvb, bias, bias_new, page_table, kv_t, g, layer):
    bd, n_pages = page_table.shape
    steps = n_pages // g

    def page_map(gi):
        return lambda b, t, pt: (layer, pt[b, t * g + gi], 0, 0)

    per_b = lambda b, t, pt: (b, 0, 0)
    out = pl.pallas_call(
        functools.partial(_dsa_dec_body, g=g), name="dsa_decode",
        grid_spec=pltpu.PrefetchScalarGridSpec(
            num_scalar_prefetch=1, grid=(bd, steps),
            in_specs=[pl.BlockSpec((1, H_B, LANES), per_b), pl.BlockSpec((1, 1, 2 * LANES), per_b),
                      pl.BlockSpec((1, 1, 1), per_b),
                      pl.BlockSpec((1, g, PAGE), lambda b, t, pt: (b, t, 0))]
                     + [pl.BlockSpec((1, 1, 2 * LANES, PAGE), page_map(gi)) for gi in range(g)],
            out_specs=pl.BlockSpec((1, H_B, LANES), per_b),
            scratch_shapes=[pltpu.VMEM((H_B, 1), F32), pltpu.VMEM((H_B, 1), F32),
                            pltpu.VMEM((H_B, LANES), F32)]),
        out_shape=jax.ShapeDtypeStruct((bd, H_B, LANES), F32),
        compiler_params=_cparams(("parallel", "arbitrary")))(
            page_table, q2, kvb.reshape(bd, 1, 2 * LANES), bias_new.reshape(bd, 1, 1),
            bias.reshape(bd, n_pages, PAGE), *([kv_t] * g))
    o4 = out.reshape(bd, KV_B, G_B, KV_B, DH)
    return jnp.stack([o4[:, n, :, n] for n in range(KV_B)], axis=1).reshape(bd, W_B)


def _conv_dec_body(st_ref, u_ref, w_ref, cb_ref, g_ref, b_ref, o_ref):
    acc = u_ref[...] * w_ref[CONV_W - 1:CONV_W, :] + cb_ref[...]
    for t in range(CONV_W - 1):
        acc = acc + st_ref[0, t] * w_ref[t:t + 1, :]
    o_ref[...] = _conv_ln_silu(acc, g_ref[...], b_ref[...])


def _conv_decode(state_t, u, conv_w, conv_b, g, bb, layer):
    bd = u.shape[0]
    full = lambda a: pl.BlockSpec(a.shape, lambda i: (0,) * a.ndim)
    return pl.pallas_call(
        _conv_dec_body, grid=(1,), name="conv_decode",
        in_specs=[pl.BlockSpec((1,) + state_t.shape[1:], lambda i: (layer, 0, 0, 0)),
                  full(u), full(conv_w), full(conv_b), full(g), full(bb)],
        out_specs=pl.BlockSpec((bd, D_C), lambda i: (0, 0)),
        out_shape=jax.ShapeDtypeStruct((bd, D_C), F32),
        compiler_params=_cparams(("arbitrary",)))(state_t, u, conv_w, conv_b, g, bb)


def _rope_tables(pos):
    half = ROT_DIM // 2
    inv = ROPE_THETA ** (-jnp.arange(half, dtype=F32) / half)
    ang = pos.astype(F32)[:, None] * inv[None, :]
    cos, sin = jnp.cos(ang), jnp.sin(ang)
    n = pos.shape[0]
    pad = jnp.zeros((n, DH - ROT_DIM), F32)
    c = jnp.concatenate([cos, cos, pad + 1.0], axis=1)
    s1 = jnp.concatenate([jnp.zeros_like(sin), sin, pad], axis=1)
    s2 = jnp.concatenate([-sin, jnp.zeros_like(sin), pad], axis=1)
    return tuple(jnp.tile(t, (1, LANES // DH)) for t in (c, s1, s2))


def _unspread(q_pad, half_of_head):
    q = q_pad.reshape(q_pad.shape[0], H_A, 2, DH)
    return jnp.stack([q[:, h, half_of_head(h)] for h in range(H_A)], axis=1)


def _regroup_weights(w_in, b_f):
    w = w_in.astype(BF16)
    d = w.shape[0]
    z = lambda n: jnp.zeros((d, n), BF16)
    foxq = jnp.concatenate([w[:, O_QA:O_KA], w[:, O_FA:O_GA], z(LANES - H_A)], axis=1)
    foxkv = w[:, O_KA:O_FA]
    gates = jnp.concatenate([w[:, O_GA:O_QB], w[:, O_GB:O_GLU], w[:, O_GC:O_MG]], axis=1)
    wqb = w[:, O_QB:O_KB].reshape(d, KV_B, G_B, DH)
    wqb = jnp.swapaxes(wqb, 1, 2).reshape(d, W_B)
    dsaq = jnp.concatenate([wqb, w[:, O_QI:O_KI]], axis=1)
    dsak = jnp.concatenate([w[:, O_KB:O_QI], w[:, O_KI:O_GB], z(LANES - D_I - H_I)], axis=1)
    glu = w[:, O_GLU:O_GC]
    wt = jnp.transpose(jnp.concatenate([w[:, O_VB:O_QI], w[:, O_KA:O_FA]], axis=1))
    foxk = w[:, O_KA:O_VA]
    bf = jnp.concatenate([b_f.astype(F32), jnp.zeros((LANES - H_A,), F32)])[None, :]
    return dict(foxq=foxq, foxkv=foxkv, foxk=foxk, gates=gates, dsaq=dsaq, dsak=dsak, glu=glu, t=wt,
                b_f=bf,
                w_mg=w[:, O_MG:O_END])


def _pick(n, prefs):
    for t in prefs:
        if n % t == 0:
            return t
    return n


def kernel(x_prompt, x_sample, cache_fox_kv, cache_fox_logf, cache_dsa_kv, cache_dsa_idx_k, state_conv,
           page_table, p_prompt, p_sample, w_in, b_f, w_a, w_b, w_c, w_out, ln_g, ln_b,
           conv_w, conv_b, conv_ln_g, conv_ln_b, w_pe, w_pe_gate):
    depth = w_in.shape[0]
    b, s, d = x_prompt.shape
    bd = x_sample.shape[0]
    n_phys = cache_fox_kv.shape[1]
    n_pages = page_table.shape[1]
    p_len = n_pages * PAGE
    alpha = float((2 * depth) ** 0.25)
    tm = _pick(s, (256, 128))
    tq = _pick(s, (256, 128))
    g_fox = _pick(n_pages, (16, 8, 4, 2, 1))
    g_dsa = _pick(n_pages, (32, 16, 8, 4, 2, 1))
    g_idx = _pick(n_pages, (64, 32, 16, 8, 4, 2, 1))
    n_top_dec = min(TOPK_MAX, (p_len + 1) // 4)

    tabs_p = _rope_tables(jnp.arange(s, dtype=I32))
    tabs_s = _rope_tables(jnp.full((bd,), p_len, I32))
    page_table = page_table.astype(I32)

    fox_kv_t = jnp.transpose(cache_fox_kv, (0, 1, 3, 4, 5, 2)).reshape(depth, n_phys, 2 * W_A, PAGE)
    fox_lf_t = jnp.transpose(cache_fox_logf, (0, 1, 3, 2))
    dsa_kv_t = jnp.transpose(cache_dsa_kv, (0, 1, 3, 4, 5, 2)).reshape(depth, n_phys, 2 * LANES, PAGE)
    dsa_ik_t = jnp.transpose(cache_dsa_idx_k, (0, 1, 3, 2))
    state_t = jnp.transpose(state_conv, (0, 2, 1, 3))
    head_eye = jnp.repeat(jnp.eye(H_A, dtype=BF16), DH, axis=1)

    hp = x_prompt.reshape(b * s, d)
    hs = x_sample.reshape(bd, d)
    hp_bf = hp.astype(BF16)
    hs_bf = hs.astype(BF16)
    outs_p, outs_s = [], []
    kvat = jnp.zeros((depth, b, 2 * W_A, s), F32)
    for l in range(depth):
        wg = _regroup_weights(w_in[l], b_f[l])
        wl = dict(w_mg=wg['w_mg'], w_a=w_a[l].astype(BF16), w_b=w_b[l].astype(BF16),
                  w_c=w_c[l].astype(BF16), w_out=w_out[l].astype(BF16), w_pe=w_pe[l].astype(BF16),
                  w_pe_gate=w_pe_gate[l].astype(BF16), ln_g=ln_g[l][None, :], ln_b=ln_b[l][None, :])
        cw, cb = conv_w[l], conv_b[l][None, :]
        cg, cbb = conv_ln_g[l][None, :], conv_ln_b[l][None, :]

        pp = _project(hp_bf, wg, tabs_p, tm, s // tm, stacked=(kvat, l, b, s))
        vbt, vat, kvat = pp['vbt'], pp['vat'], pp['kvat']
        logf3 = pp['logf'].reshape(b, s, H_A)
        cum = _cumsum(logf3, _pick(s, (1024, 512, 256, 128)))
        cum_t = jnp.swapaxes(cum, 1, 2)
        wi_t = jnp.swapaxes(pp['wi'].reshape(b, s, H_I), 1, 2)
        ya = _fox_prompt(pp['q_pad'], pp['k_bf'], vat, cum, cum_t, b, s, tq)
        yb = _dsa_prompt(pp['qb_pad'], pp['qi_pad'], wi_t, pp['ki_pad'], pp['kvb_bf'], vbt, b, s, tq)
        yc = _conv_prompt(pp['u'], cw, cb, cg, cbb, b, s, tq)
        hp_new, hp_bf = _merge(hp, p_prompt[l].reshape(b * s, P_DIM), ya, yb, yc, pp['sg'], wl,
                               _pick(b * s, (256, 128)), alpha)
        outs_p.append((None, logf3,
                       pp['kvb'].reshape(b, s, 2, KV_B, DH), pp['ki'].reshape(b, s, D_I),
                       pp['u'].reshape(b, s, D_C)[:, s - (CONV_W - 1):]))
        hp = hp_new

        ps = _project(hs_bf, wg, tabs_s, bd, 1)
        q_s = _unspread(ps['q_pad'], lambda h: h % 2).reshape(bd, 1, W_A)
        qrows = q_s * head_eye[None]
        ya_s = _fox_decode(qrows, ps['kva'], ps['logf'], page_table, fox_kv_t, fox_lf_t, g_fox, l)
        qi_s = _unspread(ps['qi_pad'], lambda h: h % 2)
        sc, sc_new = _idx_decode(qi_s, ps['wi'], ps['ki_pad'][:, :D_I], page_table, dsa_ik_t,
                                 g_idx, l)
        bias, bias_new = _sel_decode(sc, sc_new, n_top_dec)
        yb_s = _dsa_decode(ps['qb_pad'].reshape(bd, H_B, LANES), ps['kvb'], bias, bias_new,
                           page_table, dsa_kv_t, g_dsa, l)
        yc_s = _conv_decode(state_t, ps['u'], cw, cb, cg, cbb, l)
        hs_new, hs_bf = _merge(hs, p_sample[l].reshape(bd, P_DIM), ya_s, yb_s, yc_s, ps['sg'], wl,
                               bd, alpha)
        outs_s.append((ps['kva'].reshape(bd, 1, 2, H_A, DH), ps['logf'].reshape(bd, 1, H_A),
                       ps['kvb'].reshape(bd, 1, 2, KV_B, DH), ps['ki'].reshape(bd, 1, D_I),
                       jnp.concatenate([state_t[l, 1:], ps['u'][None]], axis=0)))
        hs = hs_new

    stack = lambda outs, k: jnp.stack([o[k] for o in outs])
    fox_kv_prompt = jnp.transpose(kvat.reshape(depth, b, 2, H_A, DH, s), (0, 1, 5, 2, 3, 4))
    return (hp.reshape(b, s, d), hs.reshape(bd, 1, d),
            fox_kv_prompt, stack(outs_s, 0), stack(outs_p, 1), stack(outs_s, 1),
            stack(outs_p, 2), stack(outs_s, 2), stack(outs_p, 3), stack(outs_s, 3),
            stack(outs_p, 4), jnp.transpose(stack(outs_s, 4), (0, 2, 1, 3)))
```

```python
import functools

import jax
import jax.numpy as jnp
import numpy as np
from jax import lax
from jax.experimental import pallas as pl
from jax.experimental.pallas import tpu as pltpu

F32 = jnp.float32
BF16 = jnp.bfloat16
I32 = jnp.int32

D_MODEL = 1024
H_A = 8
DH = 64
W_A = H_A * DH
KV_B = 2
H_B = 8
G_B = H_B // KV_B
W_B = H_B * DH
H_I = 8
D_I = 64
TOPK_MAX = 256
D_C = 512
CONV_W = 31
P_DIM = 256
N_BRANCH = 3
ROPE_THETA = 500000.0
ROT_DIM = DH // 4
LN_EPS = 1e-5
PAGE = 128
LANES = 128
PADW = H_A * LANES

PIPE_AHEAD = 3
PIPE_LAG = 2
PIPE_SLOTS = 8

NEG = -1e30
INT_MIN = -(2 ** 31)
VMEM_LIMIT = 56 * 1024 * 1024

_SPLIT = (W_A, W_A, W_A, H_A, W_A, W_B, KV_B * DH, KV_B * DH, H_I * D_I, D_I, H_I, W_B,
          2 * D_C, D_C, N_BRANCH * D_MODEL)
_OFF = np.concatenate([[0], np.cumsum(_SPLIT)]).tolist()
(O_QA, O_KA, O_VA, O_FA, O_GA, O_QB, O_KB, O_VB, O_QI, O_KI, O_WI, O_GB, O_GLU, O_GC, O_MG,
 O_END) = _OFF


def _cparams(sem):
    return pltpu.CompilerParams(dimension_semantics=sem, vmem_limit_bytes=VMEM_LIMIT)


def _dot(a, b):
    return jnp.dot(a, b, preferred_element_type=F32)


def _dot_nt(a, b):
    return lax.dot_general(a, b, (((1,), (1,)), ((), ())), preferred_element_type=F32)


def _sigmoid(x):
    return 1.0 / (1.0 + jnp.exp(-x))


def _split3(x):
    hi = x.astype(BF16)
    r1 = x - hi.astype(F32)
    mid = r1.astype(BF16)
    lo = (r1 - mid.astype(F32)).astype(BF16)
    return hi, mid, lo


def _dot3(x, w):
    hi, mid, lo = _split3(x)
    return _dot(hi, w) + _dot(mid, w) + _dot(lo, w)


def _rope(x, c, s1, s2):
    n = x.shape[-1]
    half = ROT_DIM // 2
    return x * c + pltpu.roll(x, half, 1) * s1 + pltpu.roll(x, n - half, 1) * s2


def _tile_lanes(t, reps):
    return t if reps == 1 else jnp.concatenate([t] * reps, axis=1)


def _spread_heads(y, heads_of_group):
    lane = lax.broadcasted_iota(I32, (y.shape[0], LANES), 1)
    out = [None] * (2 * len(heads_of_group))
    for g, (lo, hi) in enumerate(heads_of_group):
        yg = y[:, g * LANES:(g + 1) * LANES]
        out[lo] = jnp.where(lane < DH, yg, 0.0)
        out[hi] = jnp.where(lane >= DH, yg, 0.0)
    return jnp.concatenate(out, axis=1)


_PAIRS_ADJACENT = tuple((2 * g, 2 * g + 1) for g in range(H_A // 2))
_PAIRS_BY_KV = tuple((g, g + G_B) for g in range(G_B))


def _proj_foxq_body(x_ref, w_ref, bf_ref, q_ref, logf_ref):
    z = _dot(x_ref[...], w_ref[...])
    q_ref[...] = _spread_heads(z[:, :W_A] * (DH ** -0.5), _PAIRS_ADJACENT).astype(BF16)
    f = z[:, W_A:] + bf_ref[...]
    logf = jnp.minimum(f, 0.0) - jnp.log1p(jnp.exp(-jnp.abs(f)))
    logf_ref[...] = logf[:, :H_A]


def _proj_foxkv_body(x_ref, w_ref, k_ref, kv_ref):
    z = _dot(x_ref[...], w_ref[...])
    k_ref[...] = z[:, :W_A].astype(BF16)
    kv_ref[...] = z


def _proj_gates_body(x_ref, w_ref, o_ref):
    z = _dot(x_ref[...], w_ref[...])
    o_ref[...] = z * _sigmoid(z)


def _proj_dsaq_body(x_ref, w_ref, c_ref, s1_ref, s2_ref, qb_ref, qi_ref):
    z = _dot(x_ref[...], w_ref[...])
    reps = z.shape[1] // LANES
    y = _rope(z, _tile_lanes(c_ref[...], reps), _tile_lanes(s1_ref[...], reps),
              _tile_lanes(s2_ref[...], reps)) * (DH ** -0.5)
    qb_ref[...] = _spread_heads(y[:, :W_B], _PAIRS_BY_KV).astype(BF16)
    qi_ref[...] = _spread_heads(y[:, W_B:], _PAIRS_ADJACENT).astype(BF16)


def _proj_dsak_body(x_ref, w_ref, c_ref, s1_ref, s2_ref, kv_ref, kvbf_ref, ki_ref, kibf_ref, wi_ref):
    z = _dot(x_ref[...], w_ref[...])
    c, s1, s2 = c_ref[...], s1_ref[...], s2_ref[...]
    kb = _rope(z[:, :LANES], c, s1, s2)
    kv = jnp.concatenate([kb, z[:, LANES:2 * LANES]], axis=1)
    kv_ref[...] = kv
    kvbf_ref[...] = kv.astype(BF16)
    t = _rope(z[:, 2 * LANES:], c, s1, s2)
    ki_ref[...] = t[:, :D_I]
    lane = lax.broadcasted_iota(I32, t.shape, 1)
    kibf_ref[...] = jnp.where(lane < D_I, t, pltpu.roll(t, D_I, 1)).astype(BF16)
    wi_ref[...] = z[:, 2 * LANES + D_I:2 * LANES + D_I + H_I] * (H_I ** -0.5)


def _proj_glu_body(x_ref, w_ref, u_ref):
    z = _dot(x_ref[...], w_ref[...])
    u_ref[...] = z[:, :D_C] * _sigmoid(z[:, D_C:])


def _proj_t_body(x_ref, w_ref, vb_ref, kvt_ref, va_ref):
    zt = _dot_nt(w_ref[...], x_ref[...])
    vb_ref[0] = zt[:LANES].astype(BF16)
    kvt_ref[0, 0] = zt[LANES:]
    va_ref[0] = zt[LANES + W_A:].astype(BF16)


def _proj_foxk_body(x_ref, w_ref, k_ref):
    k_ref[...] = _dot(x_ref[...], w_ref[...]).astype(BF16)


def _row_spec(tm, n):
    return pl.BlockSpec((tm, n), lambda i: (i, 0))


def _const_spec(shape):
    return pl.BlockSpec(shape, lambda i: (0,) * len(shape), pipeline_mode=pl.Buffered(1))


_PROJ_ROW_OUTS = dict(q_pad=(PADW, BF16), logf=(H_A, F32), k_bf=(W_A, BF16), kva=(2 * W_A, F32),
                      sg=(3 * W_A, F32), qb_pad=(PADW, BF16), qi_pad=(PADW, BF16),
                      kvb=(2 * LANES, F32), kvb_bf=(2 * LANES, BF16), ki=(D_I, F32),
                      ki_pad=(LANES, BF16), wi=(H_I, F32), u=(D_C, F32))


def _proj_body(*refs, names, transposed):
    n_in = 11 + (2 if transposed else 0)
    (x_ref, wq_ref, wkv_ref, wg_ref, wdq_ref, wdk_ref, wglu_ref, bf_ref, c_ref, s1_ref,
     s2_ref) = refs[:11]
    o = dict(zip(names, refs[n_in:]))
    _proj_foxq_body(x_ref, wq_ref, bf_ref, o['q_pad'], o['logf'])
    if transposed:
        _proj_foxk_body(x_ref, wkv_ref, o['k_bf'])
        _proj_t_body(x_ref, refs[11], o['vbt'], o['kvat'], o['vat'])
    else:
        _proj_foxkv_body(x_ref, wkv_ref, o['k_bf'], o['kva'])
    _proj_gates_body(x_ref, wg_ref, o['sg'])
    _proj_dsaq_body(x_ref, wdq_ref, c_ref, s1_ref, s2_ref, o['qb_pad'], o['qi_pad'])
    _proj_dsak_body(x_ref, wdk_ref, c_ref, s1_ref, s2_ref, o['kvb'], o['kvb_bf'], o['ki'],
                    o['ki_pad'], o['wi'])
    _proj_glu_body(x_ref, wglu_ref, o['u'])


def _project(x_bf, wg, tabs, tm, tab_blocks, stacked=None):
    m = x_bf.shape[0]
    transposed = stacked is not None
    tab_spec = pl.BlockSpec((tm, LANES), lambda i: (i % tab_blocks, 0))
    names = [k for k in _PROJ_ROW_OUTS if not (transposed and k == 'kva')]
    ws = [wg['foxq'], wg['foxk'] if transposed else wg['foxkv'], wg['gates'], wg['dsaq'],
          wg['dsak'], wg['glu'], wg['b_f']]
    in_specs = ([_row_spec(tm, D_MODEL)] + [_const_spec(w.shape) for w in ws] + [tab_spec] * 3)
    args = [x_bf] + ws + list(tabs)
    out_specs = [_row_spec(tm, _PROJ_ROW_OUTS[k][0]) for k in names]
    out_shape = [jax.ShapeDtypeStruct((m,) + _PROJ_ROW_OUTS[k][:1], _PROJ_ROW_OUTS[k][1])
                 for k in names]
    aliases = {}
    if transposed:
        kvat, layer, b, s = stacked
        nt = s // tm
        in_specs += [_const_spec(wg['t'].shape), pl.BlockSpec(memory_space=pl.ANY)]
        args += [wg['t'], kvat]
        seq_block = lambda rows: pl.BlockSpec((1, rows, tm), lambda i: (i // nt, 0, i % nt))
        out_specs += [seq_block(LANES),
                      pl.BlockSpec((1, 1, 2 * W_A, tm), lambda i: (layer, i // nt, 0, i % nt)),
                      seq_block(W_A)]
        out_shape += [jax.ShapeDtypeStruct((b, LANES, s), BF16),
                      jax.ShapeDtypeStruct(kvat.shape, F32),
                      jax.ShapeDtypeStruct((b, W_A, s), BF16)]
        aliases = {len(args) - 1: len(names) + 1}
        names = names + ['vbt', 'kvat', 'vat']
    outs = pl.pallas_call(
        functools.partial(_proj_body, names=tuple(names), transposed=transposed),
        grid=(m // tm,), name="project", in_specs=in_specs, out_specs=out_specs,
        out_shape=out_shape, input_output_aliases=aliases,
        compiler_params=_cparams(("parallel",)))(*args)
    return dict(zip(names, outs))


def _cumsum_body(x_ref, o_ref, carry_ref):
    @pl.when(pl.program_id(1) == 0)
    def _():
        carry_ref[...] = jnp.zeros_like(carry_ref)

    t = x_ref.shape[1]
    r = lax.broadcasted_iota(I32, (t, t), 0)
    c = lax.broadcasted_iota(I32, (t, t), 1)
    tri = jnp.where(c <= r, 1.0, 0.0).astype(BF16)
    hi, mid, lo = _split3(x_ref[0])
    cum = _dot(tri, hi) + _dot(tri, mid) + _dot(tri, lo) + carry_ref[...]
    o_ref[0] = cum
    carry_ref[...] = cum[t - 1:t, :]


def _cumsum(logf, tb):
    b, s, h = logf.shape
    return pl.pallas_call(
        _cumsum_body, grid=(b, s // tb), name="cumsum",
        in_specs=[pl.BlockSpec((1, tb, h), lambda i, j: (i, j, 0))],
        out_specs=pl.BlockSpec((1, tb, h), lambda i, j: (i, j, 0)),
        out_shape=jax.ShapeDtypeStruct((b, s, h), F32),
        scratch_shapes=[pltpu.VMEM((1, h), F32)],
        compiler_params=_cparams(("parallel", "arbitrary")))(logf)


def _fox_body(q_ref, k_ref, vt_ref, cq_ref, ck_ref, o_ref, ckb_ref, acc_ref, m_ref, l_ref, ot_ref,
              s_ref, p_ref, *, tq):
    i = pl.program_id(1)

    @pl.when(i == 0)
    def _():
        ck = ck_ref[0]
        for h in range(H_A):
            ckb_ref[h] = jnp.broadcast_to(ck[:, h:h + 1], (ck.shape[0], LANES))

    acc_ref[...] = jnp.zeros_like(acc_ref)
    l_ref[...] = jnp.zeros_like(l_ref)
    m_ref[...] = jnp.full(m_ref.shape, NEG, F32)
    cq = cq_ref[0]
    ones = jnp.ones((16, tq), BF16)

    def step(off, diag):
        if diag:
            krow = lax.broadcasted_iota(I32, (tq, tq), 0)
            qcol = lax.broadcasted_iota(I32, (tq, tq), 1)
            causal = krow <= qcol

        def qk(h):
            grp = slice((h // 2) * LANES, (h // 2 + 1) * LANES)
            k = k_ref[0, pl.ds(off, tq), grp]
            return _dot_nt(k, q_ref[:, h * LANES:(h + 1) * LANES])

        def pv_update(h, alpha):
            vt = jnp.concatenate([vt_ref[0, h * DH:(h + 1) * DH, pl.ds(off, tq)], ones], axis=0)
            pv = _dot(vt, p_ref[h % PIPE_SLOTS])
            acc_ref[h] = alpha * acc_ref[h] + pv[:DH, :]
            l_ref[h] = alpha * l_ref[h] + pv[DH:DH + 1, :]

        ahead, lag, slots = PIPE_AHEAD, PIPE_LAG, PIPE_SLOTS
        for h in range(ahead):
            s_ref[h % slots] = qk(h)
        pending = []
        for h in range(H_A):
            if h + ahead < H_A:
                s_ref[(h + ahead) % slots] = qk(h + ahead)
            ckb = ckb_ref[h, pl.ds(off, tq), :]
            s = s_ref[h % slots] + cq[h:h + 1, :] - _tile_lanes(ckb, tq // LANES)
            if diag:
                s = jnp.where(causal, s, NEG)
            m_old = m_ref[h]
            m_new = jnp.maximum(m_old, jnp.max(s, axis=0, keepdims=True))
            p_ref[h % slots] = jnp.exp(s - m_new).astype(BF16)
            m_ref[h] = m_new
            pending.append((h, jnp.exp(m_old - m_new)))
            if len(pending) > lag:
                pv_update(*pending.pop(0))
        for item in pending:
            pv_update(*item)

    def body(c2, carry):
        step(pl.multiple_of(2 * c2 * tq, tq), False)
        step(pl.multiple_of((2 * c2 + 1) * tq, tq), False)
        return carry

    lax.fori_loop(0, i // 2, body, 0)

    @pl.when(i % 2 == 1)
    def _():
        step(pl.multiple_of((i - 1) * tq, tq), False)

    step(pl.multiple_of(i * tq, tq), True)
    for h in range(H_A):
        ot_ref[h * DH:(h + 1) * DH, :] = acc_ref[h] / l_ref[h]
    o_ref[...] = ot_ref[...].T


def _fox_prompt(q_pad, k_bf, vt, cum, cum_t, b, s, tq):
    nq = s // tq
    k3 = k_bf.reshape(b, s, W_A)
    return pl.pallas_call(
        functools.partial(_fox_body, tq=tq), grid=(b, nq), name="fox_prompt",
        in_specs=[pl.BlockSpec((tq, PADW), lambda i, j: (i * nq + j, 0)),
                  pl.BlockSpec((1, s, W_A), lambda i, j: (i, 0, 0)),
                  pl.BlockSpec((1, W_A, s), lambda i, j: (i, 0, 0)),
                  pl.BlockSpec((1, H_A, tq), lambda i, j: (i, 0, j)),
                  pl.BlockSpec((1, s, H_A), lambda i, j: (i, 0, 0))],
        out_specs=pl.BlockSpec((tq, W_A), lambda i, j: (i * nq + j, 0)),
        out_shape=jax.ShapeDtypeStruct((b * s, W_A), F32),
        scratch_shapes=[pltpu.VMEM((H_A, s, LANES), F32), pltpu.VMEM((H_A, DH, tq), F32),
                        pltpu.VMEM((H_A, 1, tq), F32), pltpu.VMEM((H_A, 1, tq), F32),
                        pltpu.VMEM((W_A, tq), F32), pltpu.VMEM((PIPE_SLOTS, tq, tq), F32),
                        pltpu.VMEM((PIPE_SLOTS, tq, tq), BF16)],
        compiler_params=_cparams(("parallel", "arbitrary")))(q_pad, k3, vt, cum_t, cum)


def _key_to_f32(key):
    bits = key ^ ((key >> 31) & 0x7FFFFFFF)
    return lax.bitcast_convert_type(bits, F32)


def _kth_largest(count_ge, shape, n_top):
    kf = jnp.float32(n_top)
    zero = jnp.zeros(shape, I32)
    t0 = jnp.where(count_ge(_key_to_f32(zero)) >= kf, zero, jnp.full(shape, INT_MIN, I32))

    def bit_step(it, t):
        cand = t + jnp.left_shift(jnp.int32(1), 30 - it)
        return jnp.where(count_ge(_key_to_f32(cand)) >= kf, cand, t)

    key = lax.fori_loop(0, 31, bit_step, t0)
    few = key == INT_MIN
    return jnp.where(few, -jnp.inf, _key_to_f32(key)), few


def _tie_cut(count_eq_below, shape, need, n_bits):
    def bit_step(it, j):
        cand = j + jnp.left_shift(jnp.int32(1), n_bits - 1 - it)
        return jnp.where(count_eq_below(cand) < need, cand, j)

    return lax.fori_loop(0, n_bits, bit_step, jnp.zeros(shape, I32))


def _dsa_body(qb_ref, qi_ref, wi_ref, ki_ref, k_ref, vt_ref, o_ref,
              sc_ref, bias_ref, acc_ref, m_ref, l_ref, ot_ref, s_ref, p_ref, *, tq, s_len, n_top):
    i = pl.program_id(1)
    nk = i + 1
    kidx0 = lax.broadcasted_iota(I32, (tq, tq), 0)
    qidx = i * tq + lax.broadcasted_iota(I32, (tq, tq), 1)
    wi = wi_ref[0]

    def score_chunk(c, _):
        off = pl.multiple_of(c * tq, tq)
        ki = ki_ref[0, pl.ds(off, tq), :]
        sc = jnp.zeros((tq, tq), F32)
        for h in range(H_I):
            r = _dot_nt(ki, qi_ref[:, h * LANES:(h + 1) * LANES])
            sc = sc + wi[h:h + 1, :] * jnp.maximum(r, 0.0)
        sc_ref[pl.ds(off, tq), :] = jnp.where(kidx0 + off <= qidx, sc, -jnp.inf)
        return 0

    lax.fori_loop(0, nk, score_chunk, 0)

    def count(pred):
        def body(c, cnt):
            off = pl.multiple_of(c * tq, tq)
            hit = jnp.where(pred(sc_ref[pl.ds(off, tq), :], off), 1.0, 0.0)
            parts = [hit[r * 8:(r + 1) * 8, :] for r in range(tq // 8)]
            while len(parts) > 1:
                parts = [parts[a] + parts[a + 1] for a in range(0, len(parts), 2)]
            return cnt + parts[0]
        cnt = lax.fori_loop(0, nk, body, jnp.zeros((8, tq), F32))
        return jnp.sum(cnt, axis=0, keepdims=True)

    thr, few = _kth_largest(lambda cand: count(lambda x, off: x >= cand), (1, tq), n_top)
    need = jnp.where(few, 0.0, jnp.float32(n_top) - count(lambda x, off: x > thr))
    below = jnp.where(lax.broadcasted_iota(I32, (tq, tq), 1) < kidx0, 1.0, 0.0).astype(BF16)

    def bias_chunk(c, seen):
        off = pl.multiple_of(c * tq, tq)
        x = sc_ref[pl.ds(off, tq), :]
        tied = jnp.where(x == thr, 1.0, 0.0)
        rank = _dot(below, tied.astype(BF16)) + seen
        keep_tied = jnp.where(rank < need, 0.0, NEG)
        bias_ref[pl.ds(off, tq), :] = jnp.where(x > thr, 0.0, jnp.where(x == thr, keep_tied, NEG))
        return seen + jnp.sum(tied, axis=0, keepdims=True)

    lax.fori_loop(0, nk, bias_chunk, jnp.zeros((1, tq), F32))

    acc_ref[...] = jnp.zeros_like(acc_ref)
    l_ref[...] = jnp.zeros_like(l_ref)
    m_ref[...] = jnp.full(m_ref.shape, NEG, F32)
    ones = jnp.ones((16, tq), BF16)

    def attend(c, _):
        off = pl.multiple_of(c * tq, tq)
        k = k_ref[0, pl.ds(off, tq), :]
        vt = jnp.concatenate([vt_ref[0, :, pl.ds(off, tq)], ones], axis=0)
        bias = bias_ref[pl.ds(off, tq), :]
        def qk(h):
            return _dot_nt(k, qb_ref[:, h * LANES:(h + 1) * LANES])

        def pv_update(h, alpha):
            n = h // G_B
            pv = _dot(vt, p_ref[h % PIPE_SLOTS])
            acc_ref[h] = alpha * acc_ref[h] + pv[n * DH:(n + 1) * DH, :]
            l_ref[h] = alpha * l_ref[h] + pv[LANES:LANES + 1, :]

        ahead, lag, slots = PIPE_AHEAD, PIPE_LAG, PIPE_SLOTS
        for h in range(ahead):
            s_ref[h % slots] = qk(h) + bias
        pending = []
        for h in range(H_B):
            if h + ahead < H_B:
                s_ref[(h + ahead) % slots] = qk(h + ahead) + bias
            m_old = m_ref[h]
            m_new = jnp.maximum(m_old, jnp.max(s_ref[h % slots], axis=0, keepdims=True))
            alpha = jnp.exp(m_old - m_new)
            p_ref[h % slots] = jnp.exp(s_ref[h % slots] - m_new).astype(BF16)
            m_ref[h] = m_new
            pending.append((h, alpha))
            if len(pending) > lag:
                pv_update(*pending.pop(0))
        for item in pending:
            pv_update(*item)
        return 0

    def attend_pair(c2, _):
        attend(2 * c2, 0)
        attend(2 * c2 + 1, 0)
        return 0

    lax.fori_loop(0, nk // 2, attend_pair, 0)

    @pl.when(nk % 2 == 1)
    def _():
        attend(nk - 1, 0)
    for h in range(H_B):
        ot_ref[h * DH:(h + 1) * DH, :] = acc_ref[h] / l_ref[h]
    o_ref[...] = ot_ref[...].T


def _dsa_prompt(qb_pad, qi_pad, wi_t, ki_pad, kvb_bf, vt, b, s, tq):
    nq = s // tq
    n_top = min(TOPK_MAX, s // 4)
    ki3 = ki_pad.reshape(b, s, LANES)
    kv3 = kvb_bf.reshape(b, s, 2 * LANES)
    rows = lambda i, j: (i * nq + j, 0)
    return pl.pallas_call(
        functools.partial(_dsa_body, tq=tq, s_len=s, n_top=n_top), grid=(b, nq), name="dsa_prompt",
        in_specs=[pl.BlockSpec((tq, PADW), rows), pl.BlockSpec((tq, PADW), rows),
                  pl.BlockSpec((1, H_I, tq), lambda i, j: (i, 0, j)),
                  pl.BlockSpec((1, s, LANES), lambda i, j: (i, 0, 0)),
                  pl.BlockSpec((1, s, LANES), lambda i, j: (i, 0, 0)),
                  pl.BlockSpec((1, LANES, s), lambda i, j: (i, 0, 0))],
        out_specs=pl.BlockSpec((tq, W_B), rows),
        out_shape=jax.ShapeDtypeStruct((b * s, W_B), F32),
        scratch_shapes=[pltpu.VMEM((s, tq), F32), pltpu.VMEM((s, tq), F32),
                        pltpu.VMEM((H_B, DH, tq), F32), pltpu.VMEM((H_B, 1, tq), F32),
                        pltpu.VMEM((H_B, 1, tq), F32), pltpu.VMEM((W_B, tq), F32),
                        pltpu.VMEM((PIPE_SLOTS, tq, tq), F32),
                        pltpu.VMEM((PIPE_SLOTS, tq, tq), BF16)],
        compiler_params=_cparams(("parallel", "arbitrary")))(qb_pad, qi_pad, wi_t, ki3, kv3, vt)


HALO = 32


def _conv_ln_silu(acc, g, b):
    mu = jnp.mean(acc, axis=1, keepdims=True)
    d = acc - mu
    var = jnp.mean(d * d, axis=1, keepdims=True)
    y = d * lax.rsqrt(var + LN_EPS) * g + b
    return y * _sigmoid(y)


def _conv_body(u_ref, halo_ref, w_ref, cb_ref, g_ref, b_ref, o_ref, ext_ref, sh_ref, *, tq):
    j = pl.program_id(1)
    ext_ref[0:HALO, :] = jnp.where(j > 0, halo_ref[...], 0.0)
    ext_ref[HALO:, :] = u_ref[...]
    base = HALO - (CONV_W - 1)
    for r in range(1, 8):
        sh_ref[r - 1] = ext_ref[r:r + tq + HALO - 8, :]
    accs = [jnp.zeros((tq, D_C), F32) + cb_ref[...], jnp.zeros((tq, D_C), F32)]
    for t in range(CONV_W):
        r, a = (base + t) % 8, (base + t) // 8
        if r == 0:
            x = ext_ref[8 * a:8 * a + tq, :]
        else:
            x = sh_ref[r - 1, 8 * a:8 * a + tq, :]
        accs[t % 2] = accs[t % 2] + x * w_ref[t:t + 1, :]
    o_ref[...] = _conv_ln_silu(accs[0] + accs[1], g_ref[...], b_ref[...])


def _conv_prompt(u, conv_w, conv_b, g, bb, b, s, tq):
    nq = s // tq
    hb = tq // HALO
    rows = lambda i, j: (i * nq + j, 0)
    return pl.pallas_call(
        functools.partial(_conv_body, tq=tq), grid=(b, nq), name="conv_prompt",
        in_specs=[pl.BlockSpec((tq, D_C), rows),
                  pl.BlockSpec((HALO, D_C), lambda i, j: (jnp.maximum((i * nq + j) * hb - 1, 0), 0)),
                  pl.BlockSpec((CONV_W, D_C), lambda i, j: (0, 0)),
                  pl.BlockSpec((1, D_C), lambda i, j: (0, 0)),
                  pl.BlockSpec((1, D_C), lambda i, j: (0, 0)),
                  pl.BlockSpec((1, D_C), lambda i, j: (0, 0))],
        out_specs=pl.BlockSpec((tq, D_C), rows),
        out_shape=jax.ShapeDtypeStruct((b * s, D_C), F32),
        scratch_shapes=[pltpu.VMEM((HALO + tq, D_C), F32),
                        pltpu.VMEM((7, tq + HALO - 8, D_C), F32)],
        compiler_params=_cparams(("parallel", "arbitrary")))(u, u, conv_w, conv_b, g, bb)


def _merge_body(x_ref, p_ref, ya_ref, yb_ref, yc_ref, sg_ref, wmg_ref, wa_ref, wb_ref, wc_ref,
                wo_ref, wpe_ref, wpg_ref, g_ref, b_ref, y_ref, ybf_ref, *, alpha):
    x = x_ref[...]
    gates = _sigmoid(_dot(x.astype(BF16), wmg_ref[...]))
    sg = sg_ref[...]
    ua = _dot((ya_ref[...] * sg[:, :W_A]).astype(BF16), wa_ref[...])
    ub = _dot((yb_ref[...] * sg[:, W_A:2 * W_A]).astype(BF16), wb_ref[...])
    uc = _dot((yc_ref[...] * sg[:, 2 * W_A:]).astype(BF16), wc_ref[...])
    m = (gates[:, :D_MODEL] * ua + gates[:, D_MODEL:2 * D_MODEL] * ub
         + gates[:, 2 * D_MODEL:] * uc)
    r = alpha * x + _dot(m.astype(BF16), wo_ref[...])
    mu = jnp.mean(r, axis=1, keepdims=True)
    d = r - mu
    var = jnp.mean(d * d, axis=1, keepdims=True)
    h = d * lax.rsqrt(var + LN_EPS) * g_ref[...] + b_ref[...]
    pe = _dot(p_ref[...].astype(BF16), wpe_ref[...])
    y = h + _sigmoid(_dot(h.astype(BF16), wpg_ref[...])) * pe
    y_ref[...] = y
    ybf_ref[...] = y.astype(BF16)


def _merge(x, p, ya, yb, yc, sg, wl, tm, alpha):
    m = x.shape[0]
    ws = [wl['w_mg'], wl['w_a'], wl['w_b'], wl['w_c'], wl['w_out'], wl['w_pe'], wl['w_pe_gate'],
          wl['ln_g'], wl['ln_b']]
    return pl.pallas_call(
        functools.partial(_merge_body, alpha=alpha), grid=(m // tm,), name="merge",
        in_specs=[_row_spec(tm, D_MODEL), _row_spec(tm, P_DIM), _row_spec(tm, W_A),
                  _row_spec(tm, W_B), _row_spec(tm, D_C), _row_spec(tm, 3 * W_A)]
                 + [_const_spec(w.shape) for w in ws],
        out_specs=[_row_spec(tm, D_MODEL), _row_spec(tm, D_MODEL)],
        out_shape=[jax.ShapeDtypeStruct((m, D_MODEL), F32), jax.ShapeDtypeStruct((m, D_MODEL), BF16)],
        compiler_params=_cparams(("parallel",)))(x, p, ya, yb, yc, sg, *ws)


def _fox_dec_body(pt_ref, q_ref, kvn_ref, lfn_ref, *refs, g, layer):
    kv_refs = refs[:g]
    lf_refs = refs[g:2 * g]
    o_ref = refs[2 * g]
    m_ref, l_ref, acc_ref, carry_ref = refs[2 * g + 1:]
    t = pl.program_id(1)
    qrows = q_ref[0]

    @pl.when(t == 0)
    def _():
        kn = kvn_ref[0][:, :W_A].astype(BF16).astype(F32)
        vn = kvn_ref[0][:, W_A:]
        s_new = jnp.sum(qrows.astype(F32) * kn, axis=1, keepdims=True)
        m_ref[...] = s_new
        l_ref[...] = jnp.ones_like(l_ref)
        acc_ref[...] = jnp.broadcast_to(vn.astype(BF16).astype(F32), (H_A, W_A))
        carry_ref[...] = lfn_ref[0]

    r = lax.broadcasted_iota(I32, (PAGE, PAGE), 0)
    c = lax.broadcasted_iota(I32, (PAGE, PAGE), 1)
    upper = jnp.where(r > c, 1.0, 0.0).astype(BF16)
    lf = jnp.concatenate([lf_refs[gi][0, 0] for gi in range(g)], axis=0)
    suf = _dot3(lf, upper)
    tot = jnp.sum(lf, axis=1, keepdims=True)
    carry = carry_ref[...]
    parts = []
    for gi in range(g):
        kt = kv_refs[gi][0, 0, :W_A, :].astype(BF16)
        bias = suf[gi * H_A:(gi + 1) * H_A, :] + carry
        carry = carry + tot[gi * H_A:(gi + 1) * H_A, :]
        parts.append(_dot(qrows, kt) + bias)
    carry_ref[...] = carry
    s = jnp.concatenate(parts, axis=1)
    m_old = m_ref[...]
    m_new = jnp.maximum(m_old, jnp.max(s, axis=1, keepdims=True))
    alpha = jnp.exp(m_old - m_new)
    p = jnp.exp(s - m_new)
    l_ref[...] = alpha * l_ref[...] + jnp.sum(p, axis=1, keepdims=True)
    pb = p.astype(BF16)
    acc = alpha * acc_ref[...]
    for gi in range(g):
        vt = kv_refs[gi][0, 0, W_A:, :].astype(BF16)
        acc = acc + _dot_nt(pb[:, gi * PAGE:(gi + 1) * PAGE], vt)
    acc_ref[...] = acc
    m_ref[...] = m_new

    @pl.when(t == pl.num_programs(1) - 1)
    def _():
        o_ref[0] = acc_ref[...] / l_ref[...]


def _fox_decode(qrows, kva, logf, page_table, kv_t, lf_t, g, layer):
    bd, n_pages = page_table.shape
    steps = n_pages // g

    def page_map(gi):
        return lambda b, t, pt: (layer, pt[b, n_pages - 1 - (t * g + gi)], 0, 0)

    per_b = lambda b, t, pt: (b, 0, 0)
    out = pl.pallas_call(
        functools.partial(_fox_dec_body, g=g, layer=layer), name="fox_decode",
        grid_spec=pltpu.PrefetchScalarGridSpec(
            num_scalar_prefetch=1, grid=(bd, steps),
            in_specs=[pl.BlockSpec((1, H_A, W_A), per_b), pl.BlockSpec((1, 1, 2 * W_A), per_b),
                      pl.BlockSpec((1, H_A, 1), per_b)]
                     + [pl.BlockSpec((1, 1, 2 * W_A, PAGE), page_map(gi)) for gi in range(g)]
                     + [pl.BlockSpec((1, 1, H_A, PAGE), page_map(gi)) for gi in range(g)],
            out_specs=pl.BlockSpec((1, H_A, W_A), per_b),
            scratch_shapes=[pltpu.VMEM((H_A, 1), F32), pltpu.VMEM((H_A, 1), F32),
                            pltpu.VMEM((H_A, W_A), F32), pltpu.VMEM((H_A, 1), F32)]),
        out_shape=jax.ShapeDtypeStruct((bd, H_A, W_A), F32),
        compiler_params=_cparams(("parallel", "arbitrary")))(
            page_table, qrows, kva.reshape(bd, 1, 2 * W_A), logf.reshape(bd, H_A, 1),
            *([kv_t] * g), *([lf_t] * g))
    o4 = out.reshape(bd, H_A, H_A, DH)
    return jnp.stack([o4[:, h, h] for h in range(H_A)], axis=1).reshape(bd, W_A)


def _idx_dec_body(pt_ref, qi_ref, wi_ref, kin_ref, *refs, g):
    ki_refs = refs[:g]
    o_ref, on_ref = refs[g:]
    qi = qi_ref[0]
    w = wi_ref[0]
    rows = []
    for gi in range(g):
        r = _dot(qi, ki_refs[gi][0, 0].astype(BF16))
        rows.append(jnp.sum(w * jnp.maximum(r, 0.0), axis=0, keepdims=True))
    o_ref[0] = jnp.concatenate(rows, axis=0)
    r_new = jnp.sum(qi.astype(F32) * kin_ref[0].astype(F32), axis=1, keepdims=True)
    sc_new = jnp.sum(w * jnp.maximum(r_new, 0.0), axis=0, keepdims=True)
    on_ref[0] = jnp.broadcast_to(sc_new, (1, LANES))


def _idx_decode(qi_bf, wi, ki_new_bf, page_table, ik_t, g, layer):
    bd, n_pages = page_table.shape
    steps = n_pages // g

    def page_map(gi):
        return lambda b, t, pt: (layer, pt[b, t * g + gi], 0, 0)

    per_b = lambda b, t, pt: (b, 0, 0)
    sc, sc_new = pl.pallas_call(
        functools.partial(_idx_dec_body, g=g), name="idx_decode",
        grid_spec=pltpu.PrefetchScalarGridSpec(
            num_scalar_prefetch=1, grid=(bd, steps),
            in_specs=[pl.BlockSpec((1, H_I, D_I), per_b), pl.BlockSpec((1, H_I, 1), per_b),
                      pl.BlockSpec((1, 1, D_I), per_b)]
                     + [pl.BlockSpec((1, 1, D_I, PAGE), page_map(gi)) for gi in range(g)],
            out_specs=[pl.BlockSpec((1, g, PAGE), lambda b, t, pt: (b, t, 0)),
                       pl.BlockSpec((1, 1, LANES), per_b)]),
        out_shape=[jax.ShapeDtypeStruct((bd, n_pages, PAGE), F32),
                   jax.ShapeDtypeStruct((bd, 1, LANES), F32)],
        compiler_params=_cparams(("parallel", "arbitrary")))(
            page_table, qi_bf, wi.reshape(bd, H_I, 1), ki_new_bf.reshape(bd, 1, D_I),
            *([ik_t] * g))
    return sc.reshape(bd, n_pages * PAGE), sc_new[:, 0, :1]


def _sel_dec_body(sc_ref, scn_ref, bias_ref, biasn_ref, *, n_top, n_bits):
    bd, p_len = sc_ref.shape
    sc = sc_ref[...]
    sc_n = scn_ref[...]
    idx = lax.broadcasted_iota(I32, (bd, p_len), 1)

    def count(pred, pred_n):
        return (jnp.sum(jnp.where(pred, 1.0, 0.0), axis=1, keepdims=True)
                + jnp.where(pred_n, 1.0, 0.0))

    thr, _ = _kth_largest(lambda cand: count(sc >= cand, sc_n >= cand), (bd, 1), n_top)
    need = jnp.float32(n_top) - count(sc > thr, sc_n > thr)
    cut = _tie_cut(lambda cand: jnp.sum(jnp.where((sc == thr) & (idx < cand), 1.0, 0.0),
                                        axis=1, keepdims=True), (bd, 1), need, n_bits)
    eq_past = jnp.sum(jnp.where(sc == thr, 1.0, 0.0), axis=1, keepdims=True)
    sel = (sc > thr) | ((sc == thr) & (idx <= cut))
    bias_ref[...] = jnp.where(sel, 0.0, NEG)
    sel_n = (sc_n > thr) | ((sc_n == thr) & (eq_past < need))
    biasn_ref[...] = jnp.where(sel_n, 0.0, NEG)


def _sel_decode(sc, sc_new, n_top):
    bd, p_len = sc.shape
    n_bits = int(np.log2(p_len))
    return pl.pallas_call(
        functools.partial(_sel_dec_body, n_top=n_top, n_bits=n_bits), name="sel_decode",
        out_shape=[jax.ShapeDtypeStruct((bd, p_len), F32), jax.ShapeDtypeStruct((bd, 1), F32)],
        compiler_params=pltpu.CompilerParams(vmem_limit_bytes=VMEM_LIMIT))(sc, sc_new)


def _dsa_dec_body(pt_ref, q_ref, kvn_ref, bn_ref, bias_ref, *refs, g):
    kv_refs = refs[:g]
    o_ref = refs[g]
    m_ref, l_ref, acc_ref = refs[g + 1:]
    t = pl.program_id(1)
    qrows = q_ref[0]

    @pl.when(t == 0)
    def _():
        kn = kvn_ref[0][:, :LANES].astype(BF16).astype(F32)
        vn = kvn_ref[0][:, LANES:].astype(BF16).astype(F32)
        s_new = jnp.sum(qrows.astype(F32) * kn, axis=1, keepdims=True) + bn_ref[0]
        m_ref[...] = s_new
        l_ref[...] = jnp.where(s_new > 0.5 * NEG, 1.0, 0.0)
        acc_ref[...] = jnp.where(s_new > 0.5 * NEG, jnp.broadcast_to(vn, (H_B, LANES)), 0.0)

    parts = []
    for gi in range(g):
        kt = kv_refs[gi][0, 0, :LANES, :].astype(BF16)
        parts.append(_dot(qrows, kt) + bias_ref[0, gi:gi + 1, :])
    s = jnp.concatenate(parts, axis=1)
    m_old = m_ref[...]
    m_new = jnp.maximum(m_old, jnp.max(s, axis=1, keepdims=True))
    alpha = jnp.exp(m_old - m_new)
    p = jnp.where(s > 0.5 * NEG, jnp.exp(s - m_new), 0.0)
    l_ref[...] = alpha * l_ref[...] + jnp.sum(p, axis=1, keepdims=True)
    pb = p.astype(BF16)
    acc = alpha * acc_ref[...]
    for gi in range(g):
        vt = kv_refs[gi][0, 0, LANES:, :].astype(BF16)
        acc = acc + _dot_nt(pb[:, gi * PAGE:(gi + 1) * PAGE], vt)
    acc_ref[...] = acc
    m_ref[...] = m_new

    @pl.when(t == pl.num_programs(1) - 1)
    def _():
        o_ref[0] = acc_ref[...] / l_ref[...]


def _dsa_decode(q2, kvb, bias, bias_new, page_table, kv_t, g, layer):
    bd, n_pages = page_table.shape
    steps = n_pages // g

    def page_map(gi):
        return lambda b, t, pt: (layer, pt[b, t * g + gi], 0, 0)

    per_b = lambda b, t, pt: (b, 0, 0)
    out = pl.pallas_call(
        functools.partial(_dsa_dec_body, g=g), name="dsa_decode",
        grid_spec=pltpu.PrefetchScalarGridSpec(
            num_scalar_prefetch=1, grid=(bd, steps),
            in_specs=[pl.BlockSpec((1, H_B, LANES), per_b), pl.BlockSpec((1, 1, 2 * LANES), per_b),
                      pl.BlockSpec((1, 1, 1), per_b),
                      pl.BlockSpec((1, g, PAGE), lambda b, t, pt: (b, t, 0))]
                     + [pl.BlockSpec((1, 1, 2 * LANES, PAGE), page_map(gi)) for gi in range(g)],
            out_specs=pl.BlockSpec((1, H_B, LANES), per_b),
            scratch_shapes=[pltpu.VMEM((H_B, 1), F32), pltpu.VMEM((H_B, 1), F32),
                            pltpu.VMEM((H_B, LANES), F32)]),
        out_shape=jax.ShapeDtypeStruct((bd, H_B, LANES), F32),
        compiler_params=_cparams(("parallel", "arbitrary")))(
            page_table, q2, kvb.reshape(bd, 1, 2 * LANES), bias_new.reshape(bd, 1, 1),
            bias.reshape(bd, n_pages, PAGE), *([kv_t] * g))
    o4 = out.reshape(bd, KV_B, G_B, KV_B, DH)
    return jnp.stack([o4[:, n, :, n] for n in range(KV_B)], axis=1).reshape(bd, W_B)


def _conv_dec_body(st_ref, u_ref, w_ref, cb_ref, g_ref, b_ref, o_ref):
    acc = u_ref[...] * w_ref[CONV_W - 1:CONV_W, :] + cb_ref[...]
    for t in range(CONV_W - 1):
        acc = acc + st_ref[0, t] * w_ref[t:t + 1, :]
    o_ref[...] = _conv_ln_silu(acc, g_ref[...], b_ref[...])


def _conv_decode(state_t, u, conv_w, conv_b, g, bb, layer):
    bd = u.shape[0]
    full = lambda a: pl.BlockSpec(a.shape, lambda i: (0,) * a.ndim)
    return pl.pallas_call(
        _conv_dec_body, grid=(1,), name="conv_decode",
        in_specs=[pl.BlockSpec((1,) + state_t.shape[1:], lambda i: (layer, 0, 0, 0)),
                  full(u), full(conv_w), full(conv_b), full(g), full(bb)],
        out_specs=pl.BlockSpec((bd, D_C), lambda i: (0, 0)),
        out_shape=jax.ShapeDtypeStruct((bd, D_C), F32),
        compiler_params=_cparams(("arbitrary",)))(state_t, u, conv_w, conv_b, g, bb)


def _rope_tables(pos):
    half = ROT_DIM // 2
    inv = ROPE_THETA ** (-jnp.arange(half, dtype=F32) / half)
    ang = pos.astype(F32)[:, None] * inv[None, :]
    cos, sin = jnp.cos(ang), jnp.sin(ang)
    n = pos.shape[0]
    pad = jnp.zeros((n, DH - ROT_DIM), F32)
    c = jnp.concatenate([cos, cos, pad + 1.0], axis=1)
    s1 = jnp.concatenate([jnp.zeros_like(sin), sin, pad], axis=1)
    s2 = jnp.concatenate([-sin, jnp.zeros_like(sin), pad], axis=1)
    return tuple(jnp.tile(t, (1, LANES // DH)) for t in (c, s1, s2))


def _unspread(q_pad, half_of_head):
    q = q_pad.reshape(q_pad.shape[0], H_A, 2, DH)
    return jnp.stack([q[:, h, half_of_head(h)] for h in range(H_A)], axis=1)


def _regroup_weights(w_in, b_f):
    w = w_in.astype(BF16)
    d = w.shape[0]
    z = lambda n: jnp.zeros((d, n), BF16)
    foxq = jnp.concatenate([w[:, O_QA:O_KA], w[:, O_FA:O_GA], z(LANES - H_A)], axis=1)
    foxkv = w[:, O_KA:O_FA]
    gates = jnp.concatenate([w[:, O_GA:O_QB], w[:, O_GB:O_GLU], w[:, O_GC:O_MG]], axis=1)
    wqb = w[:, O_QB:O_KB].reshape(d, KV_B, G_B, DH)
    wqb = jnp.swapaxes(wqb, 1, 2).reshape(d, W_B)
    dsaq = jnp.concatenate([wqb, w[:, O_QI:O_KI]], axis=1)
    dsak = jnp.concatenate([w[:, O_KB:O_QI], w[:, O_KI:O_GB], z(LANES - D_I - H_I)], axis=1)
    glu = w[:, O_GLU:O_GC]
    wt = jnp.transpose(jnp.concatenate([w[:, O_VB:O_QI], w[:, O_KA:O_FA]], axis=1))
    foxk = w[:, O_KA:O_VA]
    bf = jnp.concatenate([b_f.astype(F32), jnp.zeros((LANES - H_A,), F32)])[None, :]
    return dict(foxq=foxq, foxkv=foxkv, foxk=foxk, gates=gates, dsaq=dsaq, dsak=dsak, glu=glu, t=wt,
                b_f=bf,
                w_mg=w[:, O_MG:O_END])


def _pick(n, prefs):
    for t in prefs:
        if n % t == 0:
            return t
    return n


def kernel(x_prompt, x_sample, cache_fox_kv, cache_fox_logf, cache_dsa_kv, cache_dsa_idx_k, state_conv,
           page_table, p_prompt, p_sample, w_in, b_f, w_a, w_b, w_c, w_out, ln_g, ln_b,
           conv_w, conv_b, conv_ln_g, conv_ln_b, w_pe, w_pe_gate):
    depth = w_in.shape[0]
    b, s, d = x_prompt.shape
    bd = x_sample.shape[0]
    n_phys = cache_fox_kv.shape[1]
    n_pages = page_table.shape[1]
    p_len = n_pages * PAGE
    alpha = float((2 * depth) ** 0.25)
    tm = _pick(s, (256, 128))
    tq = _pick(s, (256, 128))
    g_fox = _pick(n_pages, (16, 8, 4, 2, 1))
    g_dsa = _pick(n_pages, (32, 16, 8, 4, 2, 1))
    g_idx = _pick(n_pages, (64, 32, 16, 8, 4, 2, 1))
    n_top_dec = min(TOPK_MAX, (p_len + 1) // 4)

    tabs_p = _rope_tables(jnp.arange(s, dtype=I32))
    tabs_s = _rope_tables(jnp.full((bd,), p_len, I32))
    page_table = page_table.astype(I32)

    fox_kv_t = jnp.transpose(cache_fox_kv, (0, 1, 3, 4, 5, 2)).reshape(depth, n_phys, 2 * W_A, PAGE)
    fox_lf_t = jnp.transpose(cache_fox_logf, (0, 1, 3, 2))
    dsa_kv_t = jnp.transpose(cache_dsa_kv, (0, 1, 3, 4, 5, 2)).reshape(depth, n_phys, 2 * LANES, PAGE)
    dsa_ik_t = jnp.transpose(cache_dsa_idx_k, (0, 1, 3, 2))
    state_t = jnp.transpose(state_conv, (0, 2, 1, 3))
    head_eye = jnp.repeat(jnp.eye(H_A, dtype=BF16), DH, axis=1)

    hp = x_prompt.reshape(b * s, d)
    hs = x_sample.reshape(bd, d)
    hp_bf = hp.astype(BF16)
    hs_bf = hs.astype(BF16)
    outs_p, outs_s = [], []
    kvat = jnp.zeros((depth, b, 2 * W_A, s), F32)
    for l in range(depth):
        wg = _regroup_weights(w_in[l], b_f[l])
        wl = dict(w_mg=wg['w_mg'], w_a=w_a[l].astype(BF16), w_b=w_b[l].astype(BF16),
                  w_c=w_c[l].astype(BF16), w_out=w_out[l].astype(BF16), w_pe=w_pe[l].astype(BF16),
                  w_pe_gate=w_pe_gate[l].astype(BF16), ln_g=ln_g[l][None, :], ln_b=ln_b[l][None, :])
        cw, cb = conv_w[l], conv_b[l][None, :]
        cg, cbb = conv_ln_g[l][None, :], conv_ln_b[l][None, :]

        pp = _project(hp_bf, wg, tabs_p, tm, s // tm, stacked=(kvat, l, b, s))
        vbt, vat, kvat = pp['vbt'], pp['vat'], pp['kvat']
        logf3 = pp['logf'].reshape(b, s, H_A)
        cum = _cumsum(logf3, _pick(s, (1024, 512, 256, 128)))
        cum_t = jnp.swapaxes(cum, 1, 2)
        wi_t = jnp.swapaxes(pp['wi'].reshape(b, s, H_I), 1, 2)
        ya = _fox_prompt(pp['q_pad'], pp['k_bf'], vat, cum, cum_t, b, s, tq)
        yb = _dsa_prompt(pp['qb_pad'], pp['qi_pad'], wi_t, pp['ki_pad'], pp['kvb_bf'], vbt, b, s, tq)
        yc = _conv_prompt(pp['u'], cw, cb, cg, cbb, b, s, tq)
        hp_new, hp_bf = _merge(hp, p_prompt[l].reshape(b * s, P_DIM), ya, yb, yc, pp['sg'], wl,
                               _pick(b * s, (256, 128)), alpha)
        outs_p.append((None, logf3,
                       pp['kvb'].reshape(b, s, 2, KV_B, DH), pp['ki'].reshape(b, s, D_I),
                       pp['u'].reshape(b, s, D_C)[:, s - (CONV_W - 1):]))
        hp = hp_new

        ps = _project(hs_bf, wg, tabs_s, bd, 1)
        q_s = _unspread(ps['q_pad'], lambda h: h % 2).reshape(bd, 1, W_A)
        qrows = q_s * head_eye[None]
        ya_s = _fox_decode(qrows, ps['kva'], ps['logf'], page_table, fox_kv_t, fox_lf_t, g_fox, l)
        qi_s = _unspread(ps['qi_pad'], lambda h: h % 2)
        sc, sc_new = _idx_decode(qi_s, ps['wi'], ps['ki_pad'][:, :D_I], page_table, dsa_ik_t,
                                 g_idx, l)
        bias, bias_new = _sel_decode(sc, sc_new, n_top_dec)
        yb_s = _dsa_decode(ps['qb_pad'].reshape(bd, H_B, LANES), ps['kvb'], bias, bias_new,
                           page_table, dsa_kv_t, g_dsa, l)
        yc_s = _conv_decode(state_t, ps['u'], cw, cb, cg, cbb, l)
        hs_new, hs_bf = _merge(hs, p_sample[l].reshape(bd, P_DIM), ya_s, yb_s, yc_s, ps['sg'], wl,
                               bd, alpha)
        outs_s.append((ps['kva'].reshape(bd, 1, 2, H_A, DH), ps['logf'].reshape(bd, 1, H_A),
                       ps['kvb'].reshape(bd, 1, 2, KV_B, DH), ps['ki'].reshape(bd, 1, D_I),
                       jnp.concatenate([state_t[l, 1:], ps['u'][None]], axis=0)))
        hs = hs_new

    stack = lambda outs, k: jnp.stack([o[k] for o in outs])
    fox_kv_prompt = jnp.transpose(kvat.reshape(depth, b, 2, H_A, DH, s), (0, 1, 5, 2, 3, 4))
    return (hp.reshape(b, s, d), hs.reshape(bd, 1, d),
            fox_kv_prompt, stack(outs_s, 0), stack(outs_p, 1), stack(outs_s, 1),
            stack(outs_p, 2), stack(outs_s, 2), stack(outs_p, 3), stack(outs_s, 3),
            stack(outs_p, 4), jnp.transpose(stack(outs_s, 4), (0, 2, 1, 3)))
```

```python
import functools

import jax
import jax.numpy as jnp
import numpy as np
from jax import lax
from jax.experimental import pallas as pl
from jax.experimental.pallas import tpu as pltpu

F32 = jnp.float32
BF16 = jnp.bfloat16
I32 = jnp.int32

D_MODEL = 1024
H_A = 8
DH = 64
W_A = H_A * DH
KV_B = 2
H_B = 8
G_B = H_B // KV_B
W_B = H_B * DH
H_I = 8
D_I = 64
TOPK_MAX = 256
D_C = 512
CONV_W = 31
P_DIM = 256
N_BRANCH = 3
ROPE_THETA = 500000.0
ROT_DIM = DH // 4
LN_EPS = 1e-5
PAGE = 128
LANES = 128
PADW = H_A * LANES

PIPE_AHEAD = 3
PIPE_LAG = 2
PIPE_SLOTS = 8

NEG = -1e30
INT_MIN = -(2 ** 31)
VMEM_LIMIT = 56 * 1024 * 1024

_SPLIT = (W_A, W_A, W_A, H_A, W_A, W_B, KV_B * DH, KV_B * DH, H_I * D_I, D_I, H_I, W_B,
          2 * D_C, D_C, N_BRANCH * D_MODEL)
_OFF = np.concatenate([[0], np.cumsum(_SPLIT)]).tolist()
(O_QA, O_KA, O_VA, O_FA, O_GA, O_QB, O_KB, O_VB, O_QI, O_KI, O_WI, O_GB, O_GLU, O_GC, O_MG,
 O_END) = _OFF


def _cparams(sem):
    return pltpu.CompilerParams(dimension_semantics=sem, vmem_limit_bytes=VMEM_LIMIT)


def _dot(a, b):
    return jnp.dot(a, b, preferred_element_type=F32)


def _dot_nt(a, b):
    return lax.dot_general(a, b, (((1,), (1,)), ((), ())), preferred_element_type=F32)


def _sigmoid(x):
    return 1.0 / (1.0 + jnp.exp(-x))


def _split3(x):
    hi = x.astype(BF16)
    r1 = x - hi.astype(F32)
    mid = r1.astype(BF16)
    lo = (r1 - mid.astype(F32)).astype(BF16)
    return hi, mid, lo


def _dot3(x, w):
    hi, mid, lo = _split3(x)
    return _dot(hi, w) + _dot(mid, w) + _dot(lo, w)


def _rope(x, c, s1, s2):
    n = x.shape[-1]
    half = ROT_DIM // 2
    return x * c + pltpu.roll(x, half, 1) * s1 + pltpu.roll(x, n - half, 1) * s2


def _tile_lanes(t, reps):
    return t if reps == 1 else jnp.concatenate([t] * reps, axis=1)


def _spread_heads(y, heads_of_group):
    lane = lax.broadcasted_iota(I32, (y.shape[0], LANES), 1)
    out = [None] * (2 * len(heads_of_group))
    for g, (lo, hi) in enumerate(heads_of_group):
        yg = y[:, g * LANES:(g + 1) * LANES]
        out[lo] = jnp.where(lane < DH, yg, 0.0)
        out[hi] = jnp.where(lane >= DH, yg, 0.0)
    return jnp.concatenate(out, axis=1)


_PAIRS_ADJACENT = tuple((2 * g, 2 * g + 1) for g in range(H_A // 2))
_PAIRS_BY_KV = tuple((g, g + G_B) for g in range(G_B))


def _proj_foxq_body(x_ref, w_ref, bf_ref, q_ref, logf_ref):
    z = _dot(x_ref[...], w_ref[...])
    q_ref[...] = _spread_heads(z[:, :W_A] * (DH ** -0.5), _PAIRS_ADJACENT).astype(BF16)
    f = z[:, W_A:] + bf_ref[...]
    logf = jnp.minimum(f, 0.0) - jnp.log1p(jnp.exp(-jnp.abs(f)))
    logf_ref[...] = logf[:, :H_A]


def _proj_foxkv_body(x_ref, w_ref, k_ref, kv_ref):
    z = _dot(x_ref[...], w_ref[...])
    k_ref[...] = z[:, :W_A].astype(BF16)
    kv_ref[...] = z


def _proj_gates_body(x_ref, w_ref, o_ref):
    z = _dot(x_ref[...], w_ref[...])
    o_ref[...] = z * _sigmoid(z)


def _proj_dsaq_body(x_ref, w_ref, c_ref, s1_ref, s2_ref, qb_ref, qi_ref):
    z = _dot(x_ref[...], w_ref[...])
    reps = z.shape[1] // LANES
    y = _rope(z, _tile_lanes(c_ref[...], reps), _tile_lanes(s1_ref[...], reps),
              _tile_lanes(s2_ref[...], reps)) * (DH ** -0.5)
    qb_ref[...] = _spread_heads(y[:, :W_B], _PAIRS_BY_KV).astype(BF16)
    qi_ref[...] = _spread_heads(y[:, W_B:], _PAIRS_ADJACENT).astype(BF16)


def _proj_dsak_body(x_ref, w_ref, c_ref, s1_ref, s2_ref, kv_ref, kvbf_ref, ki_ref, kibf_ref, wi_ref):
    z = _dot(x_ref[...], w_ref[...])
    c, s1, s2 = c_ref[...], s1_ref[...], s2_ref[...]
    kb = _rope(z[:, :LANES], c, s1, s2)
    kv = jnp.concatenate([kb, z[:, LANES:2 * LANES]], axis=1)
    kv_ref[...] = kv
    kvbf_ref[...] = kv.astype(BF16)
    t = _rope(z[:, 2 * LANES:], c, s1, s2)
    ki_ref[...] = t[:, :D_I]
    lane = lax.broadcasted_iota(I32, t.shape, 1)
    kibf_ref[...] = jnp.where(lane < D_I, t, pltpu.roll(t, D_I, 1)).astype(BF16)
    wi_ref[...] = z[:, 2 * LANES + D_I:2 * LANES + D_I + H_I] * (H_I ** -0.5)


def _proj_glu_body(x_ref, w_ref, u_ref):
    z = _dot(x_ref[...], w_ref[...])
    u_ref[...] = z[:, :D_C] * _sigmoid(z[:, D_C:])


def _proj_t_body(x_ref, w_ref, vb_ref, kvt_ref, va_ref):
    zt = _dot_nt(w_ref[...], x_ref[...])
    vb_ref[0] = zt[:LANES].astype(BF16)
    kvt_ref[0, 0] = zt[LANES:]
    va_ref[0] = zt[LANES + W_A:].astype(BF16)


def _proj_foxk_body(x_ref, w_ref, k_ref):
    k_ref[...] = _dot(x_ref[...], w_ref[...]).astype(BF16)


def _row_spec(tm, n):
    return pl.BlockSpec((tm, n), lambda i: (i, 0))


def _const_spec(shape):
    return pl.BlockSpec(shape, lambda i: (0,) * len(shape), pipeline_mode=pl.Buffered(1))


_PROJ_ROW_OUTS = dict(q_pad=(PADW, BF16), logf=(H_A, F32), k_bf=(W_A, BF16), kva=(2 * W_A, F32),
                      sg=(3 * W_A, F32), qb_pad=(PADW, BF16), qi_pad=(PADW, BF16),
                      kvb=(2 * LANES, F32), kvb_bf=(2 * LANES, BF16), ki=(D_I, F32),
                      ki_pad=(LANES, BF16), wi=(H_I, F32), u=(D_C, F32))


def _proj_body(*refs, names, transposed):
    n_in = 11 + (2 if transposed else 0)
    (x_ref, wq_ref, wkv_ref, wg_ref, wdq_ref, wdk_ref, wglu_ref, bf_ref, c_ref, s1_ref,
     s2_ref) = refs[:11]
    o = dict(zip(names, refs[n_in:]))
    _proj_foxq_body(x_ref, wq_ref, bf_ref, o['q_pad'], o['logf'])
    if transposed:
        _proj_foxk_body(x_ref, wkv_ref, o['k_bf'])
        _proj_t_body(x_ref, refs[11], o['vbt'], o['kvat'], o['vat'])
    else:
        _proj_foxkv_body(x_ref, wkv_ref, o['k_bf'], o['kva'])
    _proj_gates_body(x_ref, wg_ref, o['sg'])
    _proj_dsaq_body(x_ref, wdq_ref, c_ref, s1_ref, s2_ref, o['qb_pad'], o['qi_pad'])
    _proj_dsak_body(x_ref, wdk_ref, c_ref, s1_ref, s2_ref, o['kvb'], o['kvb_bf'], o['ki'],
                    o['ki_pad'], o['wi'])
    _proj_glu_body(x_ref, wglu_ref, o['u'])


def _project(x_bf, wg, tabs, tm, tab_blocks, stacked=None):
    m = x_bf.shape[0]
    transposed = stacked is not None
    tab_spec = pl.BlockSpec((tm, LANES), lambda i: (i % tab_blocks, 0))
    names = [k for k in _PROJ_ROW_OUTS if not (transposed and k == 'kva')]
    ws = [wg['foxq'], wg['foxk'] if transposed else wg['foxkv'], wg['gates'], wg['dsaq'],
          wg['dsak'], wg['glu'], wg['b_f']]
    in_specs = ([_row_spec(tm, D_MODEL)] + [_const_spec(w.shape) for w in ws] + [tab_spec] * 3)
    args = [x_bf] + ws + list(tabs)
    out_specs = [_row_spec(tm, _PROJ_ROW_OUTS[k][0]) for k in names]
    out_shape = [jax.ShapeDtypeStruct((m,) + _PROJ_ROW_OUTS[k][:1], _PROJ_ROW_OUTS[k][1])
                 for k in names]
    aliases = {}
    if transposed:
        kvat, layer, b, s = stacked
        nt = s // tm
        in_specs += [_const_spec(wg['t'].shape), pl.BlockSpec(memory_space=pl.ANY)]
        args += [wg['t'], kvat]
        seq_block = lambda rows: pl.BlockSpec((1, rows, tm), lambda i: (i // nt, 0, i % nt))
        out_specs += [seq_block(LANES),
                      pl.BlockSpec((1, 1, 2 * W_A, tm), lambda i: (layer, i // nt, 0, i % nt)),
                      seq_block(W_A)]
        out_shape += [jax.ShapeDtypeStruct((b, LANES, s), BF16),
                      jax.ShapeDtypeStruct(kvat.shape, F32),
                      jax.ShapeDtypeStruct((b, W_A, s), BF16)]
        aliases = {len(args) - 1: len(names) + 1}
        names = names + ['vbt', 'kvat', 'vat']
    outs = pl.pallas_call(
        functools.partial(_proj_body, names=tuple(names), transposed=transposed),
        grid=(m // tm,), name="project", in_specs=in_specs, out_specs=out_specs,
        out_shape=out_shape, input_output_aliases=aliases,
        compiler_params=_cparams(("parallel",)))(*args)
    return dict(zip(names, outs))


def _cumsum_body(x_ref, o_ref, carry_ref):
    @pl.when(pl.program_id(1) == 0)
    def _():
        carry_ref[...] = jnp.zeros_like(carry_ref)

    t = x_ref.shape[1]
    r = lax.broadcasted_iota(I32, (t, t), 0)
    c = lax.broadcasted_iota(I32, (t, t), 1)
    tri = jnp.where(c <= r, 1.0, 0.0).astype(BF16)
    hi, mid, lo = _split3(x_ref[0])
    cum = _dot(tri, hi) + _dot(tri, mid) + _dot(tri, lo) + carry_ref[...]
    o_ref[0] = cum
    carry_ref[...] = cum[t - 1:t, :]


def _cumsum(logf, tb):
    b, s, h = logf.shape
    return pl.pallas_call(
        _cumsum_body, grid=(b, s // tb), name="cumsum",
        in_specs=[pl.BlockSpec((1, tb, h), lambda i, j: (i, j, 0))],
        out_specs=pl.BlockSpec((1, tb, h), lambda i, j: (i, j, 0)),
        out_shape=jax.ShapeDtypeStruct((b, s, h), F32),
        scratch_shapes=[pltpu.VMEM((1, h), F32)],
        compiler_params=_cparams(("parallel", "arbitrary")))(logf)


def _fox_body(q_ref, k_ref, vt_ref, cq_ref, ck_ref, o_ref, ckb_ref, acc_ref, m_ref, l_ref, ot_ref,
              s_ref, p_ref, *, tq):
    i = pl.program_id(1)

    @pl.when(i == 0)
    def _():
        ck = ck_ref[0]
        for h in range(H_A):
            ckb_ref[h] = jnp.broadcast_to(ck[:, h:h + 1], (ck.shape[0], LANES))

    acc_ref[...] = jnp.zeros_like(acc_ref)
    l_ref[...] = jnp.zeros_like(l_ref)
    m_ref[...] = jnp.full(m_ref.shape, NEG, F32)
    cq = cq_ref[0]
    ones = jnp.ones((16, tq), BF16)

    def step(off, diag):
        if diag:
            krow = lax.broadcasted_iota(I32, (tq, tq), 0)
            qcol = lax.broadcasted_iota(I32, (tq, tq), 1)
            causal = krow <= qcol

        def qk(h):
            grp = slice((h // 2) * LANES, (h // 2 + 1) * LANES)
            k = k_ref[0, pl.ds(off, tq), grp]
            return _dot_nt(k, q_ref[:, h * LANES:(h + 1) * LANES])

        def pv_update(h, alpha):
            vt = jnp.concatenate([vt_ref[0, h * DH:(h + 1) * DH, pl.ds(off, tq)], ones], axis=0)
            pv = _dot(vt, p_ref[h % PIPE_SLOTS])
            acc_ref[h] = alpha * acc_ref[h] + pv[:DH, :]
            l_ref[h] = alpha * l_ref[h] + pv[DH:DH + 1, :]

        ahead, lag, slots = PIPE_AHEAD, PIPE_LAG, PIPE_SLOTS
        for h in range(ahead):
            s_ref[h % slots] = qk(h)
        pending = []
        for h in range(H_A):
            if h + ahead < H_A:
                s_ref[(h + ahead) % slots] = qk(h + ahead)
            ckb = ckb_ref[h, pl.ds(off, tq), :]
            s = s_ref[h % slots] + cq[h:h + 1, :] - _tile_lanes(ckb, tq // LANES)
            if diag:
                s = jnp.where(causal, s, NEG)
            m_old = m_ref[h]
            m_new = jnp.maximum(m_old, jnp.max(s, axis=0, keepdims=True))
            p_ref[h % slots] = jnp.exp(s - m_new).astype(BF16)
            m_ref[h] = m_new
            pending.append((h, jnp.exp(m_old - m_new)))
            if len(pending) > lag:
                pv_update(*pending.pop(0))
        for item in pending:
            pv_update(*item)

    def body(c2, carry):
        step(pl.multiple_of(2 * c2 * tq, tq), False)
        step(pl.multiple_of((2 * c2 + 1) * tq, tq), False)
        return carry

    lax.fori_loop(0, i // 2, body, 0)

    @pl.when(i % 2 == 1)
    def _():
        step(pl.multiple_of((i - 1) * tq, tq), False)

    step(pl.multiple_of(i * tq, tq), True)
    for h in range(H_A):
        ot_ref[h * DH:(h + 1) * DH, :] = acc_ref[h] / l_ref[h]
    o_ref[...] = ot_ref[...].T


def _fox_prompt(q_pad, k_bf, vt, cum, cum_t, b, s, tq):
    nq = s // tq
    k3 = k_bf.reshape(b, s, W_A)
    return pl.pallas_call(
        functools.partial(_fox_body, tq=tq), grid=(b, nq), name="fox_prompt",
        in_specs=[pl.BlockSpec((tq, PADW), lambda i, j: (i * nq + j, 0)),
                  pl.BlockSpec((1, s, W_A), lambda i, j: (i, 0, 0)),
                  pl.BlockSpec((1, W_A, s), lambda i, j: (i, 0, 0)),
                  pl.BlockSpec((1, H_A, tq), lambda i, j: (i, 0, j)),
                  pl.BlockSpec((1, s, H_A), lambda i, j: (i, 0, 0))],
        out_specs=pl.BlockSpec((tq, W_A), lambda i, j: (i * nq + j, 0)),
        out_shape=jax.ShapeDtypeStruct((b * s, W_A), F32),
        scratch_shapes=[pltpu.VMEM((H_A, s, LANES), F32), pltpu.VMEM((H_A, DH, tq), F32),
                        pltpu.VMEM((H_A, 1, tq), F32), pltpu.VMEM((H_A, 1, tq), F32),
                        pltpu.VMEM((W_A, tq), F32), pltpu.VMEM((PIPE_SLOTS, tq, tq), F32),
                        pltpu.VMEM((PIPE_SLOTS, tq, tq), BF16)],
        compiler_params=_cparams(("parallel", "arbitrary")))(q_pad, k3, vt, cum_t, cum)


def _key_to_f32(key):
    bits = key ^ ((key >> 31) & 0x7FFFFFFF)
    return lax.bitcast_convert_type(bits, F32)


def _kth_largest(count_ge, shape, n_top):
    kf = jnp.float32(n_top)
    zero = jnp.zeros(shape, I32)
    t0 = jnp.where(count_ge(_key_to_f32(zero)) >= kf, zero, jnp.full(shape, INT_MIN, I32))

    def bit_step(it, t):
        cand = t + jnp.left_shift(jnp.int32(1), 30 - it)
        return jnp.where(count_ge(_key_to_f32(cand)) >= kf, cand, t)

    key = lax.fori_loop(0, 31, bit_step, t0)
    few = key == INT_MIN
    return jnp.where(few, -jnp.inf, _key_to_f32(key)), few


def _tie_cut(count_eq_below, shape, need, n_bits):
    def bit_step(it, j):
        cand = j + jnp.left_shift(jnp.int32(1), n_bits - 1 - it)
        return jnp.where(count_eq_below(cand) < need, cand, j)

    return lax.fori_loop(0, n_bits, bit_step, jnp.zeros(shape, I32))


def _dsa_body(qb_ref, qi_ref, wi_ref, ki_ref, k_ref, vt_ref, o_ref,
              sc_ref, bias_ref, acc_ref, m_ref, l_ref, ot_ref, s_ref, p_ref, *, tq, s_len, n_top):
    i = pl.program_id(1)
    nk = i + 1
    kidx0 = lax.broadcasted_iota(I32, (tq, tq), 0)
    qidx = i * tq + lax.broadcasted_iota(I32, (tq, tq), 1)
    wi = wi_ref[0]

    def score_chunk(c, _):
        off = pl.multiple_of(c * tq, tq)
        ki = ki_ref[0, pl.ds(off, tq), :]
        sc = jnp.zeros((tq, tq), F32)
        for h in range(H_I):
            r = _dot_nt(ki, qi_ref[:, h * LANES:(h + 1) * LANES])
            sc = sc + wi[h:h + 1, :] * jnp.maximum(r, 0.0)
        sc_ref[pl.ds(off, tq), :] = jnp.where(kidx0 + off <= qidx, sc, -jnp.inf)
        return 0

    def score_pair(c2, _):
        score_chunk(2 * c2, 0)
        score_chunk(2 * c2 + 1, 0)
        return 0

    lax.fori_loop(0, nk // 2, score_pair, 0)

    @pl.when(nk % 2 == 1)
    def _():
        score_chunk(nk - 1, 0)

    def count(pred):
        def body(c, cnt):
            off = pl.multiple_of(c * tq, tq)
            hit = jnp.where(pred(sc_ref[pl.ds(off, tq), :], off), 1.0, 0.0)
            parts = [hit[r * 8:(r + 1) * 8, :] for r in range(tq // 8)]
            while len(parts) > 1:
                parts = [parts[a] + parts[a + 1] for a in range(0, len(parts), 2)]
            return cnt + parts[0]
        cnt = lax.fori_loop(0, nk, body, jnp.zeros((8, tq), F32))
        return jnp.sum(cnt, axis=0, keepdims=True)

    thr, few = _kth_largest(lambda cand: count(lambda x, off: x >= cand), (1, tq), n_top)
    need = jnp.where(few, 0.0, jnp.float32(n_top) - count(lambda x, off: x > thr))
    below = jnp.where(lax.broadcasted_iota(I32, (tq, tq), 1) < kidx0, 1.0, 0.0).astype(BF16)

    def bias_chunk(c, seen):
        off = pl.multiple_of(c * tq, tq)
        x = sc_ref[pl.ds(off, tq), :]
        tied = jnp.where(x == thr, 1.0, 0.0)
        rank = _dot(below, tied.astype(BF16)) + seen
        keep_tied = jnp.where(rank < need, 0.0, NEG)
        bias_ref[pl.ds(off, tq), :] = jnp.where(x > thr, 0.0, jnp.where(x == thr, keep_tied, NEG))
        return seen + jnp.sum(tied, axis=0, keepdims=True)

    lax.fori_loop(0, nk, bias_chunk, jnp.zeros((1, tq), F32))

    acc_ref[...] = jnp.zeros_like(acc_ref)
    l_ref[...] = jnp.zeros_like(l_ref)
    m_ref[...] = jnp.full(m_ref.shape, NEG, F32)
    ones = jnp.ones((16, tq), BF16)

    def attend(c, _):
        off = pl.multiple_of(c * tq, tq)
        k = k_ref[0, pl.ds(off, tq), :]
        vt = jnp.concatenate([vt_ref[0, :, pl.ds(off, tq)], ones], axis=0)
        bias = bias_ref[pl.ds(off, tq), :]
        def qk(h):
            return _dot_nt(k, qb_ref[:, h * LANES:(h + 1) * LANES])

        def pv_update(h, alpha):
            n = h // G_B
            pv = _dot(vt, p_ref[h % PIPE_SLOTS])
            acc_ref[h] = alpha * acc_ref[h] + pv[n * DH:(n + 1) * DH, :]
            l_ref[h] = alpha * l_ref[h] + pv[LANES:LANES + 1, :]

        ahead, lag, slots = PIPE_AHEAD, PIPE_LAG, PIPE_SLOTS
        for h in range(ahead):
            s_ref[h % slots] = qk(h) + bias
        pending = []
        for h in range(H_B):
            if h + ahead < H_B:
                s_ref[(h + ahead) % slots] = qk(h + ahead) + bias
            m_old = m_ref[h]
            m_new = jnp.maximum(m_old, jnp.max(s_ref[h % slots], axis=0, keepdims=True))
            alpha = jnp.exp(m_old - m_new)
            p_ref[h % slots] = jnp.exp(s_ref[h % slots] - m_new).astype(BF16)
            m_ref[h] = m_new
            pending.append((h, alpha))
            if len(pending) > lag:
                pv_update(*pending.pop(0))
        for item in pending:
            pv_update(*item)
        return 0

    def attend_pair(c2, _):
        attend(2 * c2, 0)
        attend(2 * c2 + 1, 0)
        return 0

    lax.fori_loop(0, nk // 2, attend_pair, 0)

    @pl.when(nk % 2 == 1)
    def _():
        attend(nk - 1, 0)
    for h in range(H_B):
        ot_ref[h * DH:(h + 1) * DH, :] = acc_ref[h] / l_ref[h]
    o_ref[...] = ot_ref[...].T


def _dsa_prompt(qb_pad, qi_pad, wi_t, ki_pad, kvb_bf, vt, b, s, tq):
    nq = s // tq
    n_top = min(TOPK_MAX, s // 4)
    ki3 = ki_pad.reshape(b, s, LANES)
    kv3 = kvb_bf.reshape(b, s, 2 * LANES)
    rows = lambda i, j: (i * nq + j, 0)
    return pl.pallas_call(
        functools.partial(_dsa_body, tq=tq, s_len=s, n_top=n_top), grid=(b, nq), name="dsa_prompt",
        in_specs=[pl.BlockSpec((tq, PADW), rows), pl.BlockSpec((tq, PADW), rows),
                  pl.BlockSpec((1, H_I, tq), lambda i, j: (i, 0, j)),
                  pl.BlockSpec((1, s, LANES), lambda i, j: (i, 0, 0)),
                  pl.BlockSpec((1, s, LANES), lambda i, j: (i, 0, 0)),
                  pl.BlockSpec((1, LANES, s), lambda i, j: (i, 0, 0))],
        out_specs=pl.BlockSpec((tq, W_B), rows),
        out_shape=jax.ShapeDtypeStruct((b * s, W_B), F32),
        scratch_shapes=[pltpu.VMEM((s, tq), F32), pltpu.VMEM((s, tq), F32),
                        pltpu.VMEM((H_B, DH, tq), F32), pltpu.VMEM((H_B, 1, tq), F32),
                        pltpu.VMEM((H_B, 1, tq), F32), pltpu.VMEM((W_B, tq), F32),
                        pltpu.VMEM((PIPE_SLOTS, tq, tq), F32),
                        pltpu.VMEM((PIPE_SLOTS, tq, tq), BF16)],
        compiler_params=_cparams(("parallel", "arbitrary")))(qb_pad, qi_pad, wi_t, ki3, kv3, vt)


HALO = 32


def _conv_ln_silu(acc, g, b):
    mu = jnp.mean(acc, axis=1, keepdims=True)
    d = acc - mu
    var = jnp.mean(d * d, axis=1, keepdims=True)
    y = d * lax.rsqrt(var + LN_EPS) * g + b
    return y * _sigmoid(y)


def _conv_body(u_ref, halo_ref, w_ref, cb_ref, g_ref, b_ref, o_ref, ext_ref, sh_ref, *, tq):
    j = pl.program_id(1)
    ext_ref[0:HALO, :] = jnp.where(j > 0, halo_ref[...], 0.0)
    ext_ref[HALO:, :] = u_ref[...]
    base = HALO - (CONV_W - 1)
    for r in range(1, 8):
        sh_ref[r - 1] = ext_ref[r:r + tq + HALO - 8, :]
    accs = [jnp.zeros((tq, D_C), F32) + cb_ref[...], jnp.zeros((tq, D_C), F32)]
    for t in range(CONV_W):
        r, a = (base + t) % 8, (base + t) // 8
        if r == 0:
            x = ext_ref[8 * a:8 * a + tq, :]
        else:
            x = sh_ref[r - 1, 8 * a:8 * a + tq, :]
        accs[t % 2] = accs[t % 2] + x * w_ref[t:t + 1, :]
    o_ref[...] = _conv_ln_silu(accs[0] + accs[1], g_ref[...], b_ref[...])


def _conv_prompt(u, conv_w, conv_b, g, bb, b, s, tq):
    nq = s // tq
    hb = tq // HALO
    rows = lambda i, j: (i * nq + j, 0)
    return pl.pallas_call(
        functools.partial(_conv_body, tq=tq), grid=(b, nq), name="conv_prompt",
        in_specs=[pl.BlockSpec((tq, D_C), rows),
                  pl.BlockSpec((HALO, D_C), lambda i, j: (jnp.maximum((i * nq + j) * hb - 1, 0), 0)),
                  pl.BlockSpec((CONV_W, D_C), lambda i, j: (0, 0)),
                  pl.BlockSpec((1, D_C), lambda i, j: (0, 0)),
                  pl.BlockSpec((1, D_C), lambda i, j: (0, 0)),
                  pl.BlockSpec((1, D_C), lambda i, j: (0, 0))],
        out_specs=pl.BlockSpec((tq, D_C), rows),
        out_shape=jax.ShapeDtypeStruct((b * s, D_C), F32),
        scratch_shapes=[pltpu.VMEM((HALO + tq, D_C), F32),
                        pltpu.VMEM((7, tq + HALO - 8, D_C), F32)],
        compiler_params=_cparams(("parallel", "arbitrary")))(u, u, conv_w, conv_b, g, bb)


def _merge_body(x_ref, p_ref, ya_ref, yb_ref, yc_ref, sg_ref, wmg_ref, wa_ref, wb_ref, wc_ref,
                wo_ref, wpe_ref, wpg_ref, g_ref, b_ref, y_ref, ybf_ref, *, alpha):
    x = x_ref[...]
    gates = _sigmoid(_dot(x.astype(BF16), wmg_ref[...]))
    sg = sg_ref[...]
    ua = _dot((ya_ref[...] * sg[:, :W_A]).astype(BF16), wa_ref[...])
    ub = _dot((yb_ref[...] * sg[:, W_A:2 * W_A]).astype(BF16), wb_ref[...])
    uc = _dot((yc_ref[...] * sg[:, 2 * W_A:]).astype(BF16), wc_ref[...])
    m = (gates[:, :D_MODEL] * ua + gates[:, D_MODEL:2 * D_MODEL] * ub
         + gates[:, 2 * D_MODEL:] * uc)
    r = alpha * x + _dot(m.astype(BF16), wo_ref[...])
    mu = jnp.mean(r, axis=1, keepdims=True)
    d = r - mu
    var = jnp.mean(d * d, axis=1, keepdims=True)
    h = d * lax.rsqrt(var + LN_EPS) * g_ref[...] + b_ref[...]
    pe = _dot(p_ref[...].astype(BF16), wpe_ref[...])
    y = h + _sigmoid(_dot(h.astype(BF16), wpg_ref[...])) * pe
    y_ref[...] = y
    ybf_ref[...] = y.astype(BF16)


def _merge(x, p, ya, yb, yc, sg, wl, tm, alpha):
    m = x.shape[0]
    ws = [wl['w_mg'], wl['w_a'], wl['w_b'], wl['w_c'], wl['w_out'], wl['w_pe'], wl['w_pe_gate'],
          wl['ln_g'], wl['ln_b']]
    return pl.pallas_call(
        functools.partial(_merge_body, alpha=alpha), grid=(m // tm,), name="merge",
        in_specs=[_row_spec(tm, D_MODEL), _row_spec(tm, P_DIM), _row_spec(tm, W_A),
                  _row_spec(tm, W_B), _row_spec(tm, D_C), _row_spec(tm, 3 * W_A)]
                 + [_const_spec(w.shape) for w in ws],
        out_specs=[_row_spec(tm, D_MODEL), _row_spec(tm, D_MODEL)],
        out_shape=[jax.ShapeDtypeStruct((m, D_MODEL), F32), jax.ShapeDtypeStruct((m, D_MODEL), BF16)],
        compiler_params=_cparams(("parallel",)))(x, p, ya, yb, yc, sg, *ws)


def _fox_dec_body(pt_ref, q_ref, kvn_ref, lfn_ref, *refs, g, layer):
    kv_refs = refs[:g]
    lf_refs = refs[g:2 * g]
    o_ref = refs[2 * g]
    m_ref, l_ref, acc_ref, carry_ref = refs[2 * g + 1:]
    t = pl.program_id(1)
    qrows = q_ref[0]

    @pl.when(t == 0)
    def _():
        kn = kvn_ref[0][:, :W_A].astype(BF16).astype(F32)
        vn = kvn_ref[0][:, W_A:]
        s_new = jnp.sum(qrows.astype(F32) * kn, axis=1, keepdims=True)
        m_ref[...] = s_new
        l_ref[...] = jnp.ones_like(l_ref)
        acc_ref[...] = jnp.broadcast_to(vn.astype(BF16).astype(F32), (H_A, W_A))
        carry_ref[...] = lfn_ref[0]

    r = lax.broadcasted_iota(I32, (PAGE, PAGE), 0)
    c = lax.broadcasted_iota(I32, (PAGE, PAGE), 1)
    upper = jnp.where(r > c, 1.0, 0.0).astype(BF16)
    lf = jnp.concatenate([lf_refs[gi][0, 0] for gi in range(g)], axis=0)
    suf = _dot3(lf, upper)
    tot = jnp.sum(lf, axis=1, keepdims=True)
    carry = carry_ref[...]
    parts = []
    for gi in range(g):
        kt = kv_refs[gi][0, 0, :W_A, :].astype(BF16)
        bias = suf[gi * H_A:(gi + 1) * H_A, :] + carry
        carry = carry + tot[gi * H_A:(gi + 1) * H_A, :]
        parts.append(_dot(qrows, kt) + bias)
    carry_ref[...] = carry
    s = jnp.concatenate(parts, axis=1)
    m_old = m_ref[...]
    m_new = jnp.maximum(m_old, jnp.max(s, axis=1, keepdims=True))
    alpha = jnp.exp(m_old - m_new)
    p = jnp.exp(s - m_new)
    l_ref[...] = alpha * l_ref[...] + jnp.sum(p, axis=1, keepdims=True)
    pb = p.astype(BF16)
    acc = alpha * acc_ref[...]
    for gi in range(g):
        vt = kv_refs[gi][0, 0, W_A:, :].astype(BF16)
        acc = acc + _dot_nt(pb[:, gi * PAGE:(gi + 1) * PAGE], vt)
    acc_ref[...] = acc
    m_ref[...] = m_new

    @pl.when(t == pl.num_programs(1) - 1)
    def _():
        o_ref[0] = acc_ref[...] / l_ref[...]


def _fox_decode(qrows, kva, logf, page_table, kv_t, lf_t, g, layer):
    bd, n_pages = page_table.shape
    steps = n_pages // g

    def page_map(gi):
        return lambda b, t, pt: (layer, pt[b, n_pages - 1 - (t * g + gi)], 0, 0)

    per_b = lambda b, t, pt: (b, 0, 0)
    out = pl.pallas_call(
        functools.partial(_fox_dec_body, g=g, layer=layer), name="fox_decode",
        grid_spec=pltpu.PrefetchScalarGridSpec(
            num_scalar_prefetch=1, grid=(bd, steps),
            in_specs=[pl.BlockSpec((1, H_A, W_A), per_b), pl.BlockSpec((1, 1, 2 * W_A), per_b),
                      pl.BlockSpec((1, H_A, 1), per_b)]
                     + [pl.BlockSpec((1, 1, 2 * W_A, PAGE), page_map(gi)) for gi in range(g)]
                     + [pl.BlockSpec((1, 1, H_A, PAGE), page_map(gi)) for gi in range(g)],
            out_specs=pl.BlockSpec((1, H_A, W_A), per_b),
            scratch_shapes=[pltpu.VMEM((H_A, 1), F32), pltpu.VMEM((H_A, 1), F32),
                            pltpu.VMEM((H_A, W_A), F32), pltpu.VMEM((H_A, 1), F32)]),
        out_shape=jax.ShapeDtypeStruct((bd, H_A, W_A), F32),
        compiler_params=_cparams(("parallel", "arbitrary")))(
            page_table, qrows, kva.reshape(bd, 1, 2 * W_A), logf.reshape(bd, H_A, 1),
            *([kv_t] * g), *([lf_t] * g))
    o4 = out.reshape(bd, H_A, H_A, DH)
    return jnp.stack([o4[:, h, h] for h in range(H_A)], axis=1).reshape(bd, W_A)


def _idx_dec_body(pt_ref, qi_ref, wi_ref, kin_ref, *refs, g):
    ki_refs = refs[:g]
    o_ref, on_ref = refs[g:]
    qi = qi_ref[0]
    w = wi_ref[0]
    rows = []
    for gi in range(g):
        r = _dot(qi, ki_refs[gi][0, 0].astype(BF16))
        rows.append(jnp.sum(w * jnp.maximum(r, 0.0), axis=0, keepdims=True))
    o_ref[0] = jnp.concatenate(rows, axis=0)
    r_new = jnp.sum(qi.astype(F32) * kin_ref[0].astype(F32), axis=1, keepdims=True)
    sc_new = jnp.sum(w * jnp.maximum(r_new, 0.0), axis=0, keepdims=True)
    on_ref[0] = jnp.broadcast_to(sc_new, (1, LANES))


def _idx_decode(qi_bf, wi, ki_new_bf, page_table, ik_t, g, layer):
    bd, n_pages = page_table.shape
    steps = n_pages // g

    def page_map(gi):
        return lambda b, t, pt: (layer, pt[b, t * g + gi], 0, 0)

    per_b = lambda b, t, pt: (b, 0, 0)
    sc, sc_new = pl.pallas_call(
        functools.partial(_idx_dec_body, g=g), name="idx_decode",
        grid_spec=pltpu.PrefetchScalarGridSpec(
            num_scalar_prefetch=1, grid=(bd, steps),
            in_specs=[pl.BlockSpec((1, H_I, D_I), per_b), pl.BlockSpec((1, H_I, 1), per_b),
                      pl.BlockSpec((1, 1, D_I), per_b)]
                     + [pl.BlockSpec((1, 1, D_I, PAGE), page_map(gi)) for gi in range(g)],
            out_specs=[pl.BlockSpec((1, g, PAGE), lambda b, t, pt: (b, t, 0)),
                       pl.BlockSpec((1, 1, LANES), per_b)]),
        out_shape=[jax.ShapeDtypeStruct((bd, n_pages, PAGE), F32),
                   jax.ShapeDtypeStruct((bd, 1, LANES), F32)],
        compiler_params=_cparams(("parallel", "arbitrary")))(
            page_table, qi_bf, wi.reshape(bd, H_I, 1), ki_new_bf.reshape(bd, 1, D_I),
            *([ik_t] * g))
    return sc.reshape(bd, n_pages * PAGE), sc_new[:, 0, :1]


def _sel_dec_body(sc_ref, scn_ref, bias_ref, biasn_ref, *, n_top, n_bits):
    bd, p_len = sc_ref.shape
    sc = sc_ref[...]
    sc_n = scn_ref[...]
    idx = lax.broadcasted_iota(I32, (bd, p_len), 1)

    def count(pred, pred_n):
        return (jnp.sum(jnp.where(pred, 1.0, 0.0), axis=1, keepdims=True)
                + jnp.where(pred_n, 1.0, 0.0))

    thr, _ = _kth_largest(lambda cand: count(sc >= cand, sc_n >= cand), (bd, 1), n_top)
    need = jnp.float32(n_top) - count(sc > thr, sc_n > thr)
    cut = _tie_cut(lambda cand: jnp.sum(jnp.where((sc == thr) & (idx < cand), 1.0, 0.0),
                                        axis=1, keepdims=True), (bd, 1), need, n_bits)
    eq_past = jnp.sum(jnp.where(sc == thr, 1.0, 0.0), axis=1, keepdims=True)
    sel = (sc > thr) | ((sc == thr) & (idx <= cut))
    bias_ref[...] = jnp.where(sel, 0.0, NEG)
    sel_n = (sc_n > thr) | ((sc_n == thr) & (eq_past < need))
    biasn_ref[...] = jnp.where(sel_n, 0.0, NEG)


def _sel_decode(sc, sc_new, n_top):
    bd, p_len = sc.shape
    n_bits = int(np.log2(p_len))
    return pl.pallas_call(
        functools.partial(_sel_dec_body, n_top=n_top, n_bits=n_bits), name="sel_decode",
        out_shape=[jax.ShapeDtypeStruct((bd, p_len), F32), jax.ShapeDtypeStruct((bd, 1), F32)],
        compiler_params=pltpu.CompilerParams(vmem_limit_bytes=VMEM_LIMIT))(sc, sc_new)


def _dsa_dec_body(pt_ref, q_ref, kvn_ref, bn_ref, bias_ref, *refs, g):
    kv_refs = refs[:g]
    o_ref = refs[g]
    m_ref, l_ref, acc_ref = refs[g + 1:]
    t = pl.program_id(1)
    qrows = q_ref[0]

    @pl.when(t == 0)
    def _():
        kn = kvn_ref[0][:, :LANES].astype(BF16).astype(F32)
        vn = kvn_ref[0][:, LANES:].astype(BF16).astype(F32)
        s_new = jnp.sum(qrows.astype(F32) * kn, axis=1, keepdims=True) + bn_ref[0]
        m_ref[...] = s_new
        l_ref[...] = jnp.where(s_new > 0.5 * NEG, 1.0, 0.0)
        acc_ref[...] = jnp.where(s_new > 0.5 * NEG, jnp.broadcast_to(vn, (H_B, LANES)), 0.0)

    parts = []
    for gi in range(g):
        kt = kv_refs[gi][0, 0, :LANES, :].astype(BF16)
        parts.append(_dot(qrows, kt) + bias_ref[0, gi:gi + 1, :])
    s = jnp.concatenate(parts, axis=1)
    m_old = m_ref[...]
    m_new = jnp.maximum(m_old, jnp.max(s, axis=1, keepdims=True))
    alpha = jnp.exp(m_old - m_new)
    p = jnp.where(s > 0.5 * NEG, jnp.exp(s - m_new), 0.0)
    l_ref[...] = alpha * l_ref[...] + jnp.sum(p, axis=1, keepdims=True)
    pb = p.astype(BF16)
    acc = alpha * acc_ref[...]
    for gi in range(g):
        vt = kv_refs[gi][0, 0, LANES:, :].astype(BF16)
        acc = acc + _dot_nt(pb[:, gi * PAGE:(gi + 1) * PAGE], vt)
    acc_ref[...] = acc
    m_ref[...] = m_new

    @pl.when(t == pl.num_programs(1) - 1)
    def _():
        o_ref[0] = acc_ref[...] / l_ref[...]


def _dsa_decode(q2, kvb, bias, bias_new, page_table, kv_t, g, layer):
    bd, n_pages = page_table.shape
    steps = n_pages // g

    def page_map(gi):
        return lambda b, t, pt: (layer, pt[b, t * g + gi], 0, 0)

    per_b = lambda b, t, pt: (b, 0, 0)
    out = pl.pallas_call(
        functools.partial(_dsa_dec_body, g=g), name="dsa_decode",
        grid_spec=pltpu.PrefetchScalarGridSpec(
            num_scalar_prefetch=1, grid=(bd, steps),
            in_specs=[pl.BlockSpec((1, H_B, LANES), per_b), pl.BlockSpec((1, 1, 2 * LANES), per_b),
                      pl.BlockSpec((1, 1, 1), per_b),
                      pl.BlockSpec((1, g, PAGE), lambda b, t, pt: (b, t, 0))]
                     + [pl.BlockSpec((1, 1, 2 * LANES, PAGE), page_map(gi)) for gi in range(g)],
            out_specs=pl.BlockSpec((1, H_B, LANES), per_b),
            scratch_shapes=[pltpu.VMEM((H_B, 1), F32), pltpu.VMEM((H_B, 1), F32),
                            pltpu.VMEM((H_B, LANES), F32)]),
        out_shape=jax.ShapeDtypeStruct((bd, H_B, LANES), F32),
        compiler_params=_cparams(("parallel", "arbitrary")))(
            page_table, q2, kvb.reshape(bd, 1, 2 * LANES), bias_new.reshape(bd, 1, 1),
            bias.reshape(bd, n_pages, PAGE), *([kv_t] * g))
    o4 = out.reshape(bd, KV_B, G_B, KV_B, DH)
    return jnp.stack([o4[:, n, :, n] for n in range(KV_B)], axis=1).reshape(bd, W_B)


def _conv_dec_body(st_ref, u_ref, w_ref, cb_ref, g_ref, b_ref, o_ref):
    acc = u_ref[...] * w_ref[CONV_W - 1:CONV_W, :] + cb_ref[...]
    for t in range(CONV_W - 1):
        acc = acc + st_ref[0, t] * w_ref[t:t + 1, :]
    o_ref[...] = _conv_ln_silu(acc, g_ref[...], b_ref[...])


def _conv_decode(state_t, u, conv_w, conv_b, g, bb, layer):
    bd = u.shape[0]
    full = lambda a: pl.BlockSpec(a.shape, lambda i: (0,) * a.ndim)
    return pl.pallas_call(
        _conv_dec_body, grid=(1,), name="conv_decode",
        in_specs=[pl.BlockSpec((1,) + state_t.shape[1:], lambda i: (layer, 0, 0, 0)),
                  full(u), full(conv_w), full(conv_b), full(g), full(bb)],
        out_specs=pl.BlockSpec((bd, D_C), lambda i: (0, 0)),
        out_shape=jax.ShapeDtypeStruct((bd, D_C), F32),
        compiler_params=_cparams(("arbitrary",)))(state_t, u, conv_w, conv_b, g, bb)


def _rope_tables(pos):
    half = ROT_DIM // 2
    inv = ROPE_THETA ** (-jnp.arange(half, dtype=F32) / half)
    ang = pos.astype(F32)[:, None] * inv[None, :]
    cos, sin = jnp.cos(ang), jnp.sin(ang)
    n = pos.shape[0]
    pad = jnp.zeros((n, DH - ROT_DIM), F32)
    c = jnp.concatenate([cos, cos, pad + 1.0], axis=1)
    s1 = jnp.concatenate([jnp.zeros_like(sin), sin, pad], axis=1)
    s2 = jnp.concatenate([-sin, jnp.zeros_like(sin), pad], axis=1)
    return tuple(jnp.tile(t, (1, LANES // DH)) for t in (c, s1, s2))


def _unspread(q_pad, half_of_head):
    q = q_pad.reshape(q_pad.shape[0], H_A, 2, DH)
    return jnp.stack([q[:, h, half_of_head(h)] for h in range(H_A)], axis=1)


def _regroup_weights(w_in, b_f):
    w = w_in.astype(BF16)
    d = w.shape[0]
    z = lambda n: jnp.zeros((d, n), BF16)
    foxq = jnp.concatenate([w[:, O_QA:O_KA], w[:, O_FA:O_GA], z(LANES - H_A)], axis=1)
    foxkv = w[:, O_KA:O_FA]
    gates = jnp.concatenate([w[:, O_GA:O_QB], w[:, O_GB:O_GLU], w[:, O_GC:O_MG]], axis=1)
    wqb = w[:, O_QB:O_KB].reshape(d, KV_B, G_B, DH)
    wqb = jnp.swapaxes(wqb, 1, 2).reshape(d, W_B)
    dsaq = jnp.concatenate([wqb, w[:, O_QI:O_KI]], axis=1)
    dsak = jnp.concatenate([w[:, O_KB:O_QI], w[:, O_KI:O_GB], z(LANES - D_I - H_I)], axis=1)
    glu = w[:, O_GLU:O_GC]
    wt = jnp.transpose(jnp.concatenate([w[:, O_VB:O_QI], w[:, O_KA:O_FA]], axis=1))
    foxk = w[:, O_KA:O_VA]
    bf = jnp.concatenate([b_f.astype(F32), jnp.zeros((LANES - H_A,), F32)])[None, :]
    return dict(foxq=foxq, foxkv=foxkv, foxk=foxk, gates=gates, dsaq=dsaq, dsak=dsak, glu=glu, t=wt,
                b_f=bf,
                w_mg=w[:, O_MG:O_END])


def _pick(n, prefs):
    for t in prefs:
        if n % t == 0:
            return t
    return n


def kernel(x_prompt, x_sample, cache_fox_kv, cache_fox_logf, cache_dsa_kv, cache_dsa_idx_k, state_conv,
           page_table, p_prompt, p_sample, w_in, b_f, w_a, w_b, w_c, w_out, ln_g, ln_b,
           conv_w, conv_b, conv_ln_g, conv_ln_b, w_pe, w_pe_gate):
    depth = w_in.shape[0]
    b, s, d = x_prompt.shape
    bd = x_sample.shape[0]
    n_phys = cache_fox_kv.shape[1]
    n_pages = page_table.shape[1]
    p_len = n_pages * PAGE
    alpha = float((2 * depth) ** 0.25)
    tm = _pick(s, (256, 128))
    tq = _pick(s, (256, 128))
    g_fox = _pick(n_pages, (32, 16, 8, 4, 2, 1))
    g_dsa = _pick(n_pages, (64, 32, 16, 8, 4, 2, 1))
    g_idx = _pick(n_pages, (64, 32, 16, 8, 4, 2, 1))
    n_top_dec = min(TOPK_MAX, (p_len + 1) // 4)

    tabs_p = _rope_tables(jnp.arange(s, dtype=I32))
    tabs_s = _rope_tables(jnp.full((bd,), p_len, I32))
    page_table = page_table.astype(I32)

    fox_kv_t = jnp.transpose(cache_fox_kv, (0, 1, 3, 4, 5, 2)).reshape(depth, n_phys, 2 * W_A, PAGE)
    fox_lf_t = jnp.transpose(cache_fox_logf, (0, 1, 3, 2))
    dsa_kv_t = jnp.transpose(cache_dsa_kv, (0, 1, 3, 4, 5, 2)).reshape(depth, n_phys, 2 * LANES, PAGE)
    dsa_ik_t = jnp.transpose(cache_dsa_idx_k, (0, 1, 3, 2))
    state_t = jnp.transpose(state_conv, (0, 2, 1, 3))
    head_eye = jnp.repeat(jnp.eye(H_A, dtype=BF16), DH, axis=1)

    hp = x_prompt.reshape(b * s, d)
    hs = x_sample.reshape(bd, d)
    hp_bf = hp.astype(BF16)
    hs_bf = hs.astype(BF16)
    outs_p, outs_s = [], []
    kvat = jnp.zeros((depth, b, 2 * W_A, s), F32)
    for l in range(depth):
        wg = _regroup_weights(w_in[l], b_f[l])
        wl = dict(w_mg=wg['w_mg'], w_a=w_a[l].astype(BF16), w_b=w_b[l].astype(BF16),
                  w_c=w_c[l].astype(BF16), w_out=w_out[l].astype(BF16), w_pe=w_pe[l].astype(BF16),
                  w_pe_gate=w_pe_gate[l].astype(BF16), ln_g=ln_g[l][None, :], ln_b=ln_b[l][None, :])
        cw, cb = conv_w[l], conv_b[l][None, :]
        cg, cbb = conv_ln_g[l][None, :], conv_ln_b[l][None, :]

        pp = _project(hp_bf, wg, tabs_p, tm, s // tm, stacked=(kvat, l, b, s))
        vbt, vat, kvat = pp['vbt'], pp['vat'], pp['kvat']
        logf3 = pp['logf'].reshape(b, s, H_A)
        cum = _cumsum(logf3, _pick(s, (1024, 512, 256, 128)))
        cum_t = jnp.swapaxes(cum, 1, 2)
        wi_t = jnp.swapaxes(pp['wi'].reshape(b, s, H_I), 1, 2)
        ya = _fox_prompt(pp['q_pad'], pp['k_bf'], vat, cum, cum_t, b, s, tq)
        yb = _dsa_prompt(pp['qb_pad'], pp['qi_pad'], wi_t, pp['ki_pad'], pp['kvb_bf'], vbt, b, s, tq)
        yc = _conv_prompt(pp['u'], cw, cb, cg, cbb, b, s, tq)
        hp_new, hp_bf = _merge(hp, p_prompt[l].reshape(b * s, P_DIM), ya, yb, yc, pp['sg'], wl,
                               _pick(b * s, (512, 256, 128)), alpha)
        outs_p.append((None, logf3,
                       pp['kvb'].reshape(b, s, 2, KV_B, DH), pp['ki'].reshape(b, s, D_I),
                       pp['u'].reshape(b, s, D_C)[:, s - (CONV_W - 1):]))
        hp = hp_new

        ps = _project(hs_bf, wg, tabs_s, bd, 1)
        q_s = _unspread(ps['q_pad'], lambda h: h % 2).reshape(bd, 1, W_A)
        qrows = q_s * head_eye[None]
        ya_s = _fox_decode(qrows, ps['kva'], ps['logf'], page_table, fox_kv_t, fox_lf_t, g_fox, l)
        qi_s = _unspread(ps['qi_pad'], lambda h: h % 2)
        sc, sc_new = _idx_decode(qi_s, ps['wi'], ps['ki_pad'][:, :D_I], page_table, dsa_ik_t,
                                 g_idx, l)
        bias, bias_new = _sel_decode(sc, sc_new, n_top_dec)
        yb_s = _dsa_decode(ps['qb_pad'].reshape(bd, H_B, LANES), ps['kvb'], bias, bias_new,
                           page_table, dsa_kv_t, g_dsa, l)
        yc_s = _conv_decode(state_t, ps['u'], cw, cb, cg, cbb, l)
        hs_new, hs_bf = _merge(hs, p_sample[l].reshape(bd, P_DIM), ya_s, yb_s, yc_s, ps['sg'], wl,
                               bd, alpha)
        outs_s.append((ps['kva'].reshape(bd, 1, 2, H_A, DH), ps['logf'].reshape(bd, 1, H_A),
                       ps['kvb'].reshape(bd, 1, 2, KV_B, DH), ps['ki'].reshape(bd, 1, D_I),
                       jnp.concatenate([state_t[l, 1:], ps['u'][None]], axis=0)))
        hs = hs_new

    stack = lambda outs, k: jnp.stack([o[k] for o in outs])
    fox_kv_prompt = jnp.transpose(kvat.reshape(depth, b, 2, H_A, DH, s), (0, 1, 5, 2, 3, 4))
    return (hp.reshape(b, s, d), hs.reshape(bd, 1, d),
            fox_kv_prompt, stack(outs_s, 0), stack(outs_p, 1), stack(outs_s, 1),
            stack(outs_p, 2), stack(outs_s, 2), stack(outs_p, 3), stack(outs_s, 3),
            stack(outs_p, 4), jnp.transpose(stack(outs_s, 4), (0, 2, 1, 3)))
```

```python
import functools

import jax
import jax.numpy as jnp
import numpy as np
from jax import lax
from jax.experimental import pallas as pl
from jax.experimental.pallas import tpu as pltpu

F32 = jnp.float32
BF16 = jnp.bfloat16
I32 = jnp.int32

D_MODEL = 1024
H_A = 8
DH = 64
W_A = H_A * DH
KV_B = 2
H_B = 8
G_B = H_B // KV_B
W_B = H_B * DH
H_I = 8
D_I = 64
TOPK_MAX = 256
D_C = 512
CONV_W = 31
P_DIM = 256
N_BRANCH = 3
ROPE_THETA = 500000.0
ROT_DIM = DH // 4
LN_EPS = 1e-5
PAGE = 128
LANES = 128
PADW = H_A * LANES

PIPE_AHEAD = 3
PIPE_LAG = 2
PIPE_SLOTS = 8

NEG = -1e30
INT_MIN = -(2 ** 31)
VMEM_LIMIT = 56 * 1024 * 1024

_SPLIT = (W_A, W_A, W_A, H_A, W_A, W_B, KV_B * DH, KV_B * DH, H_I * D_I, D_I, H_I, W_B,
          2 * D_C, D_C, N_BRANCH * D_MODEL)
_OFF = np.concatenate([[0], np.cumsum(_SPLIT)]).tolist()
(O_QA, O_KA, O_VA, O_FA, O_GA, O_QB, O_KB, O_VB, O_QI, O_KI, O_WI, O_GB, O_GLU, O_GC, O_MG,
 O_END) = _OFF


def _cparams(sem):
    return pltpu.CompilerParams(dimension_semantics=sem, vmem_limit_bytes=VMEM_LIMIT)


def _dot(a, b):
    return jnp.dot(a, b, preferred_element_type=F32)


def _dot_nt(a, b):
    return lax.dot_general(a, b, (((1,), (1,)), ((), ())), preferred_element_type=F32)


def _sigmoid(x):
    return 1.0 / (1.0 + jnp.exp(-x))


def _split3(x):
    hi = x.astype(BF16)
    r1 = x - hi.astype(F32)
    mid = r1.astype(BF16)
    lo = (r1 - mid.astype(F32)).astype(BF16)
    return hi, mid, lo


def _dot3(x, w):
    hi, mid, lo = _split3(x)
    return _dot(hi, w) + _dot(mid, w) + _dot(lo, w)


def _rope(x, c, s1, s2):
    n = x.shape[-1]
    half = ROT_DIM // 2
    return x * c + pltpu.roll(x, half, 1) * s1 + pltpu.roll(x, n - half, 1) * s2


def _tile_lanes(t, reps):
    return t if reps == 1 else jnp.concatenate([t] * reps, axis=1)


def _spread_heads(y, heads_of_group):
    lane = lax.broadcasted_iota(I32, (y.shape[0], LANES), 1)
    out = [None] * (2 * len(heads_of_group))
    for g, (lo, hi) in enumerate(heads_of_group):
        yg = y[:, g * LANES:(g + 1) * LANES]
        out[lo] = jnp.where(lane < DH, yg, 0.0)
        out[hi] = jnp.where(lane >= DH, yg, 0.0)
    return jnp.concatenate(out, axis=1)


_PAIRS_ADJACENT = tuple((2 * g, 2 * g + 1) for g in range(H_A // 2))
_PAIRS_BY_KV = tuple((g, g + G_B) for g in range(G_B))


def _proj_foxq_body(x_ref, w_ref, bf_ref, q_ref, logf_ref):
    z = _dot(x_ref[...], w_ref[...])
    q_ref[...] = _spread_heads(z[:, :W_A] * (DH ** -0.5), _PAIRS_ADJACENT).astype(BF16)
    f = z[:, W_A:] + bf_ref[...]
    logf = jnp.minimum(f, 0.0) - jnp.log1p(jnp.exp(-jnp.abs(f)))
    logf_ref[...] = logf[:, :H_A]


def _proj_foxkv_body(x_ref, w_ref, k_ref, kv_ref):
    z = _dot(x_ref[...], w_ref[...])
    k_ref[...] = z[:, :W_A].astype(BF16)
    kv_ref[...] = z


def _proj_gates_body(x_ref, w_ref, o_ref):
    z = _dot(x_ref[...], w_ref[...])
    o_ref[...] = z * _sigmoid(z)


def _proj_dsaq_body(x_ref, w_ref, c_ref, s1_ref, s2_ref, qb_ref, qi_ref):
    z = _dot(x_ref[...], w_ref[...])
    reps = z.shape[1] // LANES
    y = _rope(z, _tile_lanes(c_ref[...], reps), _tile_lanes(s1_ref[...], reps),
              _tile_lanes(s2_ref[...], reps)) * (DH ** -0.5)
    qb_ref[...] = _spread_heads(y[:, :W_B], _PAIRS_BY_KV).astype(BF16)
    qi_ref[...] = _spread_heads(y[:, W_B:], _PAIRS_ADJACENT).astype(BF16)


def _proj_dsak_body(x_ref, w_ref, c_ref, s1_ref, s2_ref, kv_ref, kvbf_ref, ki_ref, kibf_ref, wi_ref):
    z = _dot(x_ref[...], w_ref[...])
    c, s1, s2 = c_ref[...], s1_ref[...], s2_ref[...]
    kb = _rope(z[:, :LANES], c, s1, s2)
    kv = jnp.concatenate([kb, z[:, LANES:2 * LANES]], axis=1)
    kv_ref[...] = kv
    kvbf_ref[...] = kv.astype(BF16)
    t = _rope(z[:, 2 * LANES:], c, s1, s2)
    ki_ref[...] = t[:, :D_I]
    lane = lax.broadcasted_iota(I32, t.shape, 1)
    kibf_ref[...] = jnp.where(lane < D_I, t, pltpu.roll(t, D_I, 1)).astype(BF16)
    wi_ref[...] = z[:, 2 * LANES + D_I:2 * LANES + D_I + H_I] * (H_I ** -0.5)


def _proj_glu_body(x_ref, w_ref, u_ref):
    z = _dot(x_ref[...], w_ref[...])
    u_ref[...] = z[:, :D_C] * _sigmoid(z[:, D_C:])


def _proj_t_body(x_ref, w_ref, vb_ref, kvt_ref, va_ref):
    zt = _dot_nt(w_ref[...], x_ref[...])
    vb_ref[0] = zt[:LANES].astype(BF16)
    kvt_ref[0, 0] = zt[LANES:]
    va_ref[0] = zt[LANES + W_A:].astype(BF16)


def _proj_foxk_body(x_ref, w_ref, k_ref):
    k_ref[...] = _dot(x_ref[...], w_ref[...]).astype(BF16)


def _row_spec(tm, n):
    return pl.BlockSpec((tm, n), lambda i: (i, 0))


def _const_spec(shape):
    return pl.BlockSpec(shape, lambda i: (0,) * len(shape), pipeline_mode=pl.Buffered(1))


_PROJ_ROW_OUTS = dict(q_pad=(PADW, BF16), logf=(H_A, F32), k_bf=(W_A, BF16), kva=(2 * W_A, F32),
                      sg=(3 * W_A, F32), qb_pad=(PADW, BF16), qi_pad=(PADW, BF16),
                      kvb=(2 * LANES, F32), kvb_bf=(2 * LANES, BF16), ki=(D_I, F32),
                      ki_pad=(LANES, BF16), wi=(H_I, F32), u=(D_C, F32))


def _proj_body(*refs, names, transposed):
    n_in = 11 + (2 if transposed else 0)
    (x_ref, wq_ref, wkv_ref, wg_ref, wdq_ref, wdk_ref, wglu_ref, bf_ref, c_ref, s1_ref,
     s2_ref) = refs[:11]
    o = dict(zip(names, refs[n_in:]))
    _proj_foxq_body(x_ref, wq_ref, bf_ref, o['q_pad'], o['logf'])
    if transposed:
        _proj_foxk_body(x_ref, wkv_ref, o['k_bf'])
        _proj_t_body(x_ref, refs[11], o['vbt'], o['kvat'], o['vat'])
    else:
        _proj_foxkv_body(x_ref, wkv_ref, o['k_bf'], o['kva'])
    _proj_gates_body(x_ref, wg_ref, o['sg'])
    _proj_dsaq_body(x_ref, wdq_ref, c_ref, s1_ref, s2_ref, o['qb_pad'], o['qi_pad'])
    _proj_dsak_body(x_ref, wdk_ref, c_ref, s1_ref, s2_ref, o['kvb'], o['kvb_bf'], o['ki'],
                    o['ki_pad'], o['wi'])
    _proj_glu_body(x_ref, wglu_ref, o['u'])


def _project(x_bf, wg, tabs, tm, tab_blocks, stacked=None):
    m = x_bf.shape[0]
    transposed = stacked is not None
    tab_spec = pl.BlockSpec((tm, LANES), lambda i: (i % tab_blocks, 0))
    names = [k for k in _PROJ_ROW_OUTS if not (transposed and k == 'kva')]
    ws = [wg['foxq'], wg['foxk'] if transposed else wg['foxkv'], wg['gates'], wg['dsaq'],
          wg['dsak'], wg['glu'], wg['b_f']]
    in_specs = ([_row_spec(tm, D_MODEL)] + [_const_spec(w.shape) for w in ws] + [tab_spec] * 3)
    args = [x_bf] + ws + list(tabs)
    out_specs = [_row_spec(tm, _PROJ_ROW_OUTS[k][0]) for k in names]
    out_shape = [jax.ShapeDtypeStruct((m,) + _PROJ_ROW_OUTS[k][:1], _PROJ_ROW_OUTS[k][1])
                 for k in names]
    aliases = {}
    if transposed:
        kvat, layer, b, s = stacked
        nt = s // tm
        in_specs += [_const_spec(wg['t'].shape), pl.BlockSpec(memory_space=pl.ANY)]
        args += [wg['t'], kvat]
        seq_block = lambda rows: pl.BlockSpec((1, rows, tm), lambda i: (i // nt, 0, i % nt))
        out_specs += [seq_block(LANES),
                      pl.BlockSpec((1, 1, 2 * W_A, tm), lambda i: (layer, i // nt, 0, i % nt)),
                      seq_block(W_A)]
        out_shape += [jax.ShapeDtypeStruct((b, LANES, s), BF16),
                      jax.ShapeDtypeStruct(kvat.shape, F32),
                      jax.ShapeDtypeStruct((b, W_A, s), BF16)]
        aliases = {len(args) - 1: len(names) + 1}
        names = names + ['vbt', 'kvat', 'vat']
    outs = pl.pallas_call(
        functools.partial(_proj_body, names=tuple(names), transposed=transposed),
        grid=(m // tm,), name="project", in_specs=in_specs, out_specs=out_specs,
        out_shape=out_shape, input_output_aliases=aliases,
        compiler_params=_cparams(("parallel",)))(*args)
    return dict(zip(names, outs))


def _cumsum_body(x_ref, o_ref, carry_ref):
    @pl.when(pl.program_id(1) == 0)
    def _():
        carry_ref[...] = jnp.zeros_like(carry_ref)

    t = x_ref.shape[1]
    r = lax.broadcasted_iota(I32, (t, t), 0)
    c = lax.broadcasted_iota(I32, (t, t), 1)
    tri = jnp.where(c <= r, 1.0, 0.0).astype(BF16)
    hi, mid, lo = _split3(x_ref[0])
    cum = _dot(tri, hi) + _dot(tri, mid) + _dot(tri, lo) + carry_ref[...]
    o_ref[0] = cum
    carry_ref[...] = cum[t - 1:t, :]


def _cumsum(logf, tb):
    b, s, h = logf.shape
    return pl.pallas_call(
        _cumsum_body, grid=(b, s // tb), name="cumsum",
        in_specs=[pl.BlockSpec((1, tb, h), lambda i, j: (i, j, 0))],
        out_specs=pl.BlockSpec((1, tb, h), lambda i, j: (i, j, 0)),
        out_shape=jax.ShapeDtypeStruct((b, s, h), F32),
        scratch_shapes=[pltpu.VMEM((1, h), F32)],
        compiler_params=_cparams(("parallel", "arbitrary")))(logf)


def _fox_body(q_ref, k_ref, vt_ref, cq_ref, ck_ref, o_ref, ckb_ref, acc_ref, m_ref, l_ref, ot_ref,
              s_ref, p_ref, *, tq):
    i = pl.program_id(1)

    @pl.when(i == 0)
    def _():
        ck = ck_ref[0]
        for h in range(H_A):
            ckb_ref[h] = jnp.broadcast_to(ck[:, h:h + 1], (ck.shape[0], LANES))

    acc_ref[...] = jnp.zeros_like(acc_ref)
    l_ref[...] = jnp.zeros_like(l_ref)
    m_ref[...] = jnp.full(m_ref.shape, NEG, F32)
    cq = cq_ref[0]
    ones = jnp.ones((16, tq), BF16)

    def step(off, diag):
        if diag:
            krow = lax.broadcasted_iota(I32, (tq, tq), 0)
            qcol = lax.broadcasted_iota(I32, (tq, tq), 1)
            causal = krow <= qcol

        def qk(h):
            grp = slice((h // 2) * LANES, (h // 2 + 1) * LANES)
            k = k_ref[0, pl.ds(off, tq), grp]
            return _dot_nt(k, q_ref[:, h * LANES:(h + 1) * LANES])

        def pv_update(h, alpha):
            vt = jnp.concatenate([vt_ref[0, h * DH:(h + 1) * DH, pl.ds(off, tq)], ones], axis=0)
            pv = _dot(vt, p_ref[h % PIPE_SLOTS])
            acc_ref[h] = alpha * acc_ref[h] + pv[:DH, :]
            l_ref[h] = alpha * l_ref[h] + pv[DH:DH + 1, :]

        ahead, lag, slots = PIPE_AHEAD, PIPE_LAG, PIPE_SLOTS
        for h in range(ahead):
            s_ref[h % slots] = qk(h)
        pending = []
        for h in range(H_A):
            if h + ahead < H_A:
                s_ref[(h + ahead) % slots] = qk(h + ahead)
            ckb = ckb_ref[h, pl.ds(off, tq), :]
            s = s_ref[h % slots] + cq[h:h + 1, :] - _tile_lanes(ckb, tq // LANES)
            if diag:
                s = jnp.where(causal, s, NEG)
            m_old = m_ref[h]
            m_new = jnp.maximum(m_old, jnp.max(s, axis=0, keepdims=True))
            p_ref[h % slots] = jnp.exp(s - m_new).astype(BF16)
            m_ref[h] = m_new
            pending.append((h, jnp.exp(m_old - m_new)))
            if len(pending) > lag:
                pv_update(*pending.pop(0))
        for item in pending:
            pv_update(*item)

    def body(c2, carry):
        step(pl.multiple_of(2 * c2 * tq, tq), False)
        step(pl.multiple_of((2 * c2 + 1) * tq, tq), False)
        return carry

    lax.fori_loop(0, i // 2, body, 0)

    @pl.when(i % 2 == 1)
    def _():
        step(pl.multiple_of((i - 1) * tq, tq), False)

    step(pl.multiple_of(i * tq, tq), True)
    for h in range(H_A):
        ot_ref[h * DH:(h + 1) * DH, :] = acc_ref[h] / l_ref[h]
    o_ref[...] = ot_ref[...].T


def _fox_prompt(q_pad, k_bf, vt, cum, cum_t, b, s, tq):
    nq = s // tq
    k3 = k_bf.reshape(b, s, W_A)
    return pl.pallas_call(
        functools.partial(_fox_body, tq=tq), grid=(b, nq), name="fox_prompt",
        in_specs=[pl.BlockSpec((tq, PADW), lambda i, j: (i * nq + j, 0)),
                  pl.BlockSpec((1, s, W_A), lambda i, j: (i, 0, 0)),
                  pl.BlockSpec((1, W_A, s), lambda i, j: (i, 0, 0)),
                  pl.BlockSpec((1, H_A, tq), lambda i, j: (i, 0, j)),
                  pl.BlockSpec((1, s, H_A), lambda i, j: (i, 0, 0))],
        out_specs=pl.BlockSpec((tq, W_A), lambda i, j: (i * nq + j, 0)),
        out_shape=jax.ShapeDtypeStruct((b * s, W_A), F32),
        scratch_shapes=[pltpu.VMEM((H_A, s, LANES), F32), pltpu.VMEM((H_A, DH, tq), F32),
                        pltpu.VMEM((H_A, 1, tq), F32), pltpu.VMEM((H_A, 1, tq), F32),
                        pltpu.VMEM((W_A, tq), F32), pltpu.VMEM((PIPE_SLOTS, tq, tq), F32),
                        pltpu.VMEM((PIPE_SLOTS, tq, tq), BF16)],
        compiler_params=_cparams(("parallel", "arbitrary")))(q_pad, k3, vt, cum_t, cum)


def _key_to_f32(key):
    bits = key ^ ((key >> 31) & 0x7FFFFFFF)
    return lax.bitcast_convert_type(bits, F32)


def _kth_largest(count_ge, shape, n_top):
    kf = jnp.float32(n_top)
    zero = jnp.zeros(shape, I32)
    t0 = jnp.where(count_ge(_key_to_f32(zero)) >= kf, zero, jnp.full(shape, INT_MIN, I32))

    def bit_step(it, t):
        cand = t + jnp.left_shift(jnp.int32(1), 30 - it)
        return jnp.where(count_ge(_key_to_f32(cand)) >= kf, cand, t)

    key = lax.fori_loop(0, 31, bit_step, t0)
    few = key == INT_MIN
    return jnp.where(few, -jnp.inf, _key_to_f32(key)), few


def _tie_cut(count_eq_below, shape, need, n_bits):
    def bit_step(it, j):
        cand = j + jnp.left_shift(jnp.int32(1), n_bits - 1 - it)
        return jnp.where(count_eq_below(cand) < need, cand, j)

    return lax.fori_loop(0, n_bits, bit_step, jnp.zeros(shape, I32))


def _dsa_body(qb_ref, qi_ref, wi_ref, ki_ref, k_ref, vt_ref, o_ref,
              sc_ref, bias_ref, acc_ref, m_ref, l_ref, ot_ref, s_ref, p_ref, *, tq, s_len, n_top):
    i = pl.program_id(1)
    nk = i + 1
    kidx0 = lax.broadcasted_iota(I32, (tq, tq), 0)
    qidx = i * tq + lax.broadcasted_iota(I32, (tq, tq), 1)
    wi = wi_ref[0]

    def score_chunk(c, _):
        off = pl.multiple_of(c * tq, tq)
        ki = ki_ref[0, pl.ds(off, tq), :]
        sc = jnp.zeros((tq, tq), F32)
        for h in range(H_I):
            r = _dot_nt(ki, qi_ref[:, h * LANES:(h + 1) * LANES])
            sc = sc + wi[h:h + 1, :] * jnp.maximum(r, 0.0)
        sc_ref[pl.ds(off, tq), :] = jnp.where(kidx0 + off <= qidx, sc, -jnp.inf)
        return 0

    def score_pair(c2, _):
        score_chunk(2 * c2, 0)
        score_chunk(2 * c2 + 1, 0)
        return 0

    lax.fori_loop(0, nk // 2, score_pair, 0)

    @pl.when(nk % 2 == 1)
    def _():
        score_chunk(nk - 1, 0)

    def count(pred):
        def body(c, cnt):
            off = pl.multiple_of(c * tq, tq)
            hit = jnp.where(pred(sc_ref[pl.ds(off, tq), :], off), 1.0, 0.0)
            parts = [hit[r * 8:(r + 1) * 8, :] for r in range(tq // 8)]
            while len(parts) > 1:
                parts = [parts[a] + parts[a + 1] for a in range(0, len(parts), 2)]
            return cnt + parts[0]
        cnt = lax.fori_loop(0, nk, body, jnp.zeros((8, tq), F32))
        return jnp.sum(cnt, axis=0, keepdims=True)

    thr, few = _kth_largest(lambda cand: count(lambda x, off: x >= cand), (1, tq), n_top)
    need = jnp.where(few, 0.0, jnp.float32(n_top) - count(lambda x, off: x > thr))
    below = jnp.where(lax.broadcasted_iota(I32, (tq, tq), 1) < kidx0, 1.0, 0.0).astype(BF16)

    def bias_chunk(c, seen):
        off = pl.multiple_of(c * tq, tq)
        x = sc_ref[pl.ds(off, tq), :]
        tied = jnp.where(x == thr, 1.0, 0.0)
        rank = _dot(below, tied.astype(BF16)) + seen
        keep_tied = jnp.where(rank < need, 0.0, NEG)
        bias_ref[pl.ds(off, tq), :] = jnp.where(x > thr, 0.0, jnp.where(x == thr, keep_tied, NEG))
        return seen + jnp.sum(tied, axis=0, keepdims=True)

    def bias_pair(c2, seen):
        return bias_chunk(2 * c2 + 1, bias_chunk(2 * c2, seen))

    seen = lax.fori_loop(0, nk // 2, bias_pair, jnp.zeros((1, tq), F32))

    @pl.when(nk % 2 == 1)
    def _():
        bias_chunk(nk - 1, seen)

    acc_ref[...] = jnp.zeros_like(acc_ref)
    l_ref[...] = jnp.zeros_like(l_ref)
    m_ref[...] = jnp.full(m_ref.shape, NEG, F32)
    ones = jnp.ones((16, tq), BF16)

    def attend(c, _):
        off = pl.multiple_of(c * tq, tq)
        k = k_ref[0, pl.ds(off, tq), :]
        vt = jnp.concatenate([vt_ref[0, :, pl.ds(off, tq)], ones], axis=0)
        bias = bias_ref[pl.ds(off, tq), :]
        def qk(h):
            return _dot_nt(k, qb_ref[:, h * LANES:(h + 1) * LANES])

        def pv_update(h, alpha):
            n = h // G_B
            pv = _dot(vt, p_ref[h % PIPE_SLOTS])
            acc_ref[h] = alpha * acc_ref[h] + pv[n * DH:(n + 1) * DH, :]
            l_ref[h] = alpha * l_ref[h] + pv[LANES:LANES + 1, :]

        ahead, lag, slots = PIPE_AHEAD, PIPE_LAG, PIPE_SLOTS
        for h in range(ahead):
            s_ref[h % slots] = qk(h) + bias
        pending = []
        for h in range(H_B):
            if h + ahead < H_B:
                s_ref[(h + ahead) % slots] = qk(h + ahead) + bias
            m_old = m_ref[h]
            m_new = jnp.maximum(m_old, jnp.max(s_ref[h % slots], axis=0, keepdims=True))
            alpha = jnp.exp(m_old - m_new)
            p_ref[h % slots] = jnp.exp(s_ref[h % slots] - m_new).astype(BF16)
            m_ref[h] = m_new
            pending.append((h, alpha))
            if len(pending) > lag:
                pv_update(*pending.pop(0))
        for item in pending:
            pv_update(*item)
        return 0

    def attend_pair(c2, _):
        attend(2 * c2, 0)
        attend(2 * c2 + 1, 0)
        return 0

    lax.fori_loop(0, nk // 2, attend_pair, 0)

    @pl.when(nk % 2 == 1)
    def _():
        attend(nk - 1, 0)
    for h in range(H_B):
        ot_ref[h * DH:(h + 1) * DH, :] = acc_ref[h] / l_ref[h]
    o_ref[...] = ot_ref[...].T


def _dsa_prompt(qb_pad, qi_pad, wi_t, ki_pad, kvb_bf, vt, b, s, tq):
    nq = s // tq
    n_top = min(TOPK_MAX, s // 4)
    ki3 = ki_pad.reshape(b, s, LANES)
    kv3 = kvb_bf.reshape(b, s, 2 * LANES)
    rows = lambda i, j: (i * nq + j, 0)
    return pl.pallas_call(
        functools.partial(_dsa_body, tq=tq, s_len=s, n_top=n_top), grid=(b, nq), name="dsa_prompt",
        in_specs=[pl.BlockSpec((tq, PADW), rows), pl.BlockSpec((tq, PADW), rows),
                  pl.BlockSpec((1, H_I, tq), lambda i, j: (i, 0, j)),
                  pl.BlockSpec((1, s, LANES), lambda i, j: (i, 0, 0)),
                  pl.BlockSpec((1, s, LANES), lambda i, j: (i, 0, 0)),
                  pl.BlockSpec((1, LANES, s), lambda i, j: (i, 0, 0))],
        out_specs=pl.BlockSpec((tq, W_B), rows),
        out_shape=jax.ShapeDtypeStruct((b * s, W_B), F32),
        scratch_shapes=[pltpu.VMEM((s, tq), F32), pltpu.VMEM((s, tq), F32),
                        pltpu.VMEM((H_B, DH, tq), F32), pltpu.VMEM((H_B, 1, tq), F32),
                        pltpu.VMEM((H_B, 1, tq), F32), pltpu.VMEM((W_B, tq), F32),
                        pltpu.VMEM((PIPE_SLOTS, tq, tq), F32),
                        pltpu.VMEM((PIPE_SLOTS, tq, tq), BF16)],
        compiler_params=_cparams(("parallel", "arbitrary")))(qb_pad, qi_pad, wi_t, ki3, kv3, vt)


HALO = 32


def _conv_ln_silu(acc, g, b):
    mu = jnp.mean(acc, axis=1, keepdims=True)
    d = acc - mu
    var = jnp.mean(d * d, axis=1, keepdims=True)
    y = d * lax.rsqrt(var + LN_EPS) * g + b
    return y * _sigmoid(y)


def _conv_body(u_ref, halo_ref, w_ref, cb_ref, g_ref, b_ref, o_ref, ext_ref, sh_ref, *, tq):
    j = pl.program_id(1)
    ext_ref[0:HALO, :] = jnp.where(j > 0, halo_ref[...], 0.0)
    ext_ref[HALO:, :] = u_ref[...]
    base = HALO - (CONV_W - 1)
    for r in range(1, 8):
        sh_ref[r - 1] = ext_ref[r:r + tq + HALO - 8, :]
    accs = [jnp.zeros((tq, D_C), F32) + cb_ref[...], jnp.zeros((tq, D_C), F32)]
    for t in range(CONV_W):
        r, a = (base + t) % 8, (base + t) // 8
        if r == 0:
            x = ext_ref[8 * a:8 * a + tq, :]
        else:
            x = sh_ref[r - 1, 8 * a:8 * a + tq, :]
        accs[t % 2] = accs[t % 2] + x * w_ref[t:t + 1, :]
    o_ref[...] = _conv_ln_silu(accs[0] + accs[1], g_ref[...], b_ref[...])


def _conv_prompt(u, conv_w, conv_b, g, bb, b, s, tq):
    nq = s // tq
    hb = tq // HALO
    rows = lambda i, j: (i * nq + j, 0)
    return pl.pallas_call(
        functools.partial(_conv_body, tq=tq), grid=(b, nq), name="conv_prompt",
        in_specs=[pl.BlockSpec((tq, D_C), rows),
                  pl.BlockSpec((HALO, D_C), lambda i, j: (jnp.maximum((i * nq + j) * hb - 1, 0), 0)),
                  pl.BlockSpec((CONV_W, D_C), lambda i, j: (0, 0)),
                  pl.BlockSpec((1, D_C), lambda i, j: (0, 0)),
                  pl.BlockSpec((1, D_C), lambda i, j: (0, 0)),
                  pl.BlockSpec((1, D_C), lambda i, j: (0, 0))],
        out_specs=pl.BlockSpec((tq, D_C), rows),
        out_shape=jax.ShapeDtypeStruct((b * s, D_C), F32),
        scratch_shapes=[pltpu.VMEM((HALO + tq, D_C), F32),
                        pltpu.VMEM((7, tq + HALO - 8, D_C), F32)],
        compiler_params=_cparams(("parallel", "arbitrary")))(u, u, conv_w, conv_b, g, bb)


def _merge_body(x_ref, p_ref, ya_ref, yb_ref, yc_ref, sg_ref, wmg_ref, wa_ref, wb_ref, wc_ref,
                wo_ref, wpe_ref, wpg_ref, g_ref, b_ref, y_ref, ybf_ref, *, alpha):
    x = x_ref[...]
    gates = _sigmoid(_dot(x.astype(BF16), wmg_ref[...]))
    sg = sg_ref[...]
    ua = _dot((ya_ref[...] * sg[:, :W_A]).astype(BF16), wa_ref[...])
    ub = _dot((yb_ref[...] * sg[:, W_A:2 * W_A]).astype(BF16), wb_ref[...])
    uc = _dot((yc_ref[...] * sg[:, 2 * W_A:]).astype(BF16), wc_ref[...])
    m = (gates[:, :D_MODEL] * ua + gates[:, D_MODEL:2 * D_MODEL] * ub
         + gates[:, 2 * D_MODEL:] * uc)
    r = alpha * x + _dot(m.astype(BF16), wo_ref[...])
    mu = jnp.mean(r, axis=1, keepdims=True)
    d = r - mu
    var = jnp.mean(d * d, axis=1, keepdims=True)
    h = d * lax.rsqrt(var + LN_EPS) * g_ref[...] + b_ref[...]
    pe = _dot(p_ref[...].astype(BF16), wpe_ref[...])
    y = h + _sigmoid(_dot(h.astype(BF16), wpg_ref[...])) * pe
    y_ref[...] = y
    ybf_ref[...] = y.astype(BF16)


def _merge(x, p, ya, yb, yc, sg, wl, tm, alpha):
    m = x.shape[0]
    ws = [wl['w_mg'], wl['w_a'], wl['w_b'], wl['w_c'], wl['w_out'], wl['w_pe'], wl['w_pe_gate'],
          wl['ln_g'], wl['ln_b']]
    return pl.pallas_call(
        functools.partial(_merge_body, alpha=alpha), grid=(m // tm,), name="merge",
        in_specs=[_row_spec(tm, D_MODEL), _row_spec(tm, P_DIM), _row_spec(tm, W_A),
                  _row_spec(tm, W_B), _row_spec(tm, D_C), _row_spec(tm, 3 * W_A)]
                 + [_const_spec(w.shape) for w in ws],
        out_specs=[_row_spec(tm, D_MODEL), _row_spec(tm, D_MODEL)],
        out_shape=[jax.ShapeDtypeStruct((m, D_MODEL), F32), jax.ShapeDtypeStruct((m, D_MODEL), BF16)],
        compiler_params=_cparams(("parallel",)))(x, p, ya, yb, yc, sg, *ws)


def _fox_dec_body(pt_ref, q_ref, kvn_ref, lfn_ref, *refs, g, layer):
    kv_refs = refs[:g]
    lf_refs = refs[g:2 * g]
    o_ref = refs[2 * g]
    m_ref, l_ref, acc_ref, carry_ref = refs[2 * g + 1:]
    t = pl.program_id(1)
    qrows = q_ref[0]

    @pl.when(t == 0)
    def _():
        kn = kvn_ref[0][:, :W_A].astype(BF16).astype(F32)
        vn = kvn_ref[0][:, W_A:]
        s_new = jnp.sum(qrows.astype(F32) * kn, axis=1, keepdims=True)
        m_ref[...] = s_new
        l_ref[...] = jnp.ones_like(l_ref)
        acc_ref[...] = jnp.broadcast_to(vn.astype(BF16).astype(F32), (H_A, W_A))
        carry_ref[...] = lfn_ref[0]

    r = lax.broadcasted_iota(I32, (PAGE, PAGE), 0)
    c = lax.broadcasted_iota(I32, (PAGE, PAGE), 1)
    upper = jnp.where(r > c, 1.0, 0.0).astype(BF16)
    lf = jnp.concatenate([lf_refs[gi][0, 0] for gi in range(g)], axis=0)
    suf = _dot3(lf, upper)
    tot = jnp.sum(lf, axis=1, keepdims=True)
    carry = carry_ref[...]
    parts = []
    for gi in range(g):
        kt = kv_refs[gi][0, 0, :W_A, :].astype(BF16)
        bias = suf[gi * H_A:(gi + 1) * H_A, :] + carry
        carry = carry + tot[gi * H_A:(gi + 1) * H_A, :]
        parts.append(_dot(qrows, kt) + bias)
    carry_ref[...] = carry
    s = jnp.concatenate(parts, axis=1)
    m_old = m_ref[...]
    m_new = jnp.maximum(m_old, jnp.max(s, axis=1, keepdims=True))
    alpha = jnp.exp(m_old - m_new)
    p = jnp.exp(s - m_new)
    l_ref[...] = alpha * l_ref[...] + jnp.sum(p, axis=1, keepdims=True)
    pb = p.astype(BF16)
    acc = alpha * acc_ref[...]
    for gi in range(g):
        vt = kv_refs[gi][0, 0, W_A:, :].astype(BF16)
        acc = acc + _dot_nt(pb[:, gi * PAGE:(gi + 1) * PAGE], vt)
    acc_ref[...] = acc
    m_ref[...] = m_new

    @pl.when(t == pl.num_programs(1) - 1)
    def _():
        o_ref[0] = acc_ref[...] / l_ref[...]


def _fox_decode(qrows, kva, logf, page_table, kv_t, lf_t, g, layer):
    bd, n_pages = page_table.shape
    steps = n_pages // g

    def page_map(gi):
        return lambda b, t, pt: (layer, pt[b, n_pages - 1 - (t * g + gi)], 0, 0)

    per_b = lambda b, t, pt: (b, 0, 0)
    out = pl.pallas_call(
        functools.partial(_fox_dec_body, g=g, layer=layer), name="fox_decode",
        grid_spec=pltpu.PrefetchScalarGridSpec(
            num_scalar_prefetch=1, grid=(bd, steps),
            in_specs=[pl.BlockSpec((1, H_A, W_A), per_b), pl.BlockSpec((1, 1, 2 * W_A), per_b),
                      pl.BlockSpec((1, H_A, 1), per_b)]
                     + [pl.BlockSpec((1, 1, 2 * W_A, PAGE), page_map(gi)) for gi in range(g)]
                     + [pl.BlockSpec((1, 1, H_A, PAGE), page_map(gi)) for gi in range(g)],
            out_specs=pl.BlockSpec((1, H_A, W_A), per_b),
            scratch_shapes=[pltpu.VMEM((H_A, 1), F32), pltpu.VMEM((H_A, 1), F32),
                            pltpu.VMEM((H_A, W_A), F32), pltpu.VMEM((H_A, 1), F32)]),
        out_shape=jax.ShapeDtypeStruct((bd, H_A, W_A), F32),
        compiler_params=_cparams(("parallel", "arbitrary")))(
            page_table, qrows, kva.reshape(bd, 1, 2 * W_A), logf.reshape(bd, H_A, 1),
            *([kv_t] * g), *([lf_t] * g))
    o4 = out.reshape(bd, H_A, H_A, DH)
    return jnp.stack([o4[:, h, h] for h in range(H_A)], axis=1).reshape(bd, W_A)


def _idx_dec_body(pt_ref, qi_ref, wi_ref, kin_ref, *refs, g):
    ki_refs = refs[:g]
    o_ref, on_ref = refs[g:]
    qi = qi_ref[0]
    w = wi_ref[0]
    rows = []
    for gi in range(g):
        r = _dot(qi, ki_refs[gi][0, 0].astype(BF16))
        rows.append(jnp.sum(w * jnp.maximum(r, 0.0), axis=0, keepdims=True))
    o_ref[0] = jnp.concatenate(rows, axis=0)
    r_new = jnp.sum(qi.astype(F32) * kin_ref[0].astype(F32), axis=1, keepdims=True)
    sc_new = jnp.sum(w * jnp.maximum(r_new, 0.0), axis=0, keepdims=True)
    on_ref[0] = jnp.broadcast_to(sc_new, (1, LANES))


def _idx_decode(qi_bf, wi, ki_new_bf, page_table, ik_t, g, layer):
    bd, n_pages = page_table.shape
    steps = n_pages // g

    def page_map(gi):
        return lambda b, t, pt: (layer, pt[b, t * g + gi], 0, 0)

    per_b = lambda b, t, pt: (b, 0, 0)
    sc, sc_new = pl.pallas_call(
        functools.partial(_idx_dec_body, g=g), name="idx_decode",
        grid_spec=pltpu.PrefetchScalarGridSpec(
            num_scalar_prefetch=1, grid=(bd, steps),
            in_specs=[pl.BlockSpec((1, H_I, D_I), per_b), pl.BlockSpec((1, H_I, 1), per_b),
                      pl.BlockSpec((1, 1, D_I), per_b)]
                     + [pl.BlockSpec((1, 1, D_I, PAGE), page_map(gi)) for gi in range(g)],
            out_specs=[pl.BlockSpec((1, g, PAGE), lambda b, t, pt: (b, t, 0)),
                       pl.BlockSpec((1, 1, LANES), per_b)]),
        out_shape=[jax.ShapeDtypeStruct((bd, n_pages, PAGE), F32),
                   jax.ShapeDtypeStruct((bd, 1, LANES), F32)],
        compiler_params=_cparams(("parallel", "arbitrary")))(
            page_table, qi_bf, wi.reshape(bd, H_I, 1), ki_new_bf.reshape(bd, 1, D_I),
            *([ik_t] * g))
    return sc.reshape(bd, n_pages * PAGE), sc_new[:, 0, :1]


def _sel_dec_body(sc_ref, scn_ref, bias_ref, biasn_ref, *, n_top, n_bits):
    bd, p_len = sc_ref.shape
    sc = sc_ref[...]
    sc_n = scn_ref[...]
    idx = lax.broadcasted_iota(I32, (bd, p_len), 1)

    def count(pred, pred_n):
        return (jnp.sum(jnp.where(pred, 1.0, 0.0), axis=1, keepdims=True)
                + jnp.where(pred_n, 1.0, 0.0))

    thr, _ = _kth_largest(lambda cand: count(sc >= cand, sc_n >= cand), (bd, 1), n_top)
    need = jnp.float32(n_top) - count(sc > thr, sc_n > thr)
    cut = _tie_cut(lambda cand: jnp.sum(jnp.where((sc == thr) & (idx < cand), 1.0, 0.0),
                                        axis=1, keepdims=True), (bd, 1), need, n_bits)
    eq_past = jnp.sum(jnp.where(sc == thr, 1.0, 0.0), axis=1, keepdims=True)
    sel = (sc > thr) | ((sc == thr) & (idx <= cut))
    bias_ref[...] = jnp.where(sel, 0.0, NEG)
    sel_n = (sc_n > thr) | ((sc_n == thr) & (eq_past < need))
    biasn_ref[...] = jnp.where(sel_n, 0.0, NEG)


def _sel_decode(sc, sc_new, n_top):
    bd, p_len = sc.shape
    n_bits = int(np.log2(p_len))
    return pl.pallas_call(
        functools.partial(_sel_dec_body, n_top=n_top, n_bits=n_bits), name="sel_decode",
        out_shape=[jax.ShapeDtypeStruct((bd, p_len), F32), jax.ShapeDtypeStruct((bd, 1), F32)],
        compiler_params=pltpu.CompilerParams(vmem_limit_bytes=VMEM_LIMIT))(sc, sc_new)


def _dsa_dec_body(pt_ref, q_ref, kvn_ref, bn_ref, bias_ref, *refs, g):
    kv_refs = refs[:g]
    o_ref = refs[g]
    m_ref, l_ref, acc_ref = refs[g + 1:]
    t = pl.program_id(1)
    qrows = q_ref[0]

    @pl.when(t == 0)
    def _():
        kn = kvn_ref[0][:, :LANES].astype(BF16).astype(F32)
        vn = kvn_ref[0][:, LANES:].astype(BF16).astype(F32)
        s_new = jnp.sum(qrows.astype(F32) * kn, axis=1, keepdims=True) + bn_ref[0]
        m_ref[...] = s_new
        l_ref[...] = jnp.where(s_new > 0.5 * NEG, 1.0, 0.0)
        acc_ref[...] = jnp.where(s_new > 0.5 * NEG, jnp.broadcast_to(vn, (H_B, LANES)), 0.0)

    parts = []
    for gi in range(g):
        kt = kv_refs[gi][0, 0, :LANES, :].astype(BF16)
        parts.append(_dot(qrows, kt) + bias_ref[0, gi:gi + 1, :])
    s = jnp.concatenate(parts, axis=1)
    m_old = m_ref[...]
    m_new = jnp.maximum(m_old, jnp.max(s, axis=1, keepdims=True))
    alpha = jnp.exp(m_old - m_new)
    p = jnp.where(s > 0.5 * NEG, jnp.exp(s - m_new), 0.0)
    l_ref[...] = alpha * l_ref[...] + jnp.sum(p, axis=1, keepdims=True)
    pb = p.astype(BF16)
    acc = alpha * acc_ref[...]
    for gi in range(g):
        vt = kv_refs[gi][0, 0, LANES:, :].astype(BF16)
        acc = acc + _dot_nt(pb[:, gi * PAGE:(gi + 1) * PAGE], vt)
    acc_ref[...] = acc
    m_ref[...] = m_new

    @pl.when(t == pl.num_programs(1) - 1)
    def _():
        o_ref[0] = acc_ref[...] / l_ref[...]


def _dsa_decode(q2, kvb, bias, bias_new, page_table, kv_t, g, layer):
    bd, n_pages = page_table.shape
    steps = n_pages // g

    def page_map(gi):
        return lambda b, t, pt: (layer, pt[b, t * g + gi], 0, 0)

    per_b = lambda b, t, pt: (b, 0, 0)
    out = pl.pallas_call(
        functools.partial(_dsa_dec_body, g=g), name="dsa_decode",
        grid_spec=pltpu.PrefetchScalarGridSpec(
            num_scalar_prefetch=1, grid=(bd, steps),
            in_specs=[pl.BlockSpec((1, H_B, LANES), per_b), pl.BlockSpec((1, 1, 2 * LANES), per_b),
                      pl.BlockSpec((1, 1, 1), per_b),
                      pl.BlockSpec((1, g, PAGE), lambda b, t, pt: (b, t, 0))]
                     + [pl.BlockSpec((1, 1, 2 * LANES, PAGE), page_map(gi)) for gi in range(g)],
            out_specs=pl.BlockSpec((1, H_B, LANES), per_b),
            scratch_shapes=[pltpu.VMEM((H_B, 1), F32), pltpu.VMEM((H_B, 1), F32),
                            pltpu.VMEM((H_B, LANES), F32)]),
        out_shape=jax.ShapeDtypeStruct((bd, H_B, LANES), F32),
        compiler_params=_cparams(("parallel", "arbitrary")))(
            page_table, q2, kvb.reshape(bd, 1, 2 * LANES), bias_new.reshape(bd, 1, 1),
            bias.reshape(bd, n_pages, PAGE), *([kv_t] * g))
    o4 = out.reshape(bd, KV_B, G_B, KV_B, DH)
    return jnp.stack([o4[:, n, :, n] for n in range(KV_B)], axis=1).reshape(bd, W_B)


def _conv_dec_body(st_ref, u_ref, w_ref, cb_ref, g_ref, b_ref, o_ref):
    acc = u_ref[...] * w_ref[CONV_W - 1:CONV_W, :] + cb_ref[...]
    for t in range(CONV_W - 1):
        acc = acc + st_ref[0, t] * w_ref[t:t + 1, :]
    o_ref[...] = _conv_ln_silu(acc, g_ref[...], b_ref[...])


def _conv_decode(state_t, u, conv_w, conv_b, g, bb, layer):
    bd = u.shape[0]
    full = lambda a: pl.BlockSpec(a.shape, lambda i: (0,) * a.ndim)
    return pl.pallas_call(
        _conv_dec_body, grid=(1,), name="conv_decode",
        in_specs=[pl.BlockSpec((1,) + state_t.shape[1:], lambda i: (layer, 0, 0, 0)),
                  full(u), full(conv_w), full(conv_b), full(g), full(bb)],
        out_specs=pl.BlockSpec((bd, D_C), lambda i: (0, 0)),
        out_shape=jax.ShapeDtypeStruct((bd, D_C), F32),
        compiler_params=_cparams(("arbitrary",)))(state_t, u, conv_w, conv_b, g, bb)


def _rope_tables(pos):
    half = ROT_DIM // 2
    inv = ROPE_THETA ** (-jnp.arange(half, dtype=F32) / half)
    ang = pos.astype(F32)[:, None] * inv[None, :]
    cos, sin = jnp.cos(ang), jnp.sin(ang)
    n = pos.shape[0]
    pad = jnp.zeros((n, DH - ROT_DIM), F32)
    c = jnp.concatenate([cos, cos, pad + 1.0], axis=1)
    s1 = jnp.concatenate([jnp.zeros_like(sin), sin, pad], axis=1)
    s2 = jnp.concatenate([-sin, jnp.zeros_like(sin), pad], axis=1)
    return tuple(jnp.tile(t, (1, LANES // DH)) for t in (c, s1, s2))


def _unspread(q_pad, half_of_head):
    q = q_pad.reshape(q_pad.shape[0], H_A, 2, DH)
    return jnp.stack([q[:, h, half_of_head(h)] for h in range(H_A)], axis=1)


def _regroup_weights(w_in, b_f):
    w = w_in.astype(BF16)
    d = w.shape[0]
    z = lambda n: jnp.zeros((d, n), BF16)
    foxq = jnp.concatenate([w[:, O_QA:O_KA], w[:, O_FA:O_GA], z(LANES - H_A)], axis=1)
    foxkv = w[:, O_KA:O_FA]
    gates = jnp.concatenate([w[:, O_GA:O_QB], w[:, O_GB:O_GLU], w[:, O_GC:O_MG]], axis=1)
    wqb = w[:, O_QB:O_KB].reshape(d, KV_B, G_B, DH)
    wqb = jnp.swapaxes(wqb, 1, 2).reshape(d, W_B)
    dsaq = jnp.concatenate([wqb, w[:, O_QI:O_KI]], axis=1)
    dsak = jnp.concatenate([w[:, O_KB:O_QI], w[:, O_KI:O_GB], z(LANES - D_I - H_I)], axis=1)
    glu = w[:, O_GLU:O_GC]
    wt = jnp.transpose(jnp.concatenate([w[:, O_VB:O_QI], w[:, O_KA:O_FA]], axis=1))
    foxk = w[:, O_KA:O_VA]
    bf = jnp.concatenate([b_f.astype(F32), jnp.zeros((LANES - H_A,), F32)])[None, :]
    return dict(foxq=foxq, foxkv=foxkv, foxk=foxk, gates=gates, dsaq=dsaq, dsak=dsak, glu=glu, t=wt,
                b_f=bf,
                w_mg=w[:, O_MG:O_END])


def _pick(n, prefs):
    for t in prefs:
        if n % t == 0:
            return t
    return n


def kernel(x_prompt, x_sample, cache_fox_kv, cache_fox_logf, cache_dsa_kv, cache_dsa_idx_k, state_conv,
           page_table, p_prompt, p_sample, w_in, b_f, w_a, w_b, w_c, w_out, ln_g, ln_b,
           conv_w, conv_b, conv_ln_g, conv_ln_b, w_pe, w_pe_gate):
    depth = w_in.shape[0]
    b, s, d = x_prompt.shape
    bd = x_sample.shape[0]
    n_phys = cache_fox_kv.shape[1]
    n_pages = page_table.shape[1]
    p_len = n_pages * PAGE
    alpha = float((2 * depth) ** 0.25)
    tm = _pick(s, (256, 128))
    tq = _pick(s, (256, 128))
    g_fox = _pick(n_pages, (32, 16, 8, 4, 2, 1))
    g_dsa = _pick(n_pages, (64, 32, 16, 8, 4, 2, 1))
    g_idx = _pick(n_pages, (64, 32, 16, 8, 4, 2, 1))
    n_top_dec = min(TOPK_MAX, (p_len + 1) // 4)

    tabs_p = _rope_tables(jnp.arange(s, dtype=I32))
    tabs_s = _rope_tables(jnp.full((bd,), p_len, I32))
    page_table = page_table.astype(I32)

    fox_kv_t = jnp.transpose(cache_fox_kv, (0, 1, 3, 4, 5, 2)).reshape(depth, n_phys, 2 * W_A, PAGE)
    fox_lf_t = jnp.transpose(cache_fox_logf, (0, 1, 3, 2))
    dsa_kv_t = jnp.transpose(cache_dsa_kv, (0, 1, 3, 4, 5, 2)).reshape(depth, n_phys, 2 * LANES, PAGE)
    dsa_ik_t = jnp.transpose(cache_dsa_idx_k, (0, 1, 3, 2))
    state_t = jnp.transpose(state_conv, (0, 2, 1, 3))
    head_eye = jnp.repeat(jnp.eye(H_A, dtype=BF16), DH, axis=1)

    hp = x_prompt.reshape(b * s, d)
    hs = x_sample.reshape(bd, d)
    hp_bf = hp.astype(BF16)
    hs_bf = hs.astype(BF16)
    outs_p, outs_s = [], []
    kvat = jnp.zeros((depth, b, 2 * W_A, s), F32)
    for l in range(depth):
        wg = _regroup_weights(w_in[l], b_f[l])
        wl = dict(w_mg=wg['w_mg'], w_a=w_a[l].astype(BF16), w_b=w_b[l].astype(BF16),
                  w_c=w_c[l].astype(BF16), w_out=w_out[l].astype(BF16), w_pe=w_pe[l].astype(BF16),
                  w_pe_gate=w_pe_gate[l].astype(BF16), ln_g=ln_g[l][None, :], ln_b=ln_b[l][None, :])
        cw, cb = conv_w[l], conv_b[l][None, :]
        cg, cbb = conv_ln_g[l][None, :], conv_ln_b[l][None, :]

        pp = _project(hp_bf, wg, tabs_p, tm, s // tm, stacked=(kvat, l, b, s))
        vbt, vat, kvat = pp['vbt'], pp['vat'], pp['kvat']
        logf3 = pp['logf'].reshape(b, s, H_A)
        cum = _cumsum(logf3, _pick(s, (1024, 512, 256, 128)))
        cum_t = jnp.swapaxes(cum, 1, 2)
        wi_t = jnp.swapaxes(pp['wi'].reshape(b, s, H_I), 1, 2)
        ya = _fox_prompt(pp['q_pad'], pp['k_bf'], vat, cum, cum_t, b, s, tq)
        yb = _dsa_prompt(pp['qb_pad'], pp['qi_pad'], wi_t, pp['ki_pad'], pp['kvb_bf'], vbt, b, s, tq)
        yc = _conv_prompt(pp['u'], cw, cb, cg, cbb, b, s, tq)
        hp_new, hp_bf = _merge(hp, p_prompt[l].reshape(b * s, P_DIM), ya, yb, yc, pp['sg'], wl,
                               _pick(b * s, (512, 256, 128)), alpha)
        outs_p.append((None, logf3,
                       pp['kvb'].reshape(b, s, 2, KV_B, DH), pp['ki'].reshape(b, s, D_I),
                       pp['u'].reshape(b, s, D_C)[:, s - (CONV_W - 1):]))
        hp = hp_new

        ps = _project(hs_bf, wg, tabs_s, bd, 1)
        q_s = _unspread(ps['q_pad'], lambda h: h % 2).reshape(bd, 1, W_A)
        qrows = q_s * head_eye[None]
        ya_s = _fox_decode(qrows, ps['kva'], ps['logf'], page_table, fox_kv_t, fox_lf_t, g_fox, l)
        qi_s = _unspread(ps['qi_pad'], lambda h: h % 2)
        sc, sc_new = _idx_decode(qi_s, ps['wi'], ps['ki_pad'][:, :D_I], page_table, dsa_ik_t,
                                 g_idx, l)
        bias, bias_new = _sel_decode(sc, sc_new, n_top_dec)
        yb_s = _dsa_decode(ps['qb_pad'].reshape(bd, H_B, LANES), ps['kvb'], bias, bias_new,
                           page_table, dsa_kv_t, g_dsa, l)
        yc_s = _conv_decode(state_t, ps['u'], cw, cb, cg, cbb, l)
        hs_new, hs_bf = _merge(hs, p_sample[l].reshape(bd, P_DIM), ya_s, yb_s, yc_s, ps['sg'], wl,
                               bd, alpha)
        outs_s.append((ps['kva'].reshape(bd, 1, 2, H_A, DH), ps['logf'].reshape(bd, 1, H_A),
                       ps['kvb'].reshape(bd, 1, 2, KV_B, DH), ps['ki'].reshape(bd, 1, D_I),
                       jnp.concatenate([state_t[l, 1:], ps['u'][None]], axis=0)))
        hs = hs_new

    stack = lambda outs, k: jnp.stack([o[k] for o in outs])
    fox_kv_prompt = jnp.transpose(kvat.reshape(depth, b, 2, H_A, DH, s), (0, 1, 5, 2, 3, 4))
    return (hp.reshape(b, s, d), hs.reshape(bd, 1, d),
            fox_kv_prompt, stack(outs_s, 0), stack(outs_p, 1), stack(outs_s, 1),
            stack(outs_p, 2), stack(outs_s, 2), stack(outs_p, 3), stack(outs_s, 3),
            stack(outs_p, 4), jnp.transpose(stack(outs_s, 4), (0, 2, 1, 3)))
```
